```python
import math
import jax, jax.numpy as jnp
from jax import lax
import numpy as np

D_MODEL = 4096
BATCH = 2
SEQ = 4096
DEPTH = 2

CHUNK = 64
Q_BLOCK = 128
EPS = 1e-6
A_HEADS = 8
A_DH = 128
A_WIDTH = A_HEADS * 2 * A_DH
B_HEADS = 4
B_DK = 128
B_DV = 256
B_WIDTH = B_HEADS * B_DV
GATE_RANK = 16
GATE_TAU = 16.0
POOL_WINDOWS = (2, 4, 8, 16)
N_POOL = len(POOL_WINDOWS)
POOL_CH = 256
C_WIDTH = N_POOL * POOL_CH
MIX_WIDTH = A_WIDTH + B_WIDTH + C_WIDTH
IN_SIZES = (A_WIDTH, A_WIDTH, A_WIDTH,
            B_HEADS * B_DK, B_HEADS * B_DK,
            B_WIDTH, B_WIDTH,
            GATE_RANK,
            C_WIDTH)
IN_COLS = int(sum(IN_SIZES))
IN_SPLITS = [int(s) for s in np.cumsum(IN_SIZES)[:-1]]
D_FF = 11008
CONV_K = 3

kernel_name = "hybrid_diffattn_gla_pool_convffn"


def rmsnorm(x, g):
    xf = x.astype(jnp.float32)
    y = xf * lax.rsqrt(jnp.mean(xf * xf, axis=-1, keepdims=True) + EPS)
    return (y * g.astype(jnp.float32)).astype(x.dtype)


def alibi_slopes(n_heads):
    return 2.0 ** (-8.0 * jnp.arange(1, n_heads + 1, dtype=jnp.float32) / n_heads)


def diff_attention(q, k, v, lam):
    B, S, H = q.shape[0], q.shape[1], q.shape[2]
    E = v.shape[-1]
    scale = 1.0 / math.sqrt(A_DH)
    slopes = alibi_slopes(H)
    kpos = jnp.arange(S)
    kchunk = kpos // CHUNK
    k1, k2 = k[..., 0, :], k[..., 1, :]

    def block(i):
        start = i * Q_BLOCK
        qb = lax.dynamic_slice_in_dim(q, start, Q_BLOCK, axis=1)
        qpos = start + jnp.arange(Q_BLOCK)
        allowed = kchunk[None, :] <= (qpos // CHUNK)[:, None]
        dist = jnp.abs(qpos[:, None] - kpos[None, :]).astype(jnp.float32)
        bias = jnp.where(allowed[None], -slopes[:, None, None] * dist[None], -jnp.inf)

        def probs(qh, kh):
            s = jnp.einsum('bqhd,bkhd->bhqk', qh, kh).astype(jnp.float32) * scale + bias
            return jax.nn.softmax(s, axis=-1)

        attn = probs(qb[..., 0, :], k1) - lam * probs(qb[..., 1, :], k2)
        return jnp.einsum('bhqk,bkhe->bqhe', attn.astype(v.dtype), v)

    out = lax.map(block, jnp.arange(S // Q_BLOCK))
    return out.transpose(1, 0, 2, 3, 4).reshape(B, S, H, E)


def gla(q, k, v, log_a):
    dtype = v.dtype
    B, S, H, DK = q.shape
    DV = v.shape[-1]
    nc = S // CHUNK

    def to_chunks(t):
        return t.astype(jnp.float32).reshape(B, nc, CHUNK, H, t.shape[-1]).transpose(1, 0, 3, 2, 4)

    causal = jnp.tril(jnp.ones((CHUNK, CHUNK), dtype=bool))

    def step(state, inp):
        qc, kc, vc, gc = inp
        b = jnp.cumsum(gc, axis=2)
        diff = b[:, :, :, None, :] - b[:, :, None, :, :]
        decay = jnp.exp(jnp.where(causal[None, None, :, :, None], diff, -jnp.inf))
        a = jnp.einsum('bhtd,bhsd,bhtsd->bhts', qc, kc, decay)
        o = jnp.einsum('bhts,bhse->bhte', a, vc) + jnp.einsum('bhtd,bhde->bhte', qc * jnp.exp(b), state)
        b_last = b[:, :, -1:, :]
        new_state = jnp.exp(b_last[:, :, 0, :])[..., None] * state + \
            jnp.einsum('bhsd,bhse->bhde', kc * jnp.exp(b_last - b), vc)
        return new_state, o

    init = jnp.zeros((B, H, DK, DV), jnp.float32)
    _, o = lax.scan(step, init, (to_chunks(q), to_chunks(k), to_chunks(v), to_chunks(log_a)))
    return o.transpose(1, 0, 3, 2, 4).reshape(B, S, H, DV).astype(dtype)


def pool_mixer(u, w_pool, scale):
    B, S, _ = u.shape
    uf = u.astype(jnp.float32).reshape(B, S, N_POOL, POOL_CH)
    cs = jnp.concatenate([jnp.zeros((B, 1, N_POOL, POOL_CH), jnp.float32),
                          jnp.cumsum(uf, axis=1)], axis=1)
    hi = jnp.arange(1, S + 1)
    outs = []
    for g, w in enumerate(POOL_WINDOWS):
        lo = jnp.maximum(hi - w, 0)
        cnt = (hi - lo).astype(jnp.float32)
        win_sum = cs[:, 1:, g, :] - jnp.take(cs[:, :, g, :], lo, axis=1)
        outs.append(win_sum / cnt[None, :, None] - uf[:, :, g, :])
    d = jnp.stack(outs, axis=2).astype(u.dtype)
    y = jnp.einsum('bsgc,gcd->bsgd', d, w_pool).reshape(B, S, C_WIDTH)
    return y * scale


def causal_dwconv(z, w, b):
    S = z.shape[1]
    zp = jnp.pad(z, ((0, 0), (CONV_K - 1, 0), (0, 0)))
    y = b
    for t in range(CONV_K):
        y = y + zp[:, t:t + S] * w[t]
    return y


def setup_inputs(seed: int = 0) -> dict:
    key = jax.random.key(seed)
    ks = jax.random.split(key, 24)
    f32 = jnp.float32
    nrm = lambda k, shape, s: jax.random.normal(k, shape, f32) * s
    gain = lambda k, shape: 1.0 + 0.02 * jax.random.normal(k, shape, f32)
    L = DEPTH
    return {
        "x": nrm(ks[0], (BATCH, SEQ, D_MODEL), 1.0),
        "w_in": nrm(ks[1], (L, D_MODEL, IN_COLS), D_MODEL ** -0.5),
        "w_gate_lr2": nrm(ks[2], (L, GATE_RANK, B_HEADS * B_DK), GATE_RANK ** -0.5),
        "b_gate": nrm(ks[3], (L, B_HEADS * B_DK), 0.1),
        "lam_q1": nrm(ks[4], (L, A_DH), 0.1),
        "lam_k1": nrm(ks[5], (L, A_DH), 0.1),
        "lam_q2": nrm(ks[6], (L, A_DH), 0.1),
        "lam_k2": nrm(ks[7], (L, A_DH), 0.1),
        "g_subln": gain(ks[8], (L, 2 * A_DH)),
        "g_gla": gain(ks[9], (L, B_WIDTH)),
        "w_pool": nrm(ks[10], (L, N_POOL, POOL_CH, POOL_CH), POOL_CH ** -0.5),
        "pool_scale": gain(ks[11], (L, C_WIDTH)),
        "w_o": nrm(ks[12], (L, MIX_WIDTH, D_MODEL), MIX_WIDTH ** -0.5),
        "w_up": nrm(ks[13], (L, D_MODEL, 2 * D_FF), D_MODEL ** -0.5),
        "w_conv": nrm(ks[14], (L, CONV_K, 2 * D_FF), CONV_K ** -0.5),
        "b_conv": nrm(ks[15], (L, 2 * D_FF), 0.01),
        "w_down": nrm(ks[16], (L, D_FF, D_MODEL), D_FF ** -0.5),
        "g_pre_mix": gain(ks[17], (L, D_MODEL)),
        "g_post_mix": gain(ks[18], (L, D_MODEL)),
        "g_pre_ffn": gain(ks[19], (L, D_MODEL)),
        "g_post_ffn": gain(ks[20], (L, D_MODEL)),
    }


def reference(x, w_in, w_gate_lr2, b_gate, lam_q1, lam_k1, lam_q2, lam_k2, g_subln, g_gla,
              w_pool, pool_scale, w_o, w_up, w_conv, b_conv, w_down,
              g_pre_mix, g_post_mix, g_pre_ffn, g_post_ffn):
    B, S, _ = x.shape
    h = x
    for l in range(DEPTH):
        hn = rmsnorm(h, g_pre_mix[l])
        u = hn @ w_in[l]
        qa, ka, va, qb, kb, vb, gb, zlr, uc = jnp.split(u, IN_SPLITS, axis=-1)

        lam_init = 0.8 - 0.6 * math.exp(-0.3 * l)
        lam = (jnp.exp(jnp.sum(lam_q1[l].astype(jnp.float32) * lam_k1[l].astype(jnp.float32)))
               - jnp.exp(jnp.sum(lam_q2[l].astype(jnp.float32) * lam_k2[l].astype(jnp.float32)))
               + lam_init)
        oa = diff_attention(qa.reshape(B, S, A_HEADS, 2, A_DH),
                            ka.reshape(B, S, A_HEADS, 2, A_DH),
                            va.reshape(B, S, A_HEADS, 2 * A_DH), lam)
        oa = (rmsnorm(oa, g_subln[l]) * (1.0 - lam_init)).reshape(B, S, A_WIDTH)

        log_a = jax.nn.log_sigmoid(
            (zlr @ w_gate_lr2[l] + b_gate[l]).astype(jnp.float32)) / GATE_TAU
        ob = gla(qb.reshape(B, S, B_HEADS, B_DK) * (B_DK ** -0.5),
                 kb.reshape(B, S, B_HEADS, B_DK),
                 vb.reshape(B, S, B_HEADS, B_DV),
                 log_a.reshape(B, S, B_HEADS, B_DK))
        ob = rmsnorm(ob, g_gla[l].reshape(B_HEADS, B_DV)).reshape(B, S, B_WIDTH) * jax.nn.silu(gb)

        oc = pool_mixer(uc, w_pool[l], pool_scale[l])

        mix = jnp.concatenate([oa, ob, oc], axis=-1) @ w_o[l]
        h = h + rmsnorm(mix, g_post_mix[l])

        hn = rmsnorm(h, g_pre_ffn[l])
        z = causal_dwconv(hn @ w_up[l], w_conv[l], b_conv[l])
        gate, val = jnp.split(z, 2, axis=-1)
        f = (jax.nn.gelu(gate, approximate=True) * val) @ w_down[l]
        h = h + rmsnorm(f, g_post_ffn[l])
    return h
```

```python
import functools
import math

import jax
import jax.numpy as jnp
from jax import lax
from jax.experimental import pallas as pl
from jax.experimental.pallas import tpu as pltpu

F32 = jnp.float32
BF16 = jnp.bfloat16

CHUNK = 64
EPS = 1e-6
A_HEADS = 8
A_DH = 128
A_WIDTH = A_HEADS * 2 * A_DH
B_HEADS = 4
B_DK = 128
B_DV = 256
B_WIDTH = B_HEADS * B_DV
GATE_RANK = 16
GATE_TAU = 16.0
POOL_WINDOWS = (2, 4, 8, 16)
N_POOL = len(POOL_WINDOWS)
POOL_CH = 256
C_WIDTH = N_POOL * POOL_CH
CONV_K = 3

QA0 = 0
KA0 = A_WIDTH
VA0 = 2 * A_WIDTH
QB0 = 3 * A_WIDTH
KB0 = QB0 + B_HEADS * B_DK
VB0 = KB0 + B_HEADS * B_DK
GB0 = VB0 + B_WIDTH
UC0 = GB0 + B_WIDTH
U_COLS = UC0 + C_WIDTH

LANES = 128
MXU_COLS = 256
VMEM_LIMIT = 56 * 1024 * 1024
POOL_HALO = 16
NEG_BIG = -1e30


def _cparams(*sem):
    return pltpu.CompilerParams(dimension_semantics=sem, vmem_limit_bytes=VMEM_LIMIT)


def _dot(a, b):
    return jnp.dot(a, b, preferred_element_type=F32)


def _dot_nt(a, b):
    return lax.dot_general(a, b, (((1,), (1,)), ((), ())), preferred_element_type=F32)


def _dot_tn(a, b):
    return lax.dot_general(a, b, (((0,), (0,)), ((), ())), preferred_element_type=F32)


def _rms(x, g):
    ms = jnp.mean(x * x, axis=-1, keepdims=True)
    return x * lax.rsqrt(ms + EPS) * g


def _tile(n, want):
    t = min(n, want)
    assert n % t == 0, (n, want)
    return t


def _norm_first_kernel(h_ref, gpre_ref, hn_ref):
    hn_ref[...] = _rms(h_ref[...], gpre_ref[...]).astype(BF16)


def _resnorm_kernel(h_ref, br_ref, gpost_ref, gpre_ref, hout_ref, hn_ref):
    h = h_ref[...] + _rms(br_ref[...], gpost_ref[...])
    hout_ref[...] = h
    hn_ref[...] = _rms(h, gpre_ref[...]).astype(BF16)


def _resnorm_last_kernel(h_ref, br_ref, gpost_ref, hout_ref):
    hout_ref[...] = h_ref[...] + _rms(br_ref[...], gpost_ref[...])


def _norm_first(h, gpre):
    m, d = h.shape
    tr = _tile(m, 256)
    row = pl.BlockSpec((tr, d), lambda i: (i, 0))
    vec = pl.BlockSpec((1, d), lambda i: (0, 0))
    return pl.pallas_call(
        _norm_first_kernel, grid=(m // tr,), in_specs=[row, vec], out_specs=row,
        out_shape=jax.ShapeDtypeStruct((m, d), BF16),
        compiler_params=_cparams("parallel"), name="norm_first",
    )(h, gpre.reshape(1, d))


def _resnorm(h, br, gpost, gpre):
    m, d = h.shape
    tr = _tile(m, 256)
    row = pl.BlockSpec((tr, d), lambda i: (i, 0))
    vec = pl.BlockSpec((1, d), lambda i: (0, 0))
    return pl.pallas_call(
        _resnorm_kernel, grid=(m // tr,), in_specs=[row, row, vec, vec], out_specs=[row, row],
        out_shape=[jax.ShapeDtypeStruct((m, d), F32), jax.ShapeDtypeStruct((m, d), BF16)],
        compiler_params=_cparams("parallel"), name="resnorm",
    )(h, br, gpost.reshape(1, d), gpre.reshape(1, d))


def _resnorm_last(h, br, gpost):
    m, d = h.shape
    tr = _tile(m, 256)
    row = pl.BlockSpec((tr, d), lambda i: (i, 0))
    vec = pl.BlockSpec((1, d), lambda i: (0, 0))
    return pl.pallas_call(
        _resnorm_last_kernel, grid=(m // tr,), in_specs=[row, row, vec], out_specs=row,
        out_shape=jax.ShapeDtypeStruct((m, d), F32),
        compiler_params=_cparams("parallel"), name="resnorm_last",
    )(h, br, gpost.reshape(1, d))


def _mm_in_kernel(x_ref, w_ref, wz_ref, u_ref, z_ref):
    x = x_ref[...]
    u_ref[...] = _dot(x, w_ref[...]).astype(BF16)

    @pl.when(pl.program_id(1) == 0)
    def _():
        z_ref[...] = _dot(x, wz_ref[...])


def _mm_in(hn, w_main, w_zlr):
    m, d = hn.shape
    n = w_main.shape[1]
    tm = _tile(m, 1024)
    tn = _tile(n, 1024)
    return pl.pallas_call(
        _mm_in_kernel, grid=(m // tm, n // tn),
        in_specs=[pl.BlockSpec((tm, d), lambda i, j: (i, 0)),
                  pl.BlockSpec((d, tn), lambda i, j: (0, j)),
                  pl.BlockSpec((d, LANES), lambda i, j: (0, 0))],
        out_specs=[pl.BlockSpec((tm, tn), lambda i, j: (i, j)),
                   pl.BlockSpec((tm, LANES), lambda i, j: (i, 0))],
        out_shape=[jax.ShapeDtypeStruct((m, n), BF16), jax.ShapeDtypeStruct((m, LANES), F32)],
        compiler_params=_cparams("parallel", "arbitrary"), name="mm_in",
    )(hn, w_main, w_zlr)


def _att_kernel(lq1_ref, lk1_ref, lq2_ref, lk2_ref, g_ref, q_ref, k_ref, v_ref, o_ref,
                m1_ref, l1_ref, acc1_ref, m2_ref, l2_ref, acc2_ref, *, lam_init, t):
    head = pl.program_id(1)
    qi = pl.program_id(2)
    scale = 1.0 / math.sqrt(A_DH)
    slope = lax.bitcast_convert_type(
        jnp.full((1, t), (126 - head) << 23, jnp.int32), F32)

    q = q_ref[...]
    q1 = (q[:, :A_DH].astype(F32) * scale).astype(BF16)
    q2 = (q[:, A_DH:].astype(F32) * scale).astype(BF16)

    def step(qh, kh, v, bias, mask, m_ref, l_ref, acc_ref, first):
        s = _dot_nt(qh, kh) + bias
        if mask is not None:
            s = jnp.where(mask, s, NEG_BIG)
        if first:
            m_new = jnp.max(s, axis=-1, keepdims=True)
            p = jnp.exp(s - m_new)
            l_ref[...] = jnp.sum(p, axis=-1, keepdims=True)
            acc_ref[...] = _dot(p.astype(BF16), v)
        else:
            m_old = m_ref[...]
            m_new = jnp.maximum(m_old, jnp.max(s, axis=-1, keepdims=True))
            alpha = jnp.exp(m_old - m_new)
            p = jnp.exp(s - m_new)
            l_ref[...] = alpha * l_ref[...] + jnp.sum(p, axis=-1, keepdims=True)
            acc_ref[...] = alpha * acc_ref[...] + _dot(p.astype(BF16), v)
        m_ref[...] = m_new

    r = lax.broadcasted_iota(jnp.int32, (t, t), 0)
    c = lax.broadcasted_iota(jnp.int32, (t, t), 1)
    shift = CHUNK.bit_length() - 1
    allowed = (c >> shift) <= (r >> shift)
    bias_d = slope * (r - jnp.abs(r - c)).astype(F32)
    d0 = pl.multiple_of(qi * t, t)
    kd = k_ref[pl.ds(d0, t), :]
    vd = v_ref[pl.ds(d0, t), :]
    step(q1, kd[:, :A_DH], vd, bias_d, allowed, m1_ref, l1_ref, acc1_ref, True)
    step(q2, kd[:, A_DH:], vd, bias_d, allowed, m2_ref, l2_ref, acc2_ref, True)

    col = lax.broadcasted_iota(jnp.int32, (1, t), 1)

    def body(ki, carry):
        k0 = pl.multiple_of(ki * t, t)
        kt = k_ref[pl.ds(k0, t), :]
        vt = v_ref[pl.ds(k0, t), :]
        bias = slope * (col + (ki - qi) * t).astype(F32)
        step(q1, kt[:, :A_DH], vt, bias, None, m1_ref, l1_ref, acc1_ref, False)
        step(q2, kt[:, A_DH:], vt, bias, None, m2_ref, l2_ref, acc2_ref, False)
        return carry

    lax.fori_loop(0, qi, body, 0)

    lam = (jnp.exp(jnp.sum(lq1_ref[...] * lk1_ref[...], axis=-1, keepdims=True))
           - jnp.exp(jnp.sum(lq2_ref[...] * lk2_ref[...], axis=-1, keepdims=True))
           + lam_init)
    o = acc1_ref[...] / l1_ref[...] - lam * (acc2_ref[...] / l2_ref[...])
    o_ref[...] = (_rms(o, g_ref[...]) * (1.0 - lam_init)).astype(BF16)


def _att(u, lq1, lk1, lq2, lk2, g_subln, lam_init, batch, seq):
    m = u.shape[0]
    t = _tile(seq, 256)
    nq = seq // t
    e = 2 * A_DH
    vec = pl.BlockSpec((1, A_DH), lambda b, h, i: (0, 0))
    return pl.pallas_call(
        functools.partial(_att_kernel, lam_init=lam_init, t=t),
        grid=(batch, A_HEADS, nq),
        in_specs=[vec, vec, vec, vec,
                  pl.BlockSpec((1, e), lambda b, h, i: (0, 0)),
                  pl.BlockSpec((t, e), lambda b, h, i: (b * nq + i, QA0 // e + h)),
                  pl.BlockSpec((seq, e), lambda b, h, i: (b, KA0 // e + h)),
                  pl.BlockSpec((seq, e), lambda b, h, i: (b, VA0 // e + h))],
        out_specs=pl.BlockSpec((t, e), lambda b, h, i: (b * nq + i, h)),
        out_shape=jax.ShapeDtypeStruct((m, A_WIDTH), BF16),
        scratch_shapes=[pltpu.VMEM((t, 1), F32), pltpu.VMEM((t, 1), F32), pltpu.VMEM((t, e), F32),
                        pltpu.VMEM((t, 1), F32), pltpu.VMEM((t, 1), F32), pltpu.VMEM((t, e), F32)],
        compiler_params=_cparams("parallel", "parallel", "arbitrary"), name="diff_attn",
    )(lq1.reshape(1, A_DH), lk1.reshape(1, A_DH), lq2.reshape(1, A_DH), lk2.reshape(1, A_DH),
      g_subln.reshape(1, e), u, u, u)


def _gla_kernel(z_ref, wg_ref, bg_ref, gn_ref, q_ref, k_ref, v_ref, gb_ref, o_ref, *, seq):
    L = CHUNK
    wg = wg_ref[...]
    bg = bg_ref[...]
    gn = gn_ref[...]
    rows = lax.broadcasted_iota(jnp.int32, (L, B_DK), 0)
    ti = lax.broadcasted_iota(jnp.int32, (L, L), 0)
    si = lax.broadcasted_iota(jnp.int32, (L, L), 1)
    qscale = B_DK ** -0.5

    def chunk(c, state_t):
        r0 = pl.multiple_of(c * L, L)
        x = _dot(z_ref[pl.ds(r0, L), :].astype(BF16), wg) + bg
        la = (jnp.minimum(x, 0.0) - jnp.log(1.0 + jnp.exp(-jnp.abs(x)))) * (1.0 / GATE_TAU)
        b = la
        sh = 1
        while sh < L:
            b = b + jnp.where(rows >= sh, pltpu.roll(b, sh, axis=0), 0.0)
            sh *= 2
        b_last = b[L - 1:L, :]
        qf = q_ref[pl.ds(r0, L), :].astype(F32) * qscale
        kc = k_ref[pl.ds(r0, L), :]
        vc = v_ref[pl.ds(r0, L), :]

        a = jnp.zeros((L, L), F32)
        for s in range(L):
            e = jnp.exp(jnp.minimum(b - b[s:s + 1, :], 0.0))
            rs = _dot_nt((qf * e).astype(BF16), kc)
            a = jnp.where(si == s, rs, a)
        a = jnp.where(ti >= si, a, 0.0)

        qe = (qf * jnp.exp(b)).astype(BF16)
        o = _dot(a.astype(BF16), vc) + _dot_nt(qe, state_t.astype(BF16))
        ke = (kc.astype(F32) * jnp.exp(b_last - b)).astype(BF16)
        new_state = jnp.exp(b_last) * state_t + _dot_tn(vc, ke)

        gate = gb_ref[pl.ds(r0, L), :].astype(F32)
        y = _rms(o, gn) * (gate / (1.0 + jnp.exp(-gate)))
        o_ref[pl.ds(r0, L), :] = y.astype(BF16)
        return new_state

    lax.fori_loop(0, seq // L, chunk, jnp.zeros((B_DV, B_DK), F32))


def _gla(u, z, w_gate, b_gate, g_gla, batch, seq):
    m = u.shape[0]
    return pl.pallas_call(
        functools.partial(_gla_kernel, seq=seq),
        grid=(batch, B_HEADS),
        in_specs=[pl.BlockSpec((seq, LANES), lambda b, h: (b, 0)),
                  pl.BlockSpec((LANES, B_DK), lambda b, h: (0, h)),
                  pl.BlockSpec((1, B_DK), lambda b, h: (0, h)),
                  pl.BlockSpec((1, B_DV), lambda b, h: (0, h)),
                  pl.BlockSpec((seq, B_DK), lambda b, h: (b, QB0 // B_DK + h)),
                  pl.BlockSpec((seq, B_DK), lambda b, h: (b, KB0 // B_DK + h)),
                  pl.BlockSpec((seq, B_DV), lambda b, h: (b, VB0 // B_DV + h)),
                  pl.BlockSpec((seq, B_DV), lambda b, h: (b, GB0 // B_DV + h))],
        out_specs=pl.BlockSpec((seq, B_DV), lambda b, h: (b, h)),
        out_shape=jax.ShapeDtypeStruct((m, B_WIDTH), BF16),
        compiler_params=_cparams("parallel", "parallel"), name="gla",
    )(z, w_gate, b_gate.reshape(1, -1), g_gla.reshape(1, -1), u, u, u, u)


def _pool_kernel(u_ref, w_ref, sc_ref, o_ref, carry_ref, *, tr):
    ti = pl.program_id(1)

    @pl.when(ti == 0)
    def _():
        carry_ref[...] = jnp.zeros_like(carry_ref)

    x = u_ref[...].astype(F32)
    ext = jnp.concatenate([carry_ref[...], x], axis=0)
    carry_ref[...] = x[tr - POOL_HALO:, :]
    pos = ti * tr + lax.broadcasted_iota(jnp.int32, (tr, 1), 0)
    for g, w in enumerate(POOL_WINDOWS):
        lo, hi = g * POOL_CH, (g + 1) * POOL_CH
        s = ext[:, lo:hi]
        sh = 1
        while sh < w:
            s = s + pltpu.roll(s, sh, axis=0)
            sh *= 2
        cnt = jnp.minimum(pos + 1, w).astype(F32)
        d = s[POOL_HALO:, :] / cnt - x[:, lo:hi]
        y = _dot(d.astype(BF16), w_ref[g]) * sc_ref[:, lo:hi]
        o_ref[:, lo:hi] = y.astype(BF16)


def _pool(u, w_pool, pool_scale, batch, seq):
    m = u.shape[0]
    tr = _tile(seq, 512)
    nt = seq // tr
    assert tr >= POOL_HALO and max(POOL_WINDOWS) <= POOL_HALO
    return pl.pallas_call(
        functools.partial(_pool_kernel, tr=tr),
        grid=(batch, nt),
        in_specs=[pl.BlockSpec((tr, C_WIDTH), lambda b, i: (b * nt + i, UC0 // C_WIDTH)),
                  pl.BlockSpec((N_POOL, POOL_CH, POOL_CH), lambda b, i: (0, 0, 0)),
                  pl.BlockSpec((1, C_WIDTH), lambda b, i: (0, 0))],
        out_specs=pl.BlockSpec((tr, C_WIDTH), lambda b, i: (b * nt + i, 0)),
        out_shape=jax.ShapeDtypeStruct((m, C_WIDTH), BF16),
        scratch_shapes=[pltpu.VMEM((POOL_HALO, C_WIDTH), F32)],
        compiler_params=_cparams("parallel", "arbitrary"), name="pool",
    )(u, w_pool, pool_scale.reshape(1, C_WIDTH))


def _mm_out_kernel(a_ref, b_ref, c_ref, wa_ref, wb_ref, wc_ref, o_ref):
    o_ref[...] = (_dot(a_ref[...], wa_ref[...]) + _dot(b_ref[...], wb_ref[...])
                  + _dot(c_ref[...], wc_ref[...]))


def _mm_out(oa, ob, oc, w_o):
    m = oa.shape[0]
    n = w_o.shape[1]
    tm = _tile(m, 1024)
    tn = _tile(n, 1024)
    assert A_WIDTH % B_WIDTH == 0 and B_WIDTH == C_WIDTH
    return pl.pallas_call(
        _mm_out_kernel, grid=(m // tm, n // tn),
        in_specs=[pl.BlockSpec((tm, A_WIDTH), lambda i, j: (i, 0)),
                  pl.BlockSpec((tm, B_WIDTH), lambda i, j: (i, 0)),
                  pl.BlockSpec((tm, C_WIDTH), lambda i, j: (i, 0)),
                  pl.BlockSpec((A_WIDTH, tn), lambda i, j: (0, j)),
                  pl.BlockSpec((B_WIDTH, tn), lambda i, j: (A_WIDTH // B_WIDTH, j)),
                  pl.BlockSpec((C_WIDTH, tn), lambda i, j: (A_WIDTH // C_WIDTH + 1, j))],
        out_specs=pl.BlockSpec((tm, tn), lambda i, j: (i, j)),
        out_shape=jax.ShapeDtypeStruct((m, n), F32),
        compiler_params=_cparams("parallel", "parallel"), name="mm_out",
    )(oa, ob, oc, w_o, w_o, w_o)


def _gelu_tanh(x):
    return 0.5 * x * (1.0 + jnp.tanh(math.sqrt(2.0 / math.pi) * (x + 0.044715 * (x * x * x))))


def _mm_up_kernel(x_ref, wg_ref, wv_ref, cg_ref, cv_ref, bg_ref, bv_ref, o_ref,
                  tail_ref, yg_ref, yv_ref, *, tm, nj, tiles_per_seq):
    i = pl.program_id(0)
    j = pl.program_id(1)
    seq_start = (i % tiles_per_seq) == 0
    x = x_ref[...]
    pad = 8

    def half(w_ref, c_ref, b_ref, y_ref, slot):
        y = _dot(x, w_ref[...])
        prev = tail_ref[slot]
        y_ref[0:pad, :] = jnp.where(seq_start, 0.0, prev)
        y_ref[pad:, :] = y
        tail_ref[slot] = y[tm - pad:, :]
        cw = c_ref[...]
        return (b_ref[...] + cw[2:3, :] * y + cw[1:2, :] * y_ref[pl.ds(pad - 1, tm), :]
                + cw[0:1, :] * y_ref[pl.ds(pad - 2, tm), :])

    zg = half(wg_ref, cg_ref, bg_ref, yg_ref, j)
    zv = half(wv_ref, cv_ref, bv_ref, yv_ref, nj + j)
    o_ref[...] = (_gelu_tanh(zg) * zv).astype(BF16)


def _mm_up(hn, w_up, w_conv, b_conv, seq):
    m, d = hn.shape
    f = w_up.shape[1] // 2
    tm = _tile(seq, 1024)
    tn = _tile(f, 512)
    nj = f // tn
    return pl.pallas_call(
        functools.partial(_mm_up_kernel, tm=tm, nj=nj, tiles_per_seq=seq // tm),
        grid=(m // tm, nj),
        in_specs=[pl.BlockSpec((tm, d), lambda i, j: (i, 0)),
                  pl.BlockSpec((d, tn), lambda i, j: (0, j)),
                  pl.BlockSpec((d, tn), lambda i, j: (0, nj + j)),
                  pl.BlockSpec((CONV_K, tn), lambda i, j: (0, j)),
                  pl.BlockSpec((CONV_K, tn), lambda i, j: (0, nj + j)),
                  pl.BlockSpec((1, tn), lambda i, j: (0, j)),
                  pl.BlockSpec((1, tn), lambda i, j: (0, nj + j))],
        out_specs=pl.BlockSpec((tm, tn), lambda i, j: (i, j)),
        out_shape=jax.ShapeDtypeStruct((m, f), BF16),
        scratch_shapes=[pltpu.VMEM((2 * nj, 8, tn), F32),
                        pltpu.VMEM((tm + 8, tn), F32), pltpu.VMEM((tm + 8, tn), F32)],
        compiler_params=_cparams("arbitrary", "arbitrary"), name="mm_up",
    )(hn, w_up, w_up, w_conv, w_conv, b_conv, b_conv)


def _mm_kernel(x_ref, w_ref, o_ref):
    o_ref[...] = _dot(x_ref[...], w_ref[...]).astype(o_ref.dtype)


def _mm_down(x, w):
    m, k = x.shape
    n = w.shape[1]
    tm = _tile(m, 512)
    tn = _tile(n, 256)
    return pl.pallas_call(
        _mm_kernel, grid=(m // tm, n // tn),
        in_specs=[pl.BlockSpec((tm, k), lambda i, j: (i, 0)),
                  pl.BlockSpec((k, tn), lambda i, j: (0, j))],
        out_specs=pl.BlockSpec((tm, tn), lambda i, j: (i, j)),
        out_shape=jax.ShapeDtypeStruct((m, n), F32),
        compiler_params=_cparams("parallel", "parallel"), name="mm_down",
    )(x, w)


def _round_up(n, mult):
    return (n + mult - 1) // mult * mult


def kernel(x, w_in, w_gate_lr2, b_gate, lam_q1, lam_k1, lam_q2, lam_k2, g_subln, g_gla, w_pool, pool_scale, w_o, w_up, w_conv, b_conv, w_down, g_pre_mix, g_post_mix, g_pre_ffn, g_post_ffn):
    batch, seq, d = x.shape
    depth = w_in.shape[0]
    m = batch * seq
    d_ff = w_down.shape[1]
    assert seq % CHUNK == 0 and w_in.shape[2] == U_COLS + GATE_RANK
    ffp = _round_up(d_ff, 2 * MXU_COLS) if d_ff > 2 * MXU_COLS else d_ff
    zl0 = GB0 + B_WIDTH

    h = x.reshape(m, d)
    branch = None
    for l in range(depth):
        wl = w_in[l]
        w_main = jnp.concatenate([wl[:, :zl0], wl[:, zl0 + GATE_RANK:]], axis=1).astype(BF16)
        w_zlr = jnp.pad(wl[:, zl0:zl0 + GATE_RANK], ((0, 0), (0, LANES - GATE_RANK))).astype(BF16)
        w_gate = jnp.pad(w_gate_lr2[l], ((0, LANES - GATE_RANK), (0, 0))).astype(BF16)
        fpad = ((0, 0), (0, ffp - d_ff))
        w_up_p = jnp.concatenate([jnp.pad(w_up[l][:, :d_ff], fpad),
                                  jnp.pad(w_up[l][:, d_ff:], fpad)], axis=1).astype(BF16)
        w_conv_p = jnp.concatenate([jnp.pad(w_conv[l][:, :d_ff], fpad),
                                    jnp.pad(w_conv[l][:, d_ff:], fpad)], axis=1)
        b_conv_p = jnp.concatenate([jnp.pad(b_conv[l][:d_ff], (0, ffp - d_ff)),
                                    jnp.pad(b_conv[l][d_ff:], (0, ffp - d_ff))]).reshape(1, 2 * ffp)
        w_down_p = jnp.pad(w_down[l], ((0, ffp - d_ff), (0, 0))).astype(BF16)

        if l == 0:
            hn = _norm_first(h, g_pre_mix[0])
        else:
            h, hn = _resnorm(h, branch, g_post_ffn[l - 1], g_pre_mix[l])
        u, z = _mm_in(hn, w_main, w_zlr)
        lam_init = 0.8 - 0.6 * math.exp(-0.3 * l)
        oa = _att(u, lam_q1[l], lam_k1[l], lam_q2[l], lam_k2[l], g_subln[l], lam_init, batch, seq)
        ob = _gla(u, z, w_gate, b_gate[l], g_gla[l], batch, seq)
        oc = _pool(u, w_pool[l].astype(BF16), pool_scale[l], batch, seq)
        mix = _mm_out(oa, ob, oc, w_o[l].astype(BF16))

        h, hn = _resnorm(h, mix, g_post_mix[l], g_pre_ffn[l])
        f_in = _mm_up(hn, w_up_p, w_conv_p, b_conv_p, seq)
        branch = _mm_down(f_in, w_down_p)
    h = _resnorm_last(h, branch, g_post_ffn[depth - 1])
    return h.reshape(batch, seq, d)
```

```python
import functools
import math

import jax
import jax.numpy as jnp
from jax import lax
from jax.experimental import pallas as pl
from jax.experimental.pallas import tpu as pltpu

F32 = jnp.float32
BF16 = jnp.bfloat16

CHUNK = 64
EPS = 1e-6
A_HEADS = 8
A_DH = 128
A_WIDTH = A_HEADS * 2 * A_DH
B_HEADS = 4
B_DK = 128
B_DV = 256
B_WIDTH = B_HEADS * B_DV
GATE_RANK = 16
GATE_TAU = 16.0
POOL_WINDOWS = (2, 4, 8, 16)
N_POOL = len(POOL_WINDOWS)
POOL_CH = 256
C_WIDTH = N_POOL * POOL_CH
CONV_K = 3

QA0 = 0
KA0 = A_WIDTH
VA0 = 2 * A_WIDTH
QB0 = 3 * A_WIDTH
KB0 = QB0 + B_HEADS * B_DK
VB0 = KB0 + B_HEADS * B_DK
GB0 = VB0 + B_WIDTH
UC0 = GB0 + B_WIDTH
U_COLS = UC0 + C_WIDTH

LANES = 128
MXU_COLS = 256
VMEM_LIMIT = 56 * 1024 * 1024
POOL_HALO = 16
NEG_BIG = -1e30
LOG2E = math.log2(math.e)
ATT_TQ = 256
ATT_GROUP = 1024


def _cparams(*sem):
    return pltpu.CompilerParams(dimension_semantics=sem, vmem_limit_bytes=VMEM_LIMIT)


def _dot(a, b):
    return jnp.dot(a, b, preferred_element_type=F32)


def _dot_nt(a, b):
    return lax.dot_general(a, b, (((1,), (1,)), ((), ())), preferred_element_type=F32)


def _dot_tn(a, b):
    return lax.dot_general(a, b, (((0,), (0,)), ((), ())), preferred_element_type=F32)


def _rms(x, g):
    ms = jnp.mean(x * x, axis=-1, keepdims=True)
    return x * lax.rsqrt(ms + EPS) * g


def _tile(n, want):
    t = min(n, want)
    assert n % t == 0, (n, want)
    return t


def _norm_first_kernel(h_ref, gpre_ref, hn_ref):
    hn_ref[...] = _rms(h_ref[...], gpre_ref[...]).astype(BF16)


def _resnorm_kernel(h_ref, br_ref, gpost_ref, gpre_ref, hout_ref, hn_ref):
    h = h_ref[...] + _rms(br_ref[...], gpost_ref[...])
    hout_ref[...] = h
    hn_ref[...] = _rms(h, gpre_ref[...]).astype(BF16)


def _resnorm_last_kernel(h_ref, br_ref, gpost_ref, hout_ref):
    hout_ref[...] = h_ref[...] + _rms(br_ref[...], gpost_ref[...])


def _norm_first(h, gpre):
    m, d = h.shape
    tr = _tile(m, 256)
    row = pl.BlockSpec((tr, d), lambda i: (i, 0))
    vec = pl.BlockSpec((1, d), lambda i: (0, 0))
    return pl.pallas_call(
        _norm_first_kernel, grid=(m // tr,), in_specs=[row, vec], out_specs=row,
        out_shape=jax.ShapeDtypeStruct((m, d), BF16),
        compiler_params=_cparams("parallel"), name="norm_first",
    )(h, gpre.reshape(1, d))


def _resnorm(h, br, gpost, gpre):
    m, d = h.shape
    tr = _tile(m, 256)
    row = pl.BlockSpec((tr, d), lambda i: (i, 0))
    vec = pl.BlockSpec((1, d), lambda i: (0, 0))
    return pl.pallas_call(
        _resnorm_kernel, grid=(m // tr,), in_specs=[row, row, vec, vec], out_specs=[row, row],
        out_shape=[jax.ShapeDtypeStruct((m, d), F32), jax.ShapeDtypeStruct((m, d), BF16)],
        compiler_params=_cparams("parallel"), name="resnorm",
    )(h, br, gpost.reshape(1, d), gpre.reshape(1, d))


def _resnorm_last(h, br, gpost):
    m, d = h.shape
    tr = _tile(m, 256)
    row = pl.BlockSpec((tr, d), lambda i: (i, 0))
    vec = pl.BlockSpec((1, d), lambda i: (0, 0))
    return pl.pallas_call(
        _resnorm_last_kernel, grid=(m // tr,), in_specs=[row, row, vec], out_specs=row,
        out_shape=jax.ShapeDtypeStruct((m, d), F32),
        compiler_params=_cparams("parallel"), name="resnorm_last",
    )(h, br, gpost.reshape(1, d))


def _mm_in_kernel(x_ref, w_ref, wz_ref, u_ref, z_ref):
    x = x_ref[...]
    u_ref[...] = _dot(x, w_ref[...]).astype(BF16)

    @pl.when(pl.program_id(1) == 0)
    def _():
        z_ref[...] = _dot(x, wz_ref[...])


def _mm_in(hn, w_main, w_zlr):
    m, d = hn.shape
    n = w_main.shape[1]
    tm = _tile(m, 1024)
    tn = _tile(n, 1024)
    return pl.pallas_call(
        _mm_in_kernel, grid=(m // tm, n // tn),
        in_specs=[pl.BlockSpec((tm, d), lambda i, j: (i, 0)),
                  pl.BlockSpec((d, tn), lambda i, j: (0, j)),
                  pl.BlockSpec((d, LANES), lambda i, j: (0, 0))],
        out_specs=[pl.BlockSpec((tm, tn), lambda i, j: (i, j)),
                   pl.BlockSpec((tm, LANES), lambda i, j: (i, 0))],
        out_shape=[jax.ShapeDtypeStruct((m, n), BF16), jax.ShapeDtypeStruct((m, LANES), F32)],
        compiler_params=_cparams("parallel", "arbitrary"), name="mm_in",
    )(hn, w_main, w_zlr)


def _att_kernel(lq1_ref, lk1_ref, lq2_ref, lk2_ref, g_ref, q_ref, k_ref, v_ref, o_ref,
                s1_ref, s2_ref, acc1_ref, acc2_ref, bias_ref, *, lam_init, t, G):
    head = pl.program_id(1)
    qi = pl.program_id(2)
    qscale = LOG2E / math.sqrt(A_DH)
    slope = lax.bitcast_convert_type(
        jnp.full((1, 1), (126 - head) << 23, jnp.int32), F32) * LOG2E

    q = q_ref[...].astype(F32) * qscale
    qt1 = q[:, :A_DH].T.astype(BF16)
    qt2 = q[:, A_DH:].T.astype(BF16)

    def fold8(x, op):
        return op(x.reshape(G // 8, 8, t), axis=0)

    @pl.when(qi == 0)
    def _():
        kk = lax.broadcasted_iota(jnp.int32, (G, t), 0)
        bias_ref[...] = slope * kk.astype(F32)

    qs = qi * t
    gl = qs // G

    def offset(g):
        return slope * (g * G - qs).astype(F32)

    def score_body(g, carry):
        m1, m2 = carry
        k0 = pl.multiple_of(g * G, G)
        kt = k_ref[pl.ds(k0, G), :]
        s1 = _dot(kt[:, :A_DH], qt1) + bias_ref[...]
        s2 = _dot(kt[:, A_DH:], qt2) + bias_ref[...]
        s1_ref[pl.ds(k0, G), :] = s1
        s2_ref[pl.ds(k0, G), :] = s2
        return (jnp.maximum(m1, fold8(s1, jnp.max) + offset(g)),
                jnp.maximum(m2, fold8(s2, jnp.max) + offset(g)))

    l0 = pl.multiple_of(gl * G, G)
    jd = (qs - l0) // t
    kl = k_ref[pl.ds(l0, G), :]
    sl1 = _dot(kl[:, :A_DH], qt1) + bias_ref[...]
    sl2 = _dot(kl[:, A_DH:], qt2) + bias_ref[...]
    s1_ref[pl.ds(l0, G), :] = sl1
    s2_ref[pl.ds(l0, G), :] = sl2

    def fold8t(x, op):
        return op(x.reshape(t // 8, 8, t), axis=0)

    m1 = jnp.full((8, t), NEG_BIG, F32)
    m2 = m1
    for j in range(G // t - 1):
        skip = jnp.where(j < jd, 0.0, NEG_BIG)
        m1 = jnp.maximum(m1, fold8t(sl1[j * t:(j + 1) * t], jnp.max) + skip)
        m2 = jnp.maximum(m2, fold8t(sl2[j * t:(j + 1) * t], jnp.max) + skip)
    cshift = CHUNK.bit_length() - 1
    kd = lax.broadcasted_iota(jnp.int32, (t, t), 0)
    rd = lax.broadcasted_iota(jnp.int32, (t, t), 1)
    allowed = (kd >> cshift) <= (rd >> cshift)
    corr = slope * (rd - jnp.abs(rd - kd) - kd).astype(F32)
    d0 = pl.multiple_of(qs, t)
    sd1 = jnp.where(allowed, s1_ref[pl.ds(d0, t), :] + corr, NEG_BIG)
    sd2 = jnp.where(allowed, s2_ref[pl.ds(d0, t), :] + corr, NEG_BIG)
    s1_ref[pl.ds(d0, t), :] = sd1
    s2_ref[pl.ds(d0, t), :] = sd2
    for j in range(1, G // t):
        @pl.when(j > jd)
        def _():
            j0 = pl.multiple_of(l0 + j * t, t)
            s1_ref[pl.ds(j0, t), :] = jnp.full((t, t), NEG_BIG, F32)
            s2_ref[pl.ds(j0, t), :] = jnp.full((t, t), NEG_BIG, F32)
    m1 = jnp.maximum(m1, fold8t(sd1, jnp.max)) + offset(gl)
    m2 = jnp.maximum(m2, fold8t(sd2, jnp.max)) + offset(gl)
    m1, m2 = lax.fori_loop(0, gl, score_body, (m1, m2))
    m1 = jnp.max(m1, axis=0, keepdims=True)
    m2 = jnp.max(m2, axis=0, keepdims=True)

    acc1_ref[...] = jnp.zeros_like(acc1_ref)
    acc2_ref[...] = jnp.zeros_like(acc2_ref)

    def pv_body(g, carry):
        l1, l2 = carry
        k0 = pl.multiple_of(g * G, G)
        vt = v_ref[pl.ds(k0, G), :]
        p1 = jnp.exp2(s1_ref[pl.ds(k0, G), :] - (m1 - offset(g)))
        p2 = jnp.exp2(s2_ref[pl.ds(k0, G), :] - (m2 - offset(g)))
        acc1_ref[...] += _dot_tn(vt, p1.astype(BF16))
        acc2_ref[...] += _dot_tn(vt, p2.astype(BF16))
        return l1 + fold8(p1, jnp.sum), l2 + fold8(p2, jnp.sum)

    zero8 = jnp.zeros((8, t), F32)
    l1, l2 = lax.fori_loop(0, gl + 1, pv_body, (zero8, zero8))
    l1 = jnp.sum(l1, axis=0, keepdims=True)
    l2 = jnp.sum(l2, axis=0, keepdims=True)

    lam = (jnp.exp(jnp.sum(lq1_ref[...] * lk1_ref[...], axis=-1, keepdims=True))
           - jnp.exp(jnp.sum(lq2_ref[...] * lk2_ref[...], axis=-1, keepdims=True))
           + lam_init)
    ot = acc1_ref[...] / l1 - lam * (acc2_ref[...] / l2)
    ms = jnp.mean(ot * ot, axis=0, keepdims=True)
    o = (ot * lax.rsqrt(ms + EPS)).T
    o_ref[...] = (o * g_ref[...] * (1.0 - lam_init)).astype(BF16)


def _att(u, lq1, lk1, lq2, lk2, g_subln, lam_init, batch, seq):
    m = u.shape[0]
    t = _tile(seq, ATT_TQ)
    G = _tile(seq, ATT_GROUP)
    nq = seq // t
    e = 2 * A_DH
    assert G % t == 0
    vec = pl.BlockSpec((1, A_DH), lambda b, h, i: (0, 0))
    return pl.pallas_call(
        functools.partial(_att_kernel, lam_init=lam_init, t=t, G=G),
        grid=(batch, A_HEADS, nq),
        in_specs=[vec, vec, vec, vec,
                  pl.BlockSpec((1, e), lambda b, h, i: (0, 0)),
                  pl.BlockSpec((t, e), lambda b, h, i: (b * nq + i, QA0 // e + h)),
                  pl.BlockSpec((seq, e), lambda b, h, i: (b, KA0 // e + h)),
                  pl.BlockSpec((seq, e), lambda b, h, i: (b, VA0 // e + h))],
        out_specs=pl.BlockSpec((t, e), lambda b, h, i: (b * nq + i, h)),
        out_shape=jax.ShapeDtypeStruct((m, A_WIDTH), BF16),
        scratch_shapes=[pltpu.VMEM((seq, t), F32), pltpu.VMEM((seq, t), F32),
                        pltpu.VMEM((e, t), F32), pltpu.VMEM((e, t), F32),
                        pltpu.VMEM((G, t), F32)],
        compiler_params=_cparams("arbitrary", "arbitrary", "arbitrary"), name="diff_attn",
    )(lq1.reshape(1, A_DH), lk1.reshape(1, A_DH), lq2.reshape(1, A_DH), lk2.reshape(1, A_DH),
      g_subln.reshape(1, e), u, u, u)


def _gla_kernel(z_ref, wg_ref, bg_ref, gn_ref, q_ref, k_ref, v_ref, gb_ref, o_ref, *, seq):
    L = CHUNK
    wg = wg_ref[...]
    bg = bg_ref[...]
    gn = gn_ref[...]
    rows = lax.broadcasted_iota(jnp.int32, (L, B_DK), 0)
    ti = lax.broadcasted_iota(jnp.int32, (L, L), 0)
    si = lax.broadcasted_iota(jnp.int32, (L, L), 1)
    qscale = B_DK ** -0.5

    def chunk(c, state_t):
        r0 = pl.multiple_of(c * L, L)
        x = _dot(z_ref[pl.ds(r0, L), :].astype(BF16), wg) + bg
        la = (jnp.minimum(x, 0.0) - jnp.log(1.0 + jnp.exp(-jnp.abs(x)))) * (1.0 / GATE_TAU)
        b = la
        sh = 1
        while sh < L:
            b = b + jnp.where(rows >= sh, pltpu.roll(b, sh, axis=0), 0.0)
            sh *= 2
        b_last = b[L - 1:L, :]
        qf = q_ref[pl.ds(r0, L), :].astype(F32) * qscale
        kc = k_ref[pl.ds(r0, L), :]
        vc = v_ref[pl.ds(r0, L), :]

        a = jnp.zeros((L, L), F32)
        for s in range(L):
            e = jnp.exp(jnp.minimum(b - b[s:s + 1, :], 0.0))
            rs = _dot_nt((qf * e).astype(BF16), kc)
            a = jnp.where(si == s, rs, a)
        a = jnp.where(ti >= si, a, 0.0)

        qe = (qf * jnp.exp(b)).astype(BF16)
        o = _dot(a.astype(BF16), vc) + _dot_nt(qe, state_t.astype(BF16))
        ke = (kc.astype(F32) * jnp.exp(b_last - b)).astype(BF16)
        new_state = jnp.exp(b_last) * state_t + _dot_tn(vc, ke)

        gate = gb_ref[pl.ds(r0, L), :].astype(F32)
        y = _rms(o, gn) * (gate / (1.0 + jnp.exp(-gate)))
        o_ref[pl.ds(r0, L), :] = y.astype(BF16)
        return new_state

    lax.fori_loop(0, seq // L, chunk, jnp.zeros((B_DV, B_DK), F32))


def _gla(u, z, w_gate, b_gate, g_gla, batch, seq):
    m = u.shape[0]
    return pl.pallas_call(
        functools.partial(_gla_kernel, seq=seq),
        grid=(batch, B_HEADS),
        in_specs=[pl.BlockSpec((seq, LANES), lambda b, h: (b, 0)),
                  pl.BlockSpec((LANES, B_DK), lambda b, h: (0, h)),
                  pl.BlockSpec((1, B_DK), lambda b, h: (0, h)),
                  pl.BlockSpec((1, B_DV), lambda b, h: (0, h)),
                  pl.BlockSpec((seq, B_DK), lambda b, h: (b, QB0 // B_DK + h)),
                  pl.BlockSpec((seq, B_DK), lambda b, h: (b, KB0 // B_DK + h)),
                  pl.BlockSpec((seq, B_DV), lambda b, h: (b, VB0 // B_DV + h)),
                  pl.BlockSpec((seq, B_DV), lambda b, h: (b, GB0 // B_DV + h))],
        out_specs=pl.BlockSpec((seq, B_DV), lambda b, h: (b, h)),
        out_shape=jax.ShapeDtypeStruct((m, B_WIDTH), BF16),
        compiler_params=_cparams("parallel", "parallel"), name="gla",
    )(z, w_gate, b_gate.reshape(1, -1), g_gla.reshape(1, -1), u, u, u, u)


def _pool_kernel(u_ref, w_ref, sc_ref, o_ref, carry_ref, *, tr):
    ti = pl.program_id(1)

    @pl.when(ti == 0)
    def _():
        carry_ref[...] = jnp.zeros_like(carry_ref)

    x = u_ref[...].astype(F32)
    ext = jnp.concatenate([carry_ref[...], x], axis=0)
    carry_ref[...] = x[tr - POOL_HALO:, :]
    pos = ti * tr + lax.broadcasted_iota(jnp.int32, (tr, 1), 0)
    for g, w in enumerate(POOL_WINDOWS):
        lo, hi = g * POOL_CH, (g + 1) * POOL_CH
        s = ext[:, lo:hi]
        sh = 1
        while sh < w:
            s = s + pltpu.roll(s, sh, axis=0)
            sh *= 2
        cnt = jnp.minimum(pos + 1, w).astype(F32)
        d = s[POOL_HALO:, :] / cnt - x[:, lo:hi]
        y = _dot(d.astype(BF16), w_ref[g]) * sc_ref[:, lo:hi]
        o_ref[:, lo:hi] = y.astype(BF16)


def _pool(u, w_pool, pool_scale, batch, seq):
    m = u.shape[0]
    tr = _tile(seq, 512)
    nt = seq // tr
    assert tr >= POOL_HALO and max(POOL_WINDOWS) <= POOL_HALO
    return pl.pallas_call(
        functools.partial(_pool_kernel, tr=tr),
        grid=(batch, nt),
        in_specs=[pl.BlockSpec((tr, C_WIDTH), lambda b, i: (b * nt + i, UC0 // C_WIDTH)),
                  pl.BlockSpec((N_POOL, POOL_CH, POOL_CH), lambda b, i: (0, 0, 0)),
                  pl.BlockSpec((1, C_WIDTH), lambda b, i: (0, 0))],
        out_specs=pl.BlockSpec((tr, C_WIDTH), lambda b, i: (b * nt + i, 0)),
        out_shape=jax.ShapeDtypeStruct((m, C_WIDTH), BF16),
        scratch_shapes=[pltpu.VMEM((POOL_HALO, C_WIDTH), F32)],
        compiler_params=_cparams("parallel", "arbitrary"), name="pool",
    )(u, w_pool, pool_scale.reshape(1, C_WIDTH))


def _mm_out_kernel(a_ref, b_ref, c_ref, wa_ref, wb_ref, wc_ref, o_ref):
    o_ref[...] = (_dot(a_ref[...], wa_ref[...]) + _dot(b_ref[...], wb_ref[...])
                  + _dot(c_ref[...], wc_ref[...]))


def _mm_out(oa, ob, oc, w_o):
    m = oa.shape[0]
    n = w_o.shape[1]
    tm = _tile(m, 1024)
    tn = _tile(n, 1024)
    assert A_WIDTH % B_WIDTH == 0 and B_WIDTH == C_WIDTH
    return pl.pallas_call(
        _mm_out_kernel, grid=(m // tm, n // tn),
        in_specs=[pl.BlockSpec((tm, A_WIDTH), lambda i, j: (i, 0)),
                  pl.BlockSpec((tm, B_WIDTH), lambda i, j: (i, 0)),
                  pl.BlockSpec((tm, C_WIDTH), lambda i, j: (i, 0)),
                  pl.BlockSpec((A_WIDTH, tn), lambda i, j: (0, j)),
                  pl.BlockSpec((B_WIDTH, tn), lambda i, j: (A_WIDTH // B_WIDTH, j)),
                  pl.BlockSpec((C_WIDTH, tn), lambda i, j: (A_WIDTH // C_WIDTH + 1, j))],
        out_specs=pl.BlockSpec((tm, tn), lambda i, j: (i, j)),
        out_shape=jax.ShapeDtypeStruct((m, n), F32),
        compiler_params=_cparams("parallel", "parallel"), name="mm_out",
    )(oa, ob, oc, w_o, w_o, w_o)


def _gelu_tanh(x):
    return 0.5 * x * (1.0 + jnp.tanh(math.sqrt(2.0 / math.pi) * (x + 0.044715 * (x * x * x))))


def _mm_up_kernel(x_ref, wg_ref, wv_ref, cg_ref, cv_ref, bg_ref, bv_ref, o_ref,
                  tail_ref, yg_ref, yv_ref, *, tm, nj, tiles_per_seq):
    i = pl.program_id(0)
    j = pl.program_id(1)
    seq_start = (i % tiles_per_seq) == 0
    x = x_ref[...]
    pad = 8

    def half(w_ref, c_ref, b_ref, y_ref, slot):
        y = _dot(x, w_ref[...])
        prev = tail_ref[slot]
        y_ref[0:pad, :] = jnp.where(seq_start, 0.0, prev)
        y_ref[pad:, :] = y
        tail_ref[slot] = y[tm - pad:, :]
        cw = c_ref[...]
        return (b_ref[...] + cw[2:3, :] * y + cw[1:2, :] * y_ref[pl.ds(pad - 1, tm), :]
                + cw[0:1, :] * y_ref[pl.ds(pad - 2, tm), :])

    zg = half(wg_ref, cg_ref, bg_ref, yg_ref, j)
    zv = half(wv_ref, cv_ref, bv_ref, yv_ref, nj + j)
    o_ref[...] = (_gelu_tanh(zg) * zv).astype(BF16)


def _mm_up(hn, w_up, w_conv, b_conv, seq):
    m, d = hn.shape
    f = w_up.shape[1] // 2
    tm = _tile(seq, 1024)
    tn = _tile(f, 512)
    nj = f // tn
    return pl.pallas_call(
        functools.partial(_mm_up_kernel, tm=tm, nj=nj, tiles_per_seq=seq // tm),
        grid=(m // tm, nj),
        in_specs=[pl.BlockSpec((tm, d), lambda i, j: (i, 0)),
                  pl.BlockSpec((d, tn), lambda i, j: (0, j)),
                  pl.BlockSpec((d, tn), lambda i, j: (0, nj + j)),
                  pl.BlockSpec((CONV_K, tn), lambda i, j: (0, j)),
                  pl.BlockSpec((CONV_K, tn), lambda i, j: (0, nj + j)),
                  pl.BlockSpec((1, tn), lambda i, j: (0, j)),
                  pl.BlockSpec((1, tn), lambda i, j: (0, nj + j))],
        out_specs=pl.BlockSpec((tm, tn), lambda i, j: (i, j)),
        out_shape=jax.ShapeDtypeStruct((m, f), BF16),
        scratch_shapes=[pltpu.VMEM((2 * nj, 8, tn), F32),
                        pltpu.VMEM((tm + 8, tn), F32), pltpu.VMEM((tm + 8, tn), F32)],
        compiler_params=_cparams("arbitrary", "arbitrary"), name="mm_up",
    )(hn, w_up, w_up, w_conv, w_conv, b_conv, b_conv)


def _mm_kernel(x_ref, w_ref, o_ref):
    o_ref[...] = _dot(x_ref[...], w_ref[...]).astype(o_ref.dtype)


def _mm_down(x, w):
    m, k = x.shape
    n = w.shape[1]
    tm = _tile(m, 512)
    tn = _tile(n, 256)
    return pl.pallas_call(
        _mm_kernel, grid=(m // tm, n // tn),
        in_specs=[pl.BlockSpec((tm, k), lambda i, j: (i, 0)),
                  pl.BlockSpec((k, tn), lambda i, j: (0, j))],
        out_specs=pl.BlockSpec((tm, tn), lambda i, j: (i, j)),
        out_shape=jax.ShapeDtypeStruct((m, n), F32),
        compiler_params=_cparams("parallel", "parallel"), name="mm_down",
    )(x, w)


def _round_up(n, mult):
    return (n + mult - 1) // mult * mult


def kernel(x, w_in, w_gate_lr2, b_gate, lam_q1, lam_k1, lam_q2, lam_k2, g_subln, g_gla, w_pool, pool_scale, w_o, w_up, w_conv, b_conv, w_down, g_pre_mix, g_post_mix, g_pre_ffn, g_post_ffn):
    batch, seq, d = x.shape
    depth = w_in.shape[0]
    m = batch * seq
    d_ff = w_down.shape[1]
    assert seq % CHUNK == 0 and w_in.shape[2] == U_COLS + GATE_RANK
    ffp = _round_up(d_ff, 2 * MXU_COLS) if d_ff > 2 * MXU_COLS else d_ff
    zl0 = GB0 + B_WIDTH

    h = x.reshape(m, d)
    branch = None
    for l in range(depth):
        wl = w_in[l]
        w_main = jnp.concatenate([wl[:, :zl0], wl[:, zl0 + GATE_RANK:]], axis=1).astype(BF16)
        w_zlr = jnp.pad(wl[:, zl0:zl0 + GATE_RANK], ((0, 0), (0, LANES - GATE_RANK))).astype(BF16)
        w_gate = jnp.pad(w_gate_lr2[l], ((0, LANES - GATE_RANK), (0, 0))).astype(BF16)
        fpad = ((0, 0), (0, ffp - d_ff))
        w_up_p = jnp.concatenate([jnp.pad(w_up[l][:, :d_ff], fpad),
                                  jnp.pad(w_up[l][:, d_ff:], fpad)], axis=1).astype(BF16)
        w_conv_p = jnp.concatenate([jnp.pad(w_conv[l][:, :d_ff], fpad),
                                    jnp.pad(w_conv[l][:, d_ff:], fpad)], axis=1)
        b_conv_p = jnp.concatenate([jnp.pad(b_conv[l][:d_ff], (0, ffp - d_ff)),
                                    jnp.pad(b_conv[l][d_ff:], (0, ffp - d_ff))]).reshape(1, 2 * ffp)
        w_down_p = jnp.pad(w_down[l], ((0, ffp - d_ff), (0, 0))).astype(BF16)

        if l == 0:
            hn = _norm_first(h, g_pre_mix[0])
        else:
            h, hn = _resnorm(h, branch, g_post_ffn[l - 1], g_pre_mix[l])
        u, z = _mm_in(hn, w_main, w_zlr)
        lam_init = 0.8 - 0.6 * math.exp(-0.3 * l)
        oa = _att(u, lam_q1[l], lam_k1[l], lam_q2[l], lam_k2[l], g_subln[l], lam_init, batch, seq)
        ob = _gla(u, z, w_gate, b_gate[l], g_gla[l], batch, seq)
        oc = _pool(u, w_pool[l].astype(BF16), pool_scale[l], batch, seq)
        mix = _mm_out(oa, ob, oc, w_o[l].astype(BF16))

        h, hn = _resnorm(h, mix, g_post_mix[l], g_pre_ffn[l])
        f_in = _mm_up(hn, w_up_p, w_conv_p, b_conv_p, seq)
        branch = _mm_down(f_in, w_down_p)
    h = _resnorm_last(h, branch, g_post_ffn[depth - 1])
    return h.reshape(batch, seq, d)
```

```python
import functools
import math

import jax
import jax.numpy as jnp
from jax import lax
from jax.experimental import pallas as pl
from jax.experimental.pallas import tpu as pltpu

F32 = jnp.float32
BF16 = jnp.bfloat16

CHUNK = 64
EPS = 1e-6
A_HEADS = 8
A_DH = 128
A_WIDTH = A_HEADS * 2 * A_DH
B_HEADS = 4
B_DK = 128
B_DV = 256
B_WIDTH = B_HEADS * B_DV
GATE_RANK = 16
GATE_TAU = 16.0
POOL_WINDOWS = (2, 4, 8, 16)
N_POOL = len(POOL_WINDOWS)
POOL_CH = 256
C_WIDTH = N_POOL * POOL_CH
CONV_K = 3

QA0 = 0
KA0 = A_WIDTH
VA0 = 2 * A_WIDTH
QB0 = 3 * A_WIDTH
KB0 = QB0 + B_HEADS * B_DK
VB0 = KB0 + B_HEADS * B_DK
GB0 = VB0 + B_WIDTH
UC0 = GB0 + B_WIDTH
U_COLS = UC0 + C_WIDTH

LANES = 128
MXU_COLS = 256
VMEM_LIMIT = 56 * 1024 * 1024
POOL_HALO = 16
NEG_BIG = -1e30
LOG2E = math.log2(math.e)
ATT_TQ = 256
ATT_GROUP = 1024
UP_TM = 2048


def _cparams(*sem):
    return pltpu.CompilerParams(dimension_semantics=sem, vmem_limit_bytes=VMEM_LIMIT)


def _dot(a, b):
    return jnp.dot(a, b, preferred_element_type=F32)


def _dot_nt(a, b):
    return lax.dot_general(a, b, (((1,), (1,)), ((), ())), preferred_element_type=F32)


def _dot_tn(a, b):
    return lax.dot_general(a, b, (((0,), (0,)), ((), ())), preferred_element_type=F32)


def _rms(x, g):
    ms = jnp.mean(x * x, axis=-1, keepdims=True)
    return x * lax.rsqrt(ms + EPS) * g


def _tile(n, want):
    t = min(n, want)
    assert n % t == 0, (n, want)
    return t


def _norm_first_kernel(h_ref, gpre_ref, hn_ref):
    hn_ref[...] = _rms(h_ref[...], gpre_ref[...]).astype(BF16)


def _resnorm_kernel(h_ref, br_ref, gpost_ref, gpre_ref, hout_ref, hn_ref):
    h = h_ref[...] + _rms(br_ref[...], gpost_ref[...])
    hout_ref[...] = h
    hn_ref[...] = _rms(h, gpre_ref[...]).astype(BF16)


def _resnorm_last_kernel(h_ref, br_ref, gpost_ref, hout_ref):
    hout_ref[...] = h_ref[...] + _rms(br_ref[...], gpost_ref[...])


def _norm_first(h, gpre):
    m, d = h.shape
    tr = _tile(m, 256)
    row = pl.BlockSpec((tr, d), lambda i: (i, 0))
    vec = pl.BlockSpec((1, d), lambda i: (0, 0))
    return pl.pallas_call(
        _norm_first_kernel, grid=(m // tr,), in_specs=[row, vec], out_specs=row,
        out_shape=jax.ShapeDtypeStruct((m, d), BF16),
        compiler_params=_cparams("parallel"), name="norm_first",
    )(h, gpre.reshape(1, d))


def _resnorm(h, br, gpost, gpre):
    m, d = h.shape
    tr = _tile(m, 256)
    row = pl.BlockSpec((tr, d), lambda i: (i, 0))
    vec = pl.BlockSpec((1, d), lambda i: (0, 0))
    return pl.pallas_call(
        _resnorm_kernel, grid=(m // tr,), in_specs=[row, row, vec, vec], out_specs=[row, row],
        out_shape=[jax.ShapeDtypeStruct((m, d), F32), jax.ShapeDtypeStruct((m, d), BF16)],
        compiler_params=_cparams("parallel"), name="resnorm",
    )(h, br, gpost.reshape(1, d), gpre.reshape(1, d))


def _resnorm_last(h, br, gpost):
    m, d = h.shape
    tr = _tile(m, 256)
    row = pl.BlockSpec((tr, d), lambda i: (i, 0))
    vec = pl.BlockSpec((1, d), lambda i: (0, 0))
    return pl.pallas_call(
        _resnorm_last_kernel, grid=(m // tr,), in_specs=[row, row, vec], out_specs=row,
        out_shape=jax.ShapeDtypeStruct((m, d), F32),
        compiler_params=_cparams("parallel"), name="resnorm_last",
    )(h, br, gpost.reshape(1, d))


def _mm_in_kernel(x_ref, w_ref, wc_ref, wz_ref, u_ref, z_ref, *, n_main):
    j = pl.program_id(1)

    @pl.when(j < n_main)
    def _():
        u_ref[...] = _dot(x_ref[...], w_ref[...].astype(BF16)).astype(BF16)

    @pl.when(j >= n_main)
    def _():
        u_ref[...] = _dot(x_ref[...], wc_ref[...]).astype(BF16)

    @pl.when(j == 0)
    def _():
        z_ref[...] = _dot(x_ref[...], wz_ref[...])


def _mm_in(hn, w_in, layer, w_c, w_zlr):
    m, d = hn.shape
    tm = _tile(m, 1024)
    tn = 512
    assert UC0 % tn == 0 and C_WIDTH % tn == 0
    n_main = UC0 // tn
    return pl.pallas_call(
        functools.partial(_mm_in_kernel, n_main=n_main), grid=(m // tm, U_COLS // tn),
        in_specs=[pl.BlockSpec((tm, d), lambda i, j: (i, 0)),
                  pl.BlockSpec((None, d, tn), lambda i, j: (layer, 0, jnp.minimum(j, n_main - 1))),
                  pl.BlockSpec((d, tn), lambda i, j: (0, jnp.maximum(j - n_main, 0))),
                  pl.BlockSpec((d, LANES), lambda i, j: (0, 0))],
        out_specs=[pl.BlockSpec((tm, tn), lambda i, j: (i, j)),
                   pl.BlockSpec((tm, LANES), lambda i, j: (i, 0))],
        out_shape=[jax.ShapeDtypeStruct((m, U_COLS), BF16), jax.ShapeDtypeStruct((m, LANES), F32)],
        compiler_params=_cparams("parallel", "arbitrary"), name="mm_in",
    )(hn, w_in, w_c, w_zlr)


def _att_kernel(lq1_ref, lk1_ref, lq2_ref, lk2_ref, g_ref, q_ref, k_ref, v_ref, o_ref,
                s1_ref, s2_ref, acc1_ref, acc2_ref, bias_ref, *, lam_init, t, G):
    head = pl.program_id(1)
    qi = pl.program_id(2)
    qscale = LOG2E / math.sqrt(A_DH)
    slope = lax.bitcast_convert_type(
        jnp.full((1, 1), (126 - head) << 23, jnp.int32), F32) * LOG2E

    q = q_ref[...].astype(F32) * qscale
    qt1 = q[:, :A_DH].T.astype(BF16)
    qt2 = q[:, A_DH:].T.astype(BF16)

    def fold8(x, op):
        return op(x.reshape(G // 8, 8, t), axis=0)

    @pl.when(qi == 0)
    def _():
        kk = lax.broadcasted_iota(jnp.int32, (G, t), 0)
        bias_ref[...] = slope * kk.astype(F32)

    qs = qi * t
    gl = qs // G

    def offset(g):
        return slope * (g * G - qs).astype(F32)

    def score_body(g, carry):
        m1, m2 = carry
        k0 = pl.multiple_of(g * G, G)
        kt = k_ref[pl.ds(k0, G), :]
        s1 = _dot(kt[:, :A_DH], qt1) + bias_ref[...]
        s2 = _dot(kt[:, A_DH:], qt2) + bias_ref[...]
        s1_ref[pl.ds(k0, G), :] = s1
        s2_ref[pl.ds(k0, G), :] = s2
        return (jnp.maximum(m1, fold8(s1, jnp.max) + offset(g)),
                jnp.maximum(m2, fold8(s2, jnp.max) + offset(g)))

    l0 = pl.multiple_of(gl * G, G)
    jd = (qs - l0) // t
    kl = k_ref[pl.ds(l0, G), :]
    sl1 = _dot(kl[:, :A_DH], qt1) + bias_ref[...]
    sl2 = _dot(kl[:, A_DH:], qt2) + bias_ref[...]
    s1_ref[pl.ds(l0, G), :] = sl1
    s2_ref[pl.ds(l0, G), :] = sl2

    def fold8t(x, op):
        return op(x.reshape(t // 8, 8, t), axis=0)

    m1 = jnp.full((8, t), NEG_BIG, F32)
    m2 = m1
    for j in range(G // t - 1):
        skip = jnp.where(j < jd, 0.0, NEG_BIG)
        m1 = jnp.maximum(m1, fold8t(sl1[j * t:(j + 1) * t], jnp.max) + skip)
        m2 = jnp.maximum(m2, fold8t(sl2[j * t:(j + 1) * t], jnp.max) + skip)
    cshift = CHUNK.bit_length() - 1
    kd = lax.broadcasted_iota(jnp.int32, (t, t), 0)
    rd = lax.broadcasted_iota(jnp.int32, (t, t), 1)
    allowed = (kd >> cshift) <= (rd >> cshift)
    corr = slope * (rd - jnp.abs(rd - kd) - kd).astype(F32)
    d0 = pl.multiple_of(qs, t)
    sd1 = jnp.where(allowed, s1_ref[pl.ds(d0, t), :] + corr, NEG_BIG)
    sd2 = jnp.where(allowed, s2_ref[pl.ds(d0, t), :] + corr, NEG_BIG)
    s1_ref[pl.ds(d0, t), :] = sd1
    s2_ref[pl.ds(d0, t), :] = sd2
    for j in range(1, G // t):
        @pl.when(j > jd)
        def _():
            j0 = pl.multiple_of(l0 + j * t, t)
            s1_ref[pl.ds(j0, t), :] = jnp.full((t, t), NEG_BIG, F32)
            s2_ref[pl.ds(j0, t), :] = jnp.full((t, t), NEG_BIG, F32)
    m1 = jnp.maximum(m1, fold8t(sd1, jnp.max)) + offset(gl)
    m2 = jnp.maximum(m2, fold8t(sd2, jnp.max)) + offset(gl)
    m1, m2 = lax.fori_loop(0, gl, score_body, (m1, m2))
    m1 = jnp.max(m1, axis=0, keepdims=True)
    m2 = jnp.max(m2, axis=0, keepdims=True)

    acc1_ref[...] = jnp.zeros_like(acc1_ref)
    acc2_ref[...] = jnp.zeros_like(acc2_ref)

    def pv_body(g, carry):
        l1, l2 = carry
        k0 = pl.multiple_of(g * G, G)
        vt = v_ref[pl.ds(k0, G), :]
        p1 = jnp.exp2(s1_ref[pl.ds(k0, G), :] - (m1 - offset(g)))
        p2 = jnp.exp2(s2_ref[pl.ds(k0, G), :] - (m2 - offset(g)))
        acc1_ref[...] += _dot_tn(vt, p1.astype(BF16))
        acc2_ref[...] += _dot_tn(vt, p2.astype(BF16))
        return l1 + fold8(p1, jnp.sum), l2 + fold8(p2, jnp.sum)

    zero8 = jnp.zeros((8, t), F32)
    l1, l2 = lax.fori_loop(0, gl + 1, pv_body, (zero8, zero8))
    l1 = jnp.sum(l1, axis=0, keepdims=True)
    l2 = jnp.sum(l2, axis=0, keepdims=True)

    lam = (jnp.exp(jnp.sum(lq1_ref[...] * lk1_ref[...], axis=-1, keepdims=True))
           - jnp.exp(jnp.sum(lq2_ref[...] * lk2_ref[...], axis=-1, keepdims=True))
           + lam_init)
    ot = acc1_ref[...] / l1 - lam * (acc2_ref[...] / l2)
    ms = jnp.mean(ot * ot, axis=0, keepdims=True)
    o = (ot * lax.rsqrt(ms + EPS)).T
    o_ref[...] = (o * g_ref[...] * (1.0 - lam_init)).astype(BF16)


def _att(u, lq1, lk1, lq2, lk2, g_subln, lam_init, batch, seq):
    m = u.shape[0]
    t = _tile(seq, ATT_TQ)
    G = _tile(seq, ATT_GROUP)
    nq = seq // t
    e = 2 * A_DH
    assert G % t == 0
    vec = pl.BlockSpec((1, A_DH), lambda b, h, i: (0, 0))
    return pl.pallas_call(
        functools.partial(_att_kernel, lam_init=lam_init, t=t, G=G),
        grid=(batch, A_HEADS, nq),
        in_specs=[vec, vec, vec, vec,
                  pl.BlockSpec((1, e), lambda b, h, i: (0, 0)),
                  pl.BlockSpec((t, e), lambda b, h, i: (b * nq + i, QA0 // e + h)),
                  pl.BlockSpec((seq, e), lambda b, h, i: (b, KA0 // e + h)),
                  pl.BlockSpec((seq, e), lambda b, h, i: (b, VA0 // e + h))],
        out_specs=pl.BlockSpec((t, e), lambda b, h, i: (b * nq + i, h)),
        out_shape=jax.ShapeDtypeStruct((m, A_WIDTH), BF16),
        scratch_shapes=[pltpu.VMEM((seq, t), F32), pltpu.VMEM((seq, t), F32),
                        pltpu.VMEM((e, t), F32), pltpu.VMEM((e, t), F32),
                        pltpu.VMEM((G, t), F32)],
        compiler_params=_cparams("arbitrary", "arbitrary", "arbitrary"), name="diff_attn",
    )(lq1.reshape(1, A_DH), lk1.reshape(1, A_DH), lq2.reshape(1, A_DH), lk2.reshape(1, A_DH),
      g_subln.reshape(1, e), u, u, u)


def _gla_kernel(z_ref, wg_ref, bg_ref, gn_ref, q_ref, k_ref, v_ref, gb_ref, o_ref, *, seq):
    L = CHUNK
    wg = wg_ref[...]
    bg = bg_ref[...]
    gn = gn_ref[...]
    rows = lax.broadcasted_iota(jnp.int32, (L, B_DK), 0)
    ti = lax.broadcasted_iota(jnp.int32, (L, L), 0)
    si = lax.broadcasted_iota(jnp.int32, (L, L), 1)
    qscale = B_DK ** -0.5

    def chunk(c, state_t):
        r0 = pl.multiple_of(c * L, L)
        x = _dot(z_ref[pl.ds(r0, L), :].astype(BF16), wg) + bg
        la = (jnp.minimum(x, 0.0) - jnp.log(1.0 + jnp.exp(-jnp.abs(x)))) * (1.0 / GATE_TAU)
        b = la
        sh = 1
        while sh < L:
            b = b + jnp.where(rows >= sh, pltpu.roll(b, sh, axis=0), 0.0)
            sh *= 2
        b_last = b[L - 1:L, :]
        qf = q_ref[pl.ds(r0, L), :].astype(F32) * qscale
        kc = k_ref[pl.ds(r0, L), :]
        vc = v_ref[pl.ds(r0, L), :]

        a = jnp.zeros((L, L), F32)
        for s in range(L):
            e = jnp.exp(jnp.minimum(b - b[s:s + 1, :], 0.0))
            rs = _dot_nt((qf * e).astype(BF16), kc)
            a = jnp.where(si == s, rs, a)
        a = jnp.where(ti >= si, a, 0.0)

        qe = (qf * jnp.exp(b)).astype(BF16)
        o = _dot(a.astype(BF16), vc) + _dot_nt(qe, state_t.astype(BF16))
        ke = (kc.astype(F32) * jnp.exp(b_last - b)).astype(BF16)
        new_state = jnp.exp(b_last) * state_t + _dot_tn(vc, ke)

        gate = gb_ref[pl.ds(r0, L), :].astype(F32)
        y = _rms(o, gn) * (gate / (1.0 + jnp.exp(-gate)))
        o_ref[pl.ds(r0, L), :] = y.astype(BF16)
        return new_state

    lax.fori_loop(0, seq // L, chunk, jnp.zeros((B_DV, B_DK), F32))


def _gla(u, z, w_gate, b_gate, g_gla, batch, seq):
    m = u.shape[0]
    return pl.pallas_call(
        functools.partial(_gla_kernel, seq=seq),
        grid=(batch, B_HEADS),
        in_specs=[pl.BlockSpec((seq, LANES), lambda b, h: (b, 0)),
                  pl.BlockSpec((LANES, B_DK), lambda b, h: (0, h)),
                  pl.BlockSpec((1, B_DK), lambda b, h: (0, h)),
                  pl.BlockSpec((1, B_DV), lambda b, h: (0, h)),
                  pl.BlockSpec((seq, B_DK), lambda b, h: (b, QB0 // B_DK + h)),
                  pl.BlockSpec((seq, B_DK), lambda b, h: (b, KB0 // B_DK + h)),
                  pl.BlockSpec((seq, B_DV), lambda b, h: (b, VB0 // B_DV + h)),
                  pl.BlockSpec((seq, B_DV), lambda b, h: (b, GB0 // B_DV + h))],
        out_specs=pl.BlockSpec((seq, B_DV), lambda b, h: (b, h)),
        out_shape=jax.ShapeDtypeStruct((m, B_WIDTH), BF16),
        compiler_params=_cparams("parallel", "parallel"), name="gla",
    )(z, w_gate, b_gate.reshape(1, -1), g_gla.reshape(1, -1), u, u, u, u)


def _pool_kernel(u_ref, w_ref, sc_ref, o_ref, carry_ref, *, tr):
    ti = pl.program_id(1)

    @pl.when(ti == 0)
    def _():
        carry_ref[...] = jnp.zeros_like(carry_ref)

    x = u_ref[...].astype(F32)
    ext = jnp.concatenate([carry_ref[...], x], axis=0)
    carry_ref[...] = x[tr - POOL_HALO:, :]
    pos = ti * tr + lax.broadcasted_iota(jnp.int32, (tr, 1), 0)
    for g, w in enumerate(POOL_WINDOWS):
        lo, hi = g * POOL_CH, (g + 1) * POOL_CH
        s = ext[:, lo:hi]
        sh = 1
        while sh < w:
            s = s + pltpu.roll(s, sh, axis=0)
            sh *= 2
        cnt = jnp.minimum(pos + 1, w).astype(F32)
        d = s[POOL_HALO:, :] / cnt - x[:, lo:hi]
        y = _dot(d.astype(BF16), w_ref[g]) * sc_ref[:, lo:hi]
        o_ref[:, lo:hi] = y.astype(BF16)


def _pool(u, w_pool, pool_scale, batch, seq):
    m = u.shape[0]
    tr = _tile(seq, 512)
    nt = seq // tr
    assert tr >= POOL_HALO and max(POOL_WINDOWS) <= POOL_HALO
    return pl.pallas_call(
        functools.partial(_pool_kernel, tr=tr),
        grid=(batch, nt),
        in_specs=[pl.BlockSpec((tr, C_WIDTH), lambda b, i: (b * nt + i, UC0 // C_WIDTH)),
                  pl.BlockSpec((N_POOL, POOL_CH, POOL_CH), lambda b, i: (0, 0, 0)),
                  pl.BlockSpec((1, C_WIDTH), lambda b, i: (0, 0))],
        out_specs=pl.BlockSpec((tr, C_WIDTH), lambda b, i: (b * nt + i, 0)),
        out_shape=jax.ShapeDtypeStruct((m, C_WIDTH), BF16),
        scratch_shapes=[pltpu.VMEM((POOL_HALO, C_WIDTH), F32)],
        compiler_params=_cparams("parallel", "arbitrary"), name="pool",
    )(u, w_pool, pool_scale.reshape(1, C_WIDTH))


def _mm_out_kernel(a_ref, b_ref, c_ref, wa_ref, wb_ref, wc_ref, o_ref):
    o_ref[...] = (_dot(a_ref[...], wa_ref[...].astype(BF16))
                  + _dot(b_ref[...], wb_ref[...].astype(BF16))
                  + _dot(c_ref[...], wc_ref[...].astype(BF16)))


def _mm_out(oa, ob, oc, w_o, layer):
    m = oa.shape[0]
    n = w_o.shape[2]
    tm = _tile(m, 1024)
    tn = _tile(n, 512)
    assert A_WIDTH % B_WIDTH == 0 and B_WIDTH == C_WIDTH
    return pl.pallas_call(
        _mm_out_kernel, grid=(m // tm, n // tn),
        in_specs=[pl.BlockSpec((tm, A_WIDTH), lambda i, j: (i, 0)),
                  pl.BlockSpec((tm, B_WIDTH), lambda i, j: (i, 0)),
                  pl.BlockSpec((tm, C_WIDTH), lambda i, j: (i, 0)),
                  pl.BlockSpec((None, A_WIDTH, tn), lambda i, j: (layer, 0, j)),
                  pl.BlockSpec((None, B_WIDTH, tn), lambda i, j: (layer, A_WIDTH // B_WIDTH, j)),
                  pl.BlockSpec((None, C_WIDTH, tn), lambda i, j: (layer, A_WIDTH // C_WIDTH + 1, j))],
        out_specs=pl.BlockSpec((tm, tn), lambda i, j: (i, j)),
        out_shape=jax.ShapeDtypeStruct((m, n), F32),
        compiler_params=_cparams("parallel", "parallel"), name="mm_out",
    )(oa, ob, oc, w_o, w_o, w_o)


GELU_C = math.sqrt(2.0 / math.pi)


def _mm_up_kernel(x_ref, wg_ref, wv_ref, cg_ref, cv_ref, bg_ref, bv_ref, o_ref,
                  tail_ref, yg_ref, yv_ref, *, tm, nj, tiles_per_seq):
    i = pl.program_id(0)
    j = pl.program_id(1)
    seq_start = (i % tiles_per_seq) == 0
    pad = 8

    @pl.when(seq_start)
    def _():
        tail_ref[j] = jnp.zeros((pad, tail_ref.shape[2]), F32)
        tail_ref[nj + j] = jnp.zeros((pad, tail_ref.shape[2]), F32)

    def half(w_ref, c_ref, b_ref, y_ref, slot, post):
        y = _dot(x_ref[...], w_ref[...].astype(BF16))
        y_ref[0:pad, :] = tail_ref[slot]
        y_ref[pad:, :] = y
        tail_ref[slot] = y[tm - pad:, :]
        cw = c_ref[...] * post
        return (b_ref[...] * post + cw[2:3, :] * y + cw[1:2, :] * y_ref[pl.ds(pad - 1, tm), :]
                + cw[0:1, :] * y_ref[pl.ds(pad - 2, tm), :])

    zg = half(wg_ref, cg_ref, bg_ref, yg_ref, j, 1.0)
    zv = half(wv_ref, cv_ref, bv_ref, yv_ref, nj + j, 0.5)
    t = jnp.tanh(zg * (GELU_C + (GELU_C * 0.044715) * (zg * zg)))
    o_ref[...] = ((zg + zg * t) * zv).astype(BF16)


def _mm_up(hn, w_up, w_conv, b_conv, layer, seq):
    m, d = hn.shape
    f = w_up.shape[2] // 2
    tm = _tile(seq, UP_TM)
    tn = MXU_COLS if f % MXU_COLS == 0 else f
    nj = f // tn
    b3 = b_conv.reshape(b_conv.shape[0], 1, 2 * f)
    return pl.pallas_call(
        functools.partial(_mm_up_kernel, tm=tm, nj=nj, tiles_per_seq=seq // tm),
        grid=(m // tm, nj),
        in_specs=[pl.BlockSpec((tm, d), lambda i, j: (i, 0), pipeline_mode=pl.Buffered(1)),
                  pl.BlockSpec((None, d, tn), lambda i, j: (layer, 0, j)),
                  pl.BlockSpec((None, d, tn), lambda i, j: (layer, 0, nj + j)),
                  pl.BlockSpec((None, CONV_K, tn), lambda i, j: (layer, 0, j)),
                  pl.BlockSpec((None, CONV_K, tn), lambda i, j: (layer, 0, nj + j)),
                  pl.BlockSpec((None, 1, tn), lambda i, j: (layer, 0, j)),
                  pl.BlockSpec((None, 1, tn), lambda i, j: (layer, 0, nj + j))],
        out_specs=pl.BlockSpec((tm, tn), lambda i, j: (i, j)),
        out_shape=jax.ShapeDtypeStruct((m, f), BF16),
        scratch_shapes=[pltpu.VMEM((2 * nj, 8, tn), F32),
                        pltpu.VMEM((tm + 8, tn), F32), pltpu.VMEM((tm + 8, tn), F32)],
        compiler_params=_cparams("arbitrary", "arbitrary"), name="mm_up",
    )(hn, w_up, w_up, w_conv, w_conv, b3, b3)


def _mm_kernel(x_ref, w_ref, o_ref):
    o_ref[...] = _dot(x_ref[...], w_ref[...]).astype(o_ref.dtype)


def _mm_down(x, w):
    m, k = x.shape
    n = w.shape[1]
    tm = _tile(m, 1024)
    tn = _tile(n, MXU_COLS)
    return pl.pallas_call(
        _mm_kernel, grid=(m // tm, n // tn),
        in_specs=[pl.BlockSpec((tm, k), lambda i, j: (i, 0), pipeline_mode=pl.Buffered(1)),
                  pl.BlockSpec((k, tn), lambda i, j: (0, j))],
        out_specs=pl.BlockSpec((tm, tn), lambda i, j: (i, j)),
        out_shape=jax.ShapeDtypeStruct((m, n), F32),
        compiler_params=_cparams("parallel", "parallel"), name="mm_down",
    )(x, w)


def kernel(x, w_in, w_gate_lr2, b_gate, lam_q1, lam_k1, lam_q2, lam_k2, g_subln, g_gla, w_pool, pool_scale, w_o, w_up, w_conv, b_conv, w_down, g_pre_mix, g_post_mix, g_pre_ffn, g_post_ffn):
    batch, seq, d = x.shape
    depth = w_in.shape[0]
    m = batch * seq
    assert seq % CHUNK == 0 and w_in.shape[2] == U_COLS + GATE_RANK
    zl0 = GB0 + B_WIDTH

    h = x.reshape(m, d)
    branch = None
    for l in range(depth):
        wl = w_in[l]
        w_c = wl[:, zl0 + GATE_RANK:].astype(BF16)
        w_zlr = jnp.pad(wl[:, zl0:zl0 + GATE_RANK], ((0, 0), (0, LANES - GATE_RANK))).astype(BF16)
        w_gate = jnp.pad(w_gate_lr2[l], ((0, LANES - GATE_RANK), (0, 0))).astype(BF16)
        w_down_b = w_down[l].astype(BF16)

        if l == 0:
            hn = _norm_first(h, g_pre_mix[0])
        else:
            h, hn = _resnorm(h, branch, g_post_ffn[l - 1], g_pre_mix[l])
        u, z = _mm_in(hn, w_in, l, w_c, w_zlr)
        lam_init = 0.8 - 0.6 * math.exp(-0.3 * l)
        oa = _att(u, lam_q1[l], lam_k1[l], lam_q2[l], lam_k2[l], g_subln[l], lam_init, batch, seq)
        ob = _gla(u, z, w_gate, b_gate[l], g_gla[l], batch, seq)
        oc = _pool(u, w_pool[l].astype(BF16), pool_scale[l], batch, seq)
        mix = _mm_out(oa, ob, oc, w_o, l)

        h, hn = _resnorm(h, mix, g_post_mix[l], g_pre_ffn[l])
        f_in = _mm_up(hn, w_up, w_conv, b_conv, l, seq)
        branch = _mm_down(f_in, w_down_b)
    h = _resnorm_last(h, branch, g_post_ffn[depth - 1])
    return h.reshape(batch, seq, d)
```

```python
import functools
import math

import jax
import jax.numpy as jnp
from jax import lax
from jax.experimental import pallas as pl
from jax.experimental.pallas import tpu as pltpu

F32 = jnp.float32
BF16 = jnp.bfloat16

CHUNK = 64
EPS = 1e-6
A_HEADS = 8
A_DH = 128
A_WIDTH = A_HEADS * 2 * A_DH
B_HEADS = 4
B_DK = 128
B_DV = 256
B_WIDTH = B_HEADS * B_DV
GATE_RANK = 16
GATE_TAU = 16.0
POOL_WINDOWS = (2, 4, 8, 16)
N_POOL = len(POOL_WINDOWS)
POOL_CH = 256
C_WIDTH = N_POOL * POOL_CH
CONV_K = 3

QA0 = 0
KA0 = A_WIDTH
VA0 = 2 * A_WIDTH
QB0 = 3 * A_WIDTH
KB0 = QB0 + B_HEADS * B_DK
VB0 = KB0 + B_HEADS * B_DK
GB0 = VB0 + B_WIDTH
UC0 = GB0 + B_WIDTH
U_COLS = UC0 + C_WIDTH

LANES = 128
MXU_COLS = 256
VMEM_LIMIT = 56 * 1024 * 1024
POOL_HALO = 16
NEG_BIG = -1e30
LOG2E = math.log2(math.e)
ATT_TQ = 256
ATT_GROUP = 1024
UP_TM = 2048
GLA_SUB = 16
TAIL_ROWS = 1152


def _cparams(*sem):
    return pltpu.CompilerParams(dimension_semantics=sem, vmem_limit_bytes=VMEM_LIMIT)


def _dot(a, b):
    return jnp.dot(a, b, preferred_element_type=F32)


def _dot_nt(a, b):
    return lax.dot_general(a, b, (((1,), (1,)), ((), ())), preferred_element_type=F32)


def _dot_tn(a, b):
    return lax.dot_general(a, b, (((0,), (0,)), ((), ())), preferred_element_type=F32)


def _rms(x, g):
    ms = jnp.mean(x * x, axis=-1, keepdims=True)
    return x * lax.rsqrt(ms + EPS) * g


def _tile(n, want):
    t = min(n, want)
    assert n % t == 0, (n, want)
    return t


def _norm_first_kernel(h_ref, gpre_ref, hn_ref):
    hn_ref[...] = _rms(h_ref[...], gpre_ref[...]).astype(BF16)


def _resnorm_kernel(h_ref, br_ref, gpost_ref, gpre_ref, hout_ref, hn_ref):
    h = h_ref[...] + _rms(br_ref[...], gpost_ref[...])
    hout_ref[...] = h
    hn_ref[...] = _rms(h, gpre_ref[...]).astype(BF16)


def _resnorm_last_kernel(h_ref, br_ref, gpost_ref, hout_ref):
    hout_ref[...] = h_ref[...] + _rms(br_ref[...], gpost_ref[...])


def _norm_first(h, gpre):
    m, d = h.shape
    tr = _tile(m, 256)
    row = pl.BlockSpec((tr, d), lambda i: (i, 0))
    vec = pl.BlockSpec((1, d), lambda i: (0, 0))
    return pl.pallas_call(
        _norm_first_kernel, grid=(m // tr,), in_specs=[row, vec], out_specs=row,
        out_shape=jax.ShapeDtypeStruct((m, d), BF16),
        compiler_params=_cparams("parallel"), name="norm_first",
    )(h, gpre.reshape(1, d))


def _resnorm(h, br, gpost, gpre):
    m, d = h.shape
    tr = _tile(m, 256)
    row = pl.BlockSpec((tr, d), lambda i: (i, 0))
    vec = pl.BlockSpec((1, d), lambda i: (0, 0))
    return pl.pallas_call(
        _resnorm_kernel, grid=(m // tr,), in_specs=[row, row, vec, vec], out_specs=[row, row],
        out_shape=[jax.ShapeDtypeStruct((m, d), F32), jax.ShapeDtypeStruct((m, d), BF16)],
        compiler_params=_cparams("parallel"), name="resnorm",
    )(h, br, gpost.reshape(1, d), gpre.reshape(1, d))


def _resnorm_last(h, br, gpost):
    m, d = h.shape
    tr = _tile(m, 256)
    row = pl.BlockSpec((tr, d), lambda i: (i, 0))
    vec = pl.BlockSpec((1, d), lambda i: (0, 0))
    return pl.pallas_call(
        _resnorm_last_kernel, grid=(m // tr,), in_specs=[row, row, vec], out_specs=row,
        out_shape=jax.ShapeDtypeStruct((m, d), F32),
        compiler_params=_cparams("parallel"), name="resnorm_last",
    )(h, br, gpost.reshape(1, d))


def _mm_in_kernel(x_ref, w_ref, u_ref):
    u_ref[...] = _dot_nt(x_ref[...], w_ref[...].astype(BF16)).astype(BF16)


def _mm_in(hn, w_in_t, layer):
    m, d = hn.shape
    tm = _tile(m, 1024)
    tn = 512
    assert UC0 % tn == 0
    return pl.pallas_call(
        _mm_in_kernel, grid=(m // tm, UC0 // tn),
        in_specs=[pl.BlockSpec((tm, d), lambda i, j: (i, 0)),
                  pl.BlockSpec((None, tn, d), lambda i, j: (layer, j, 0))],
        out_specs=pl.BlockSpec((tm, tn), lambda i, j: (i, j)),
        out_shape=jax.ShapeDtypeStruct((m, UC0), BF16),
        compiler_params=_cparams("parallel", "arbitrary"), name="mm_in",
    )(hn, w_in_t)


def _mm_tail_kernel(x_ref, w_ref, z_ref, uc_ref):
    x = x_ref[...]
    z_ref[...] = _dot_nt(x, w_ref[0:LANES, :].astype(BF16))
    uc_ref[...] = _dot_nt(x, w_ref[GATE_RANK:GATE_RANK + C_WIDTH, :].astype(BF16)).astype(BF16)


def _mm_tail(hn, w_in_t, layer):
    m, d = hn.shape
    tm = _tile(m, 1024)
    rows = TAIL_ROWS
    assert UC0 % rows == 0 and rows >= GATE_RANK + C_WIDTH and GATE_RANK % 8 == 0
    return pl.pallas_call(
        _mm_tail_kernel, grid=(m // tm,),
        in_specs=[pl.BlockSpec((tm, d), lambda i: (i, 0)),
                  pl.BlockSpec((None, rows, d), lambda i: (layer, UC0 // rows, 0),
                               pipeline_mode=pl.Buffered(1))],
        out_specs=[pl.BlockSpec((tm, LANES), lambda i: (i, 0)),
                   pl.BlockSpec((tm, C_WIDTH), lambda i: (i, 0))],
        out_shape=[jax.ShapeDtypeStruct((m, LANES), F32), jax.ShapeDtypeStruct((m, C_WIDTH), BF16)],
        compiler_params=_cparams("arbitrary"), name="mm_tail",
    )(hn, w_in_t)


def _att_kernel(lq1_ref, lk1_ref, lq2_ref, lk2_ref, g_ref, q_ref, k_ref, v_ref, o_ref,
                s1_ref, s2_ref, acc1_ref, acc2_ref, bias_ref, *, lam_init, t, G):
    head = pl.program_id(1)
    qi = pl.program_id(2)
    qscale = LOG2E / math.sqrt(A_DH)
    slope = lax.bitcast_convert_type(
        jnp.full((1, 1), (126 - head) << 23, jnp.int32), F32) * LOG2E

    q = q_ref[...].astype(F32) * qscale
    qt1 = q[:, :A_DH].T.astype(BF16)
    qt2 = q[:, A_DH:].T.astype(BF16)

    def fold8(x, op):
        return op(x.reshape(G // 8, 8, t), axis=0)

    @pl.when(qi == 0)
    def _():
        kk = lax.broadcasted_iota(jnp.int32, (G, t), 0)
        bias_ref[...] = slope * kk.astype(F32)

    qs = qi * t
    gl = qs // G

    def offset(g):
        return slope * (g * G - qs).astype(F32)

    def score_body(g, carry):
        m1, m2 = carry
        k0 = pl.multiple_of(g * G, G)
        kt = k_ref[pl.ds(k0, G), :]
        s1 = _dot(kt[:, :A_DH], qt1) + bias_ref[...]
        s2 = _dot(kt[:, A_DH:], qt2) + bias_ref[...]
        s1_ref[pl.ds(k0, G), :] = s1
        s2_ref[pl.ds(k0, G), :] = s2
        return (jnp.maximum(m1, fold8(s1, jnp.max) + offset(g)),
                jnp.maximum(m2, fold8(s2, jnp.max) + offset(g)))

    l0 = pl.multiple_of(gl * G, G)
    jd = (qs - l0) // t
    kl = k_ref[pl.ds(l0, G), :]
    sl1 = _dot(kl[:, :A_DH], qt1) + bias_ref[...]
    sl2 = _dot(kl[:, A_DH:], qt2) + bias_ref[...]
    s1_ref[pl.ds(l0, G), :] = sl1
    s2_ref[pl.ds(l0, G), :] = sl2

    def fold8t(x, op):
        return op(x.reshape(t // 8, 8, t), axis=0)

    m1 = jnp.full((8, t), NEG_BIG, F32)
    m2 = m1
    for j in range(G // t - 1):
        skip = jnp.where(j < jd, 0.0, NEG_BIG)
        m1 = jnp.maximum(m1, fold8t(sl1[j * t:(j + 1) * t], jnp.max) + skip)
        m2 = jnp.maximum(m2, fold8t(sl2[j * t:(j + 1) * t], jnp.max) + skip)
    cshift = CHUNK.bit_length() - 1
    kd = lax.broadcasted_iota(jnp.int32, (t, t), 0)
    rd = lax.broadcasted_iota(jnp.int32, (t, t), 1)
    allowed = (kd >> cshift) <= (rd >> cshift)
    corr = slope * (rd - jnp.abs(rd - kd) - kd).astype(F32)
    d0 = pl.multiple_of(qs, t)
    sd1 = jnp.where(allowed, s1_ref[pl.ds(d0, t), :] + corr, NEG_BIG)
    sd2 = jnp.where(allowed, s2_ref[pl.ds(d0, t), :] + corr, NEG_BIG)
    s1_ref[pl.ds(d0, t), :] = sd1
    s2_ref[pl.ds(d0, t), :] = sd2
    for j in range(1, G // t):
        @pl.when(j > jd)
        def _():
            j0 = pl.multiple_of(l0 + j * t, t)
            s1_ref[pl.ds(j0, t), :] = jnp.full((t, t), NEG_BIG, F32)
            s2_ref[pl.ds(j0, t), :] = jnp.full((t, t), NEG_BIG, F32)
    m1 = jnp.maximum(m1, fold8t(sd1, jnp.max)) + offset(gl)
    m2 = jnp.maximum(m2, fold8t(sd2, jnp.max)) + offset(gl)
    m1, m2 = lax.fori_loop(0, gl, score_body, (m1, m2))
    m1 = jnp.max(m1, axis=0, keepdims=True)
    m2 = jnp.max(m2, axis=0, keepdims=True)

    acc1_ref[...] = jnp.zeros_like(acc1_ref)
    acc2_ref[...] = jnp.zeros_like(acc2_ref)

    def pv_body(g, carry):
        l1, l2 = carry
        k0 = pl.multiple_of(g * G, G)
        vt = v_ref[pl.ds(k0, G), :]
        p1 = jnp.exp2(s1_ref[pl.ds(k0, G), :] - (m1 - offset(g)))
        p2 = jnp.exp2(s2_ref[pl.ds(k0, G), :] - (m2 - offset(g)))
        acc1_ref[...] += _dot_tn(vt, p1.astype(BF16))
        acc2_ref[...] += _dot_tn(vt, p2.astype(BF16))
        return l1 + fold8(p1, jnp.sum), l2 + fold8(p2, jnp.sum)

    zero8 = jnp.zeros((8, t), F32)
    l1, l2 = lax.fori_loop(0, gl + 1, pv_body, (zero8, zero8))
    l1 = jnp.sum(l1, axis=0, keepdims=True)
    l2 = jnp.sum(l2, axis=0, keepdims=True)

    lam = (jnp.exp(jnp.sum(lq1_ref[...] * lk1_ref[...], axis=-1, keepdims=True))
           - jnp.exp(jnp.sum(lq2_ref[...] * lk2_ref[...], axis=-1, keepdims=True))
           + lam_init)
    ot = acc1_ref[...] / l1 - lam * (acc2_ref[...] / l2)
    ms = jnp.mean(ot * ot, axis=0, keepdims=True)
    o = (ot * lax.rsqrt(ms + EPS)).T
    o_ref[...] = (o * g_ref[...] * (1.0 - lam_init)).astype(BF16)


def _att(u, lq1, lk1, lq2, lk2, g_subln, lam_init, batch, seq):
    m = u.shape[0]
    t = _tile(seq, ATT_TQ)
    G = _tile(seq, ATT_GROUP)
    nq = seq // t
    e = 2 * A_DH
    assert G % t == 0
    vec = pl.BlockSpec((1, A_DH), lambda b, h, i: (0, 0))
    return pl.pallas_call(
        functools.partial(_att_kernel, lam_init=lam_init, t=t, G=G),
        grid=(batch, A_HEADS, nq),
        in_specs=[vec, vec, vec, vec,
                  pl.BlockSpec((1, e), lambda b, h, i: (0, 0)),
                  pl.BlockSpec((t, e), lambda b, h, i: (b * nq + i, QA0 // e + h)),
                  pl.BlockSpec((seq, e), lambda b, h, i: (b, KA0 // e + h)),
                  pl.BlockSpec((seq, e), lambda b, h, i: (b, VA0 // e + h))],
        out_specs=pl.BlockSpec((t, e), lambda b, h, i: (b * nq + i, h)),
        out_shape=jax.ShapeDtypeStruct((m, A_WIDTH), BF16),
        scratch_shapes=[pltpu.VMEM((seq, t), F32), pltpu.VMEM((seq, t), F32),
                        pltpu.VMEM((e, t), F32), pltpu.VMEM((e, t), F32),
                        pltpu.VMEM((G, t), F32)],
        compiler_params=_cparams("arbitrary", "arbitrary", "arbitrary"), name="diff_attn",
    )(lq1.reshape(1, A_DH), lk1.reshape(1, A_DH), lq2.reshape(1, A_DH), lk2.reshape(1, A_DH),
      g_subln.reshape(1, e), u, u, u)


def _gla_kernel(z_ref, wg_ref, bg_ref, gn_ref, q_ref, k_ref, v_ref, gb_ref, o_ref, *, seq):
    L = CHUNK
    wg = wg_ref[...]
    bg = bg_ref[...]
    gn = gn_ref[...]
    rows = lax.broadcasted_iota(jnp.int32, (L, B_DK), 0)
    ti = lax.broadcasted_iota(jnp.int32, (L, L), 0)
    si = lax.broadcasted_iota(jnp.int32, (L, L), 1)
    SUB = GLA_SUB
    assert L == 4 * SUB
    sshift = SUB.bit_length() - 1
    bi = ti >> sshift
    bj = si >> sshift
    sd = si - (ti - (ti & (SUB - 1)))
    m_diag = (bi == bj) & (si <= ti)
    m_near = (bi == bj + 1) & ((bi & 1) == 1)
    m_far = (bi >= 2) & (bj <= 1)
    qscale = B_DK ** -0.5

    def chunk(c, state_t):
        r0 = pl.multiple_of(c * L, L)
        x = _dot(z_ref[pl.ds(r0, L), :].astype(BF16), wg) + bg
        la = (jnp.minimum(x, 0.0) - jnp.log(1.0 + jnp.exp(-jnp.abs(x)))) * (1.0 / GATE_TAU)
        b = la
        sh = 1
        while sh < L:
            b = b + jnp.where(rows >= sh, pltpu.roll(b, sh, axis=0), 0.0)
            sh *= 2
        b = b * LOG2E
        b_last = b[L - 1:L, :]
        qf = q_ref[pl.ds(r0, L), :].astype(F32) * qscale
        kc = k_ref[pl.ds(r0, L), :]
        kf = kc.astype(F32)
        vc = v_ref[pl.ds(r0, L), :]

        def rescaled(ref):
            qr = (qf * jnp.exp2(jnp.minimum(b - ref, 0.0))).astype(BF16)
            kr = (kf * jnp.exp2(jnp.minimum(ref - b, 0.0))).astype(BF16)
            return _dot_nt(qr, kr)

        half = L // 2
        a_far = rescaled(b[half - 1:half, :])
        ref_near = jnp.where(rows < half, b[SUB - 1:SUB, :], b[half + SUB - 1:half + SUB, :])
        a_near = rescaled(ref_near)
        a = jnp.zeros((L, L), F32)
        for s in range(SUB):
            ref = jnp.concatenate(
                [jnp.broadcast_to(b[i * SUB + s:i * SUB + s + 1, :], (SUB, B_DK)) for i in range(L // SUB)],
                axis=0)
            e = jnp.exp2(jnp.minimum(b - ref, 0.0))
            rs = _dot_nt((qf * e).astype(BF16), kc)
            a = jnp.where(sd == s, rs, a)
        a = jnp.where(m_far, a_far, jnp.where(m_near, a_near, jnp.where(m_diag, a, 0.0)))

        qe = (qf * jnp.exp2(b)).astype(BF16)
        o = _dot(a.astype(BF16), vc) + _dot_nt(qe, state_t.astype(BF16))
        ke = (kf * jnp.exp2(b_last - b)).astype(BF16)
        new_state = jnp.exp2(b_last) * state_t + _dot_tn(vc, ke)

        gate = gb_ref[pl.ds(r0, L), :].astype(F32)
        y = _rms(o, gn) * (gate / (1.0 + jnp.exp(-gate)))
        o_ref[pl.ds(r0, L), :] = y.astype(BF16)
        return new_state

    nc = seq // L
    unroll = 4 if nc % 4 == 0 else 1

    def chunks(c, state_t):
        for k in range(unroll):
            state_t = chunk(c * unroll + k, state_t)
        return state_t

    lax.fori_loop(0, nc // unroll, chunks, jnp.zeros((B_DV, B_DK), F32))


def _gla(u, z, w_gate, b_gate, g_gla, batch, seq):
    m = u.shape[0]
    return pl.pallas_call(
        functools.partial(_gla_kernel, seq=seq),
        grid=(batch, B_HEADS),
        in_specs=[pl.BlockSpec((seq, LANES), lambda b, h: (b, 0)),
                  pl.BlockSpec((LANES, B_DK), lambda b, h: (0, h)),
                  pl.BlockSpec((1, B_DK), lambda b, h: (0, h)),
                  pl.BlockSpec((1, B_DV), lambda b, h: (0, h)),
                  pl.BlockSpec((seq, B_DK), lambda b, h: (b, QB0 // B_DK + h)),
                  pl.BlockSpec((seq, B_DK), lambda b, h: (b, KB0 // B_DK + h)),
                  pl.BlockSpec((seq, B_DV), lambda b, h: (b, VB0 // B_DV + h)),
                  pl.BlockSpec((seq, B_DV), lambda b, h: (b, GB0 // B_DV + h))],
        out_specs=pl.BlockSpec((seq, B_DV), lambda b, h: (b, h)),
        out_shape=jax.ShapeDtypeStruct((m, B_WIDTH), BF16),
        compiler_params=_cparams("parallel", "parallel"), name="gla",
    )(z, w_gate, b_gate.reshape(1, -1), g_gla.reshape(1, -1), u, u, u, u)


def _pool_kernel(u_ref, w_ref, sc_ref, o_ref, carry_ref, *, tr):
    ti = pl.program_id(1)

    @pl.when(ti == 0)
    def _():
        carry_ref[...] = jnp.zeros_like(carry_ref)

    x = u_ref[...].astype(F32)
    ext = jnp.concatenate([carry_ref[...], x], axis=0)
    carry_ref[...] = x[tr - POOL_HALO:, :]
    pos = ti * tr + lax.broadcasted_iota(jnp.int32, (tr, 1), 0)
    for g, w in enumerate(POOL_WINDOWS):
        lo, hi = g * POOL_CH, (g + 1) * POOL_CH
        s = ext[:, lo:hi]
        sh = 1
        while sh < w:
            s = s + pltpu.roll(s, sh, axis=0)
            sh *= 2
        cnt = jnp.minimum(pos + 1, w).astype(F32)
        d = s[POOL_HALO:, :] / cnt - x[:, lo:hi]
        y = _dot(d.astype(BF16), w_ref[g]) * sc_ref[:, lo:hi]
        o_ref[:, lo:hi] = y.astype(BF16)


def _pool(uc, w_pool, pool_scale, batch, seq):
    m = uc.shape[0]
    tr = _tile(seq, 512)
    nt = seq // tr
    assert tr >= POOL_HALO and max(POOL_WINDOWS) <= POOL_HALO
    return pl.pallas_call(
        functools.partial(_pool_kernel, tr=tr),
        grid=(batch, nt),
        in_specs=[pl.BlockSpec((tr, C_WIDTH), lambda b, i: (b * nt + i, 0)),
                  pl.BlockSpec((N_POOL, POOL_CH, POOL_CH), lambda b, i: (0, 0, 0)),
                  pl.BlockSpec((1, C_WIDTH), lambda b, i: (0, 0))],
        out_specs=pl.BlockSpec((tr, C_WIDTH), lambda b, i: (b * nt + i, 0)),
        out_shape=jax.ShapeDtypeStruct((m, C_WIDTH), BF16),
        scratch_shapes=[pltpu.VMEM((POOL_HALO, C_WIDTH), F32)],
        compiler_params=_cparams("parallel", "arbitrary"), name="pool",
    )(uc, w_pool, pool_scale.reshape(1, C_WIDTH))


def _mm_out_kernel(a_ref, b_ref, c_ref, wa_ref, wb_ref, wc_ref, o_ref):
    o_ref[...] = (_dot(a_ref[...], wa_ref[...].astype(BF16))
                  + _dot(b_ref[...], wb_ref[...].astype(BF16))
                  + _dot(c_ref[...], wc_ref[...].astype(BF16)))


def _mm_out(oa, ob, oc, w_o, layer):
    m = oa.shape[0]
    n = w_o.shape[2]
    tm = _tile(m, 1024)
    tn = _tile(n, 512)
    assert A_WIDTH % B_WIDTH == 0 and B_WIDTH == C_WIDTH
    return pl.pallas_call(
        _mm_out_kernel, grid=(m // tm, n // tn),
        in_specs=[pl.BlockSpec((tm, A_WIDTH), lambda i, j: (i, 0)),
                  pl.BlockSpec((tm, B_WIDTH), lambda i, j: (i, 0)),
                  pl.BlockSpec((tm, C_WIDTH), lambda i, j: (i, 0)),
                  pl.BlockSpec((None, A_WIDTH, tn), lambda i, j: (layer, 0, j)),
                  pl.BlockSpec((None, B_WIDTH, tn), lambda i, j: (layer, A_WIDTH // B_WIDTH, j)),
                  pl.BlockSpec((None, C_WIDTH, tn), lambda i, j: (layer, A_WIDTH // C_WIDTH + 1, j))],
        out_specs=pl.BlockSpec((tm, tn), lambda i, j: (i, j)),
        out_shape=jax.ShapeDtypeStruct((m, n), F32),
        compiler_params=_cparams("parallel", "parallel"), name="mm_out",
    )(oa, ob, oc, w_o, w_o, w_o)


GELU_C = math.sqrt(2.0 / math.pi)


def _mm_up_kernel(x_ref, wg_ref, wv_ref, cg_ref, cv_ref, bg_ref, bv_ref, o_ref,
                  tail_ref, yg_ref, yv_ref, *, tm, nj, tiles_per_seq):
    i = pl.program_id(0)
    j = pl.program_id(1)
    seq_start = (i % tiles_per_seq) == 0
    pad = 8

    @pl.when(seq_start)
    def _():
        tail_ref[j] = jnp.zeros((pad, tail_ref.shape[2]), F32)
        tail_ref[nj + j] = jnp.zeros((pad, tail_ref.shape[2]), F32)

    def half(w_ref, c_ref, b_ref, y_ref, slot, post):
        y = _dot(x_ref[...], w_ref[...].astype(BF16))
        y_ref[0:pad, :] = tail_ref[slot]
        y_ref[pad:, :] = y
        tail_ref[slot] = y[tm - pad:, :]
        cw = c_ref[...] * post
        return (b_ref[...] * post + cw[2:3, :] * y + cw[1:2, :] * y_ref[pl.ds(pad - 1, tm), :]
                + cw[0:1, :] * y_ref[pl.ds(pad - 2, tm), :])

    zg = half(wg_ref, cg_ref, bg_ref, yg_ref, j, 1.0)
    zv = half(wv_ref, cv_ref, bv_ref, yv_ref, nj + j, 0.5)
    t = jnp.tanh(zg * (GELU_C + (GELU_C * 0.044715) * (zg * zg)))
    o_ref[...] = ((zg + zg * t) * zv).astype(BF16)


def _mm_up(hn, w_up, w_conv, b_conv, layer, seq):
    m, d = hn.shape
    f = w_up.shape[2] // 2
    tm = _tile(seq, UP_TM)
    tn = MXU_COLS if f % MXU_COLS == 0 else f
    nj = f // tn
    b3 = b_conv.reshape(b_conv.shape[0], 1, 2 * f)
    return pl.pallas_call(
        functools.partial(_mm_up_kernel, tm=tm, nj=nj, tiles_per_seq=seq // tm),
        grid=(m // tm, nj),
        in_specs=[pl.BlockSpec((tm, d), lambda i, j: (i, 0), pipeline_mode=pl.Buffered(1)),
                  pl.BlockSpec((None, d, tn), lambda i, j: (layer, 0, j)),
                  pl.BlockSpec((None, d, tn), lambda i, j: (layer, 0, nj + j)),
                  pl.BlockSpec((None, CONV_K, tn), lambda i, j: (layer, 0, j)),
                  pl.BlockSpec((None, CONV_K, tn), lambda i, j: (layer, 0, nj + j)),
                  pl.BlockSpec((None, 1, tn), lambda i, j: (layer, 0, j)),
                  pl.BlockSpec((None, 1, tn), lambda i, j: (layer, 0, nj + j))],
        out_specs=pl.BlockSpec((tm, tn), lambda i, j: (i, j)),
        out_shape=jax.ShapeDtypeStruct((m, f), BF16),
        scratch_shapes=[pltpu.VMEM((2 * nj, 8, tn), F32),
                        pltpu.VMEM((tm + 8, tn), F32), pltpu.VMEM((tm + 8, tn), F32)],
        compiler_params=_cparams("arbitrary", "arbitrary"), name="mm_up",
    )(hn, w_up, w_up, w_conv, w_conv, b3, b3)


def _mm_kernel(x_ref, w_ref, o_ref):
    o_ref[...] = _dot(x_ref[...], w_ref[...]).astype(o_ref.dtype)


def _mm_down(x, w, layer):
    m, k = x.shape
    n = w.shape[2]
    tm = _tile(m, 1024)
    tn = _tile(n, MXU_COLS)
    return pl.pallas_call(
        _mm_kernel, grid=(m // tm, n // tn),
        in_specs=[pl.BlockSpec((tm, k), lambda i, j: (i, 0), pipeline_mode=pl.Buffered(1)),
                  pl.BlockSpec((None, k, tn), lambda i, j: (layer, 0, j))],
        out_specs=pl.BlockSpec((tm, tn), lambda i, j: (i, j)),
        out_shape=jax.ShapeDtypeStruct((m, n), F32),
        compiler_params=_cparams("parallel", "parallel"), name="mm_down",
    )(x, w)


def kernel(x, w_in, w_gate_lr2, b_gate, lam_q1, lam_k1, lam_q2, lam_k2, g_subln, g_gla, w_pool, pool_scale, w_o, w_up, w_conv, b_conv, w_down, g_pre_mix, g_post_mix, g_pre_ffn, g_post_ffn):
    batch, seq, d = x.shape
    depth = w_in.shape[0]
    m = batch * seq
    assert seq % CHUNK == 0 and w_in.shape[2] == U_COLS + GATE_RANK
    assert UC0 == GB0 + B_WIDTH

    w_in_t = jnp.swapaxes(w_in, 1, 2)
    w_down_b = w_down.astype(BF16)
    w_pool_b = w_pool.astype(BF16)

    h = x.reshape(m, d)
    branch = None
    for l in range(depth):
        w_gate = jnp.pad(w_gate_lr2[l], ((0, LANES - GATE_RANK), (0, 0))).astype(BF16)

        if l == 0:
            hn = _norm_first(h, g_pre_mix[0])
        else:
            h, hn = _resnorm(h, branch, g_post_ffn[l - 1], g_pre_mix[l])
        u = _mm_in(hn, w_in_t, l)
        z, uc = _mm_tail(hn, w_in_t, l)
        lam_init = 0.8 - 0.6 * math.exp(-0.3 * l)
        oa = _att(u, lam_q1[l], lam_k1[l], lam_q2[l], lam_k2[l], g_subln[l], lam_init, batch, seq)
        ob = _gla(u, z, w_gate, b_gate[l], g_gla[l], batch, seq)
        oc = _pool(uc, w_pool_b[l], pool_scale[l], batch, seq)
        mix = _mm_out(oa, ob, oc, w_o, l)

        h, hn = _resnorm(h, mix, g_post_mix[l], g_pre_ffn[l])
        f_in = _mm_up(hn, w_up, w_conv, b_conv, l, seq)
        branch = _mm_down(f_in, w_down_b, l)
    h = _resnorm_last(h, branch, g_post_ffn[depth - 1])
    return h.reshape(batch, seq, d)
```

```python
import functools
import math

import jax
import jax.numpy as jnp
from jax import lax
from jax.experimental import pallas as pl
from jax.experimental.pallas import tpu as pltpu

F32 = jnp.float32
BF16 = jnp.bfloat16
BRANCH_DTYPE = jnp.bfloat16

CHUNK = 64
EPS = 1e-6
A_HEADS = 8
A_DH = 128
A_WIDTH = A_HEADS * 2 * A_DH
B_HEADS = 4
B_DK = 128
B_DV = 256
B_WIDTH = B_HEADS * B_DV
GATE_RANK = 16
GATE_TAU = 16.0
POOL_WINDOWS = (2, 4, 8, 16)
N_POOL = len(POOL_WINDOWS)
POOL_CH = 256
C_WIDTH = N_POOL * POOL_CH
CONV_K = 3

QA0 = 0
KA0 = A_WIDTH
VA0 = 2 * A_WIDTH
QB0 = 3 * A_WIDTH
KB0 = QB0 + B_HEADS * B_DK
VB0 = KB0 + B_HEADS * B_DK
GB0 = VB0 + B_WIDTH
UC0 = GB0 + B_WIDTH
U_COLS = UC0 + C_WIDTH

LANES = 128
MXU_COLS = 256
VMEM_LIMIT = 56 * 1024 * 1024
POOL_HALO = 16
NEG_BIG = -1e30
LOG2E = math.log2(math.e)
ATT_TQ = 256
ATT_GROUP = 1024
UP_TM = 2048
UP_ROWS = 256
GLA_SUB = 16
TAIL_ROWS = 1152


def _cparams(*sem):
    return pltpu.CompilerParams(dimension_semantics=sem, vmem_limit_bytes=VMEM_LIMIT)


def _dot(a, b):
    return jnp.dot(a, b, preferred_element_type=F32)


def _dot_nt(a, b):
    return lax.dot_general(a, b, (((1,), (1,)), ((), ())), preferred_element_type=F32)


def _dot_tn(a, b):
    return lax.dot_general(a, b, (((0,), (0,)), ((), ())), preferred_element_type=F32)


def _rms(x, g):
    ms = jnp.mean(x * x, axis=-1, keepdims=True)
    return x * lax.rsqrt(ms + EPS) * g


def _tile(n, want):
    t = min(n, want)
    assert n % t == 0, (n, want)
    return t


def _norm_first_kernel(h_ref, gpre_ref, hn_ref):
    hn_ref[...] = _rms(h_ref[...], gpre_ref[...]).astype(BF16)


def _resnorm_kernel(h_ref, br_ref, gpost_ref, gpre_ref, hout_ref, hn_ref):
    h = h_ref[...] + _rms(br_ref[...].astype(F32), gpost_ref[...])
    hout_ref[...] = h
    hn_ref[...] = _rms(h, gpre_ref[...]).astype(BF16)


def _resnorm_last_kernel(h_ref, br_ref, gpost_ref, hout_ref):
    hout_ref[...] = h_ref[...] + _rms(br_ref[...].astype(F32), gpost_ref[...])


def _norm_first(h, gpre):
    m, d = h.shape
    tr = _tile(m, 256)
    row = pl.BlockSpec((tr, d), lambda i: (i, 0))
    vec = pl.BlockSpec((1, d), lambda i: (0, 0))
    return pl.pallas_call(
        _norm_first_kernel, grid=(m // tr,), in_specs=[row, vec], out_specs=row,
        out_shape=jax.ShapeDtypeStruct((m, d), BF16),
        compiler_params=_cparams("parallel"), name="norm_first",
    )(h, gpre.reshape(1, d))


def _resnorm(h, br, gpost, gpre):
    m, d = h.shape
    tr = _tile(m, 256)
    row = pl.BlockSpec((tr, d), lambda i: (i, 0))
    vec = pl.BlockSpec((1, d), lambda i: (0, 0))
    return pl.pallas_call(
        _resnorm_kernel, grid=(m // tr,), in_specs=[row, row, vec, vec], out_specs=[row, row],
        out_shape=[jax.ShapeDtypeStruct((m, d), F32), jax.ShapeDtypeStruct((m, d), BF16)],
        compiler_params=_cparams("parallel"), name="resnorm",
    )(h, br, gpost.reshape(1, d), gpre.reshape(1, d))


def _resnorm_last(h, br, gpost):
    m, d = h.shape
    tr = _tile(m, 256)
    row = pl.BlockSpec((tr, d), lambda i: (i, 0))
    vec = pl.BlockSpec((1, d), lambda i: (0, 0))
    return pl.pallas_call(
        _resnorm_last_kernel, grid=(m // tr,), in_specs=[row, row, vec], out_specs=row,
        out_shape=jax.ShapeDtypeStruct((m, d), F32),
        compiler_params=_cparams("parallel"), name="resnorm_last",
    )(h, br, gpost.reshape(1, d))


def _mm_in_kernel(x_ref, w_ref, u_ref):
    u_ref[...] = _dot_nt(x_ref[...], w_ref[...].astype(BF16)).astype(BF16)


def _mm_in(hn, w_in_t, layer):
    m, d = hn.shape
    tm = _tile(m, 1024)
    tn = 512
    assert UC0 % tn == 0
    return pl.pallas_call(
        _mm_in_kernel, grid=(m // tm, UC0 // tn),
        in_specs=[pl.BlockSpec((tm, d), lambda i, j: (i, 0)),
                  pl.BlockSpec((None, tn, d), lambda i, j: (layer, j, 0))],
        out_specs=pl.BlockSpec((tm, tn), lambda i, j: (i, j)),
        out_shape=jax.ShapeDtypeStruct((m, UC0), BF16),
        compiler_params=_cparams("parallel", "arbitrary"), name="mm_in",
    )(hn, w_in_t)


def _mm_tail_kernel(x_ref, w_ref, z_ref, uc_ref):
    x = x_ref[...]
    z_ref[...] = _dot_nt(x, w_ref[0:LANES, :].astype(BF16))
    uc_ref[...] = _dot_nt(x, w_ref[GATE_RANK:GATE_RANK + C_WIDTH, :].astype(BF16)).astype(BF16)


def _mm_tail(hn, w_in_t, layer):
    m, d = hn.shape
    tm = _tile(m, 1024)
    rows = TAIL_ROWS
    assert UC0 % rows == 0 and rows >= GATE_RANK + C_WIDTH and GATE_RANK % 8 == 0
    return pl.pallas_call(
        _mm_tail_kernel, grid=(m // tm,),
        in_specs=[pl.BlockSpec((tm, d), lambda i: (i, 0)),
                  pl.BlockSpec((None, rows, d), lambda i: (layer, UC0 // rows, 0),
                               pipeline_mode=pl.Buffered(1))],
        out_specs=[pl.BlockSpec((tm, LANES), lambda i: (i, 0)),
                   pl.BlockSpec((tm, C_WIDTH), lambda i: (i, 0))],
        out_shape=[jax.ShapeDtypeStruct((m, LANES), F32), jax.ShapeDtypeStruct((m, C_WIDTH), BF16)],
        compiler_params=_cparams("arbitrary"), name="mm_tail",
    )(hn, w_in_t)


def _att_kernel(lq1_ref, lk1_ref, lq2_ref, lk2_ref, g_ref, q_ref, k_ref, v_ref, o_ref,
                s1_ref, s2_ref, acc1_ref, acc2_ref, bias_ref, *, lam_init, t, G):
    head = pl.program_id(1)
    qi = pl.program_id(2)
    qscale = LOG2E / math.sqrt(A_DH)
    slope = lax.bitcast_convert_type(
        jnp.full((1, 1), (126 - head) << 23, jnp.int32), F32) * LOG2E

    q = q_ref[...].astype(F32) * qscale
    qt1 = q[:, :A_DH].T.astype(BF16)
    qt2 = q[:, A_DH:].T.astype(BF16)

    def fold8(x, op):
        return op(x.reshape(G // 8, 8, t), axis=0)

    @pl.when(qi == 0)
    def _():
        kk = lax.broadcasted_iota(jnp.int32, (G, t), 0)
        bias_ref[...] = slope * kk.astype(F32)

    qs = qi * t
    gl = qs // G

    def offset(g):
        return slope * (g * G - qs).astype(F32)

    def score_body(g, carry):
        m1, m2 = carry
        k0 = pl.multiple_of(g * G, G)
        kt = k_ref[pl.ds(k0, G), :]
        s1 = _dot(kt[:, :A_DH], qt1) + bias_ref[...]
        s2 = _dot(kt[:, A_DH:], qt2) + bias_ref[...]
        s1_ref[pl.ds(k0, G), :] = s1
        s2_ref[pl.ds(k0, G), :] = s2
        return (jnp.maximum(m1, fold8(s1, jnp.max) + offset(g)),
                jnp.maximum(m2, fold8(s2, jnp.max) + offset(g)))

    l0 = pl.multiple_of(gl * G, G)
    jd = (qs - l0) // t
    kl = k_ref[pl.ds(l0, G), :]
    sl1 = _dot(kl[:, :A_DH], qt1) + bias_ref[...]
    sl2 = _dot(kl[:, A_DH:], qt2) + bias_ref[...]
    s1_ref[pl.ds(l0, G), :] = sl1
    s2_ref[pl.ds(l0, G), :] = sl2

    def fold8t(x, op):
        return op(x.reshape(t // 8, 8, t), axis=0)

    m1 = jnp.full((8, t), NEG_BIG, F32)
    m2 = m1
    for j in range(G // t - 1):
        skip = jnp.where(j < jd, 0.0, NEG_BIG)
        m1 = jnp.maximum(m1, fold8t(sl1[j * t:(j + 1) * t], jnp.max) + skip)
        m2 = jnp.maximum(m2, fold8t(sl2[j * t:(j + 1) * t], jnp.max) + skip)
    cshift = CHUNK.bit_length() - 1
    kd = lax.broadcasted_iota(jnp.int32, (t, t), 0)
    rd = lax.broadcasted_iota(jnp.int32, (t, t), 1)
    allowed = (kd >> cshift) <= (rd >> cshift)
    corr = slope * (rd - jnp.abs(rd - kd) - kd).astype(F32)
    d0 = pl.multiple_of(qs, t)
    sd1 = jnp.where(allowed, s1_ref[pl.ds(d0, t), :] + corr, NEG_BIG)
    sd2 = jnp.where(allowed, s2_ref[pl.ds(d0, t), :] + corr, NEG_BIG)
    s1_ref[pl.ds(d0, t), :] = sd1
    s2_ref[pl.ds(d0, t), :] = sd2
    for j in range(1, G // t):
        @pl.when(j > jd)
        def _():
            j0 = pl.multiple_of(l0 + j * t, t)
            s1_ref[pl.ds(j0, t), :] = jnp.full((t, t), NEG_BIG, F32)
            s2_ref[pl.ds(j0, t), :] = jnp.full((t, t), NEG_BIG, F32)
    m1 = jnp.maximum(m1, fold8t(sd1, jnp.max)) + offset(gl)
    m2 = jnp.maximum(m2, fold8t(sd2, jnp.max)) + offset(gl)
    m1, m2 = lax.fori_loop(0, gl, score_body, (m1, m2))
    m1 = jnp.max(m1, axis=0, keepdims=True)
    m2 = jnp.max(m2, axis=0, keepdims=True)

    acc1_ref[...] = jnp.zeros_like(acc1_ref)
    acc2_ref[...] = jnp.zeros_like(acc2_ref)

    def pv_body(g, carry):
        l1, l2 = carry
        k0 = pl.multiple_of(g * G, G)
        vt = v_ref[pl.ds(k0, G), :]
        p1 = jnp.exp2(s1_ref[pl.ds(k0, G), :] - (m1 - offset(g)))
        p2 = jnp.exp2(s2_ref[pl.ds(k0, G), :] - (m2 - offset(g)))
        acc1_ref[...] += _dot_tn(vt, p1.astype(BF16))
        acc2_ref[...] += _dot_tn(vt, p2.astype(BF16))
        return l1 + fold8(p1, jnp.sum), l2 + fold8(p2, jnp.sum)

    zero8 = jnp.zeros((8, t), F32)
    l1, l2 = lax.fori_loop(0, gl + 1, pv_body, (zero8, zero8))
    l1 = jnp.sum(l1, axis=0, keepdims=True)
    l2 = jnp.sum(l2, axis=0, keepdims=True)

    lam = (jnp.exp(jnp.sum(lq1_ref[...] * lk1_ref[...], axis=-1, keepdims=True))
           - jnp.exp(jnp.sum(lq2_ref[...] * lk2_ref[...], axis=-1, keepdims=True))
           + lam_init)
    ot = acc1_ref[...] / l1 - lam * (acc2_ref[...] / l2)
    ms = jnp.mean(ot * ot, axis=0, keepdims=True)
    o = (ot * lax.rsqrt(ms + EPS)).T
    o_ref[...] = (o * g_ref[...] * (1.0 - lam_init)).astype(BF16)


def _att(u, lq1, lk1, lq2, lk2, g_subln, lam_init, batch, seq):
    m = u.shape[0]
    t = _tile(seq, ATT_TQ)
    G = _tile(seq, ATT_GROUP)
    nq = seq // t
    e = 2 * A_DH
    assert G % t == 0
    vec = pl.BlockSpec((1, A_DH), lambda b, h, i: (0, 0))
    return pl.pallas_call(
        functools.partial(_att_kernel, lam_init=lam_init, t=t, G=G),
        grid=(batch, A_HEADS, nq),
        in_specs=[vec, vec, vec, vec,
                  pl.BlockSpec((1, e), lambda b, h, i: (0, 0)),
                  pl.BlockSpec((t, e), lambda b, h, i: (b * nq + i, QA0 // e + h)),
                  pl.BlockSpec((seq, e), lambda b, h, i: (b, KA0 // e + h)),
                  pl.BlockSpec((seq, e), lambda b, h, i: (b, VA0 // e + h))],
        out_specs=pl.BlockSpec((t, e), lambda b, h, i: (b * nq + i, h)),
        out_shape=jax.ShapeDtypeStruct((m, A_WIDTH), BF16),
        scratch_shapes=[pltpu.VMEM((seq, t), F32), pltpu.VMEM((seq, t), F32),
                        pltpu.VMEM((e, t), F32), pltpu.VMEM((e, t), F32),
                        pltpu.VMEM((G, t), F32)],
        compiler_params=_cparams("arbitrary", "arbitrary", "arbitrary"), name="diff_attn",
    )(lq1.reshape(1, A_DH), lk1.reshape(1, A_DH), lq2.reshape(1, A_DH), lk2.reshape(1, A_DH),
      g_subln.reshape(1, e), u, u, u)


def _gla_kernel(z_ref, wg_ref, bg_ref, gn_ref, q_ref, k_ref, v_ref, gb_ref, o_ref, *, seq):
    L = CHUNK
    wg = wg_ref[...]
    bg = bg_ref[...]
    gn = gn_ref[...]
    rows = lax.broadcasted_iota(jnp.int32, (L, B_DK), 0)
    ti = lax.broadcasted_iota(jnp.int32, (L, L), 0)
    si = lax.broadcasted_iota(jnp.int32, (L, L), 1)
    SUB = GLA_SUB
    assert L == 4 * SUB
    sshift = SUB.bit_length() - 1
    bi = ti >> sshift
    bj = si >> sshift
    sd = si - (ti - (ti & (SUB - 1)))
    m_diag = (bi == bj) & (si <= ti)
    m_near = (bi == bj + 1) & ((bi & 1) == 1)
    m_far = (bi >= 2) & (bj <= 1)
    qscale = B_DK ** -0.5

    def chunk(c, state_t):
        r0 = pl.multiple_of(c * L, L)
        x = _dot(z_ref[pl.ds(r0, L), :].astype(BF16), wg) + bg
        la = (jnp.minimum(x, 0.0) - jnp.log(1.0 + jnp.exp(-jnp.abs(x)))) * (1.0 / GATE_TAU)
        b = la
        sh = 1
        while sh < L:
            b = b + jnp.where(rows >= sh, pltpu.roll(b, sh, axis=0), 0.0)
            sh *= 2
        b = b * LOG2E
        b_last = b[L - 1:L, :]
        qf = q_ref[pl.ds(r0, L), :].astype(F32) * qscale
        kc = k_ref[pl.ds(r0, L), :]
        kf = kc.astype(F32)
        vc = v_ref[pl.ds(r0, L), :]

        def rescaled(ref):
            qr = (qf * jnp.exp2(jnp.minimum(b - ref, 0.0))).astype(BF16)
            kr = (kf * jnp.exp2(jnp.minimum(ref - b, 0.0))).astype(BF16)
            return _dot_nt(qr, kr)

        half = L // 2
        a_far = rescaled(b[half - 1:half, :])
        ref_near = jnp.where(rows < half, b[SUB - 1:SUB, :], b[half + SUB - 1:half + SUB, :])
        a_near = rescaled(ref_near)
        a = jnp.zeros((L, L), F32)
        for s in range(SUB):
            ref = jnp.concatenate(
                [jnp.broadcast_to(b[i * SUB + s:i * SUB + s + 1, :], (SUB, B_DK)) for i in range(L // SUB)],
                axis=0)
            e = jnp.exp2(jnp.minimum(b - ref, 0.0))
            rs = _dot_nt((qf * e).astype(BF16), kc)
            a = jnp.where(sd == s, rs, a)
        a = jnp.where(m_far, a_far, jnp.where(m_near, a_near, jnp.where(m_diag, a, 0.0)))

        qe = (qf * jnp.exp2(b)).astype(BF16)
        o = _dot(a.astype(BF16), vc) + _dot_nt(qe, state_t.astype(BF16))
        ke = (kf * jnp.exp2(b_last - b)).astype(BF16)
        new_state = jnp.exp2(b_last) * state_t + _dot_tn(vc, ke)

        gate = gb_ref[pl.ds(r0, L), :].astype(F32)
        y = _rms(o, gn) * (gate / (1.0 + jnp.exp(-gate)))
        o_ref[pl.ds(r0, L), :] = y.astype(BF16)
        return new_state

    nc = seq // L
    unroll = 4 if nc % 4 == 0 else 1

    def chunks(c, state_t):
        for k in range(unroll):
            state_t = chunk(c * unroll + k, state_t)
        return state_t

    lax.fori_loop(0, nc // unroll, chunks, jnp.zeros((B_DV, B_DK), F32))


def _gla(u, z, w_gate, b_gate, g_gla, batch, seq):
    m = u.shape[0]
    return pl.pallas_call(
        functools.partial(_gla_kernel, seq=seq),
        grid=(batch, B_HEADS),
        in_specs=[pl.BlockSpec((seq, LANES), lambda b, h: (b, 0)),
                  pl.BlockSpec((LANES, B_DK), lambda b, h: (0, h)),
                  pl.BlockSpec((1, B_DK), lambda b, h: (0, h)),
                  pl.BlockSpec((1, B_DV), lambda b, h: (0, h)),
                  pl.BlockSpec((seq, B_DK), lambda b, h: (b, QB0 // B_DK + h)),
                  pl.BlockSpec((seq, B_DK), lambda b, h: (b, KB0 // B_DK + h)),
                  pl.BlockSpec((seq, B_DV), lambda b, h: (b, VB0 // B_DV + h)),
                  pl.BlockSpec((seq, B_DV), lambda b, h: (b, GB0 // B_DV + h))],
        out_specs=pl.BlockSpec((seq, B_DV), lambda b, h: (b, h)),
        out_shape=jax.ShapeDtypeStruct((m, B_WIDTH), BF16),
        compiler_params=_cparams("parallel", "parallel"), name="gla",
    )(z, w_gate, b_gate.reshape(1, -1), g_gla.reshape(1, -1), u, u, u, u)


def _pool_kernel(u_ref, w_ref, sc_ref, o_ref, carry_ref, *, tr):
    ti = pl.program_id(1)

    @pl.when(ti == 0)
    def _():
        carry_ref[...] = jnp.zeros_like(carry_ref)

    x = u_ref[...].astype(F32)
    ext = jnp.concatenate([carry_ref[...], x], axis=0)
    carry_ref[...] = x[tr - POOL_HALO:, :]
    pos = ti * tr + lax.broadcasted_iota(jnp.int32, (tr, 1), 0)
    for g, w in enumerate(POOL_WINDOWS):
        lo, hi = g * POOL_CH, (g + 1) * POOL_CH
        s = ext[:, lo:hi]
        sh = 1
        while sh < w:
            s = s + pltpu.roll(s, sh, axis=0)
            sh *= 2
        cnt = jnp.minimum(pos + 1, w).astype(F32)
        d = s[POOL_HALO:, :] / cnt - x[:, lo:hi]
        y = _dot(d.astype(BF16), w_ref[g]) * sc_ref[:, lo:hi]
        o_ref[:, lo:hi] = y.astype(BF16)


def _pool(uc, w_pool, pool_scale, batch, seq):
    m = uc.shape[0]
    tr = _tile(seq, 512)
    nt = seq // tr
    assert tr >= POOL_HALO and max(POOL_WINDOWS) <= POOL_HALO
    return pl.pallas_call(
        functools.partial(_pool_kernel, tr=tr),
        grid=(batch, nt),
        in_specs=[pl.BlockSpec((tr, C_WIDTH), lambda b, i: (b * nt + i, 0)),
                  pl.BlockSpec((N_POOL, POOL_CH, POOL_CH), lambda b, i: (0, 0, 0)),
                  pl.BlockSpec((1, C_WIDTH), lambda b, i: (0, 0))],
        out_specs=pl.BlockSpec((tr, C_WIDTH), lambda b, i: (b * nt + i, 0)),
        out_shape=jax.ShapeDtypeStruct((m, C_WIDTH), BF16),
        scratch_shapes=[pltpu.VMEM((POOL_HALO, C_WIDTH), F32)],
        compiler_params=_cparams("parallel", "arbitrary"), name="pool",
    )(uc, w_pool, pool_scale.reshape(1, C_WIDTH))


def _mm_out_kernel(a_ref, b_ref, c_ref, wa_ref, wb_ref, wc_ref, o_ref):
    o_ref[...] = (_dot(a_ref[...], wa_ref[...].astype(BF16))
                  + _dot(b_ref[...], wb_ref[...].astype(BF16))
                  + _dot(c_ref[...], wc_ref[...].astype(BF16))).astype(o_ref.dtype)


def _mm_out(oa, ob, oc, w_o, layer):
    m = oa.shape[0]
    n = w_o.shape[2]
    tm = _tile(m, 1024)
    tn = _tile(n, 512)
    assert A_WIDTH % B_WIDTH == 0 and B_WIDTH == C_WIDTH
    return pl.pallas_call(
        _mm_out_kernel, grid=(m // tm, n // tn),
        in_specs=[pl.BlockSpec((tm, A_WIDTH), lambda i, j: (i, 0)),
                  pl.BlockSpec((tm, B_WIDTH), lambda i, j: (i, 0)),
                  pl.BlockSpec((tm, C_WIDTH), lambda i, j: (i, 0)),
                  pl.BlockSpec((None, A_WIDTH, tn), lambda i, j: (layer, 0, j)),
                  pl.BlockSpec((None, B_WIDTH, tn), lambda i, j: (layer, A_WIDTH // B_WIDTH, j)),
                  pl.BlockSpec((None, C_WIDTH, tn), lambda i, j: (layer, A_WIDTH // C_WIDTH + 1, j))],
        out_specs=pl.BlockSpec((tm, tn), lambda i, j: (i, j)),
        out_shape=jax.ShapeDtypeStruct((m, n), BRANCH_DTYPE),
        compiler_params=_cparams("parallel", "parallel"), name="mm_out",
    )(oa, ob, oc, w_o, w_o, w_o)


GELU_C = math.sqrt(2.0 / math.pi)


def _mm_up_kernel(x_ref, wg_ref, wv_ref, cg_ref, cv_ref, bg_ref, bv_ref, o_ref,
                  tail_ref, yg_ref, yv_ref, *, tm, nj, tiles_per_seq):
    i = pl.program_id(0)
    j = pl.program_id(1)
    seq_start = (i % tiles_per_seq) == 0
    pad = 8

    @pl.when(seq_start)
    def _():
        tail_ref[j] = jnp.zeros((pad, tail_ref.shape[2]), F32)
        tail_ref[nj + j] = jnp.zeros((pad, tail_ref.shape[2]), F32)

    rs = min(tm, UP_ROWS)
    npieces = tm // rs
    wg = wg_ref[...].astype(BF16)
    wv = wv_ref[...].astype(BF16)
    yg_ref[0:pad, :] = tail_ref[j]
    yv_ref[0:pad, :] = tail_ref[nj + j]
    cg = cg_ref[...]
    cv = cv_ref[...] * 0.5
    bg = bg_ref[...]
    bv = bv_ref[...] * 0.5

    def matmuls(r, zero):
        r0 = r * rs
        x = x_ref[r0:r0 + rs, :]
        if zero is not None:
            xb = pltpu.bitcast(x, jnp.uint32)
            x = pltpu.bitcast(xb | jnp.tile(zero, (xb.shape[0] // 8, xb.shape[1] // LANES)), BF16)
        yg_ref[pad + r0:pad + r0 + rs, :] = _dot(x, wg)
        yv_ref[pad + r0:pad + r0 + rs, :] = _dot(x, wv)

    def conv(y_ref, cw, b, r0):
        return (b + cw[2:3, :] * y_ref[pad + r0:pad + r0 + rs, :]
                + cw[1:2, :] * y_ref[pl.ds(pad + r0 - 1, rs), :]
                + cw[0:1, :] * y_ref[pl.ds(pad + r0 - 2, rs), :])

    def epilogue(r):
        r0 = r * rs
        zg = conv(yg_ref, cg, bg, r0)
        zv = conv(yv_ref, cv, bv, r0)
        t = jnp.tanh(zg * (GELU_C + (GELU_C * 0.044715) * (zg * zg)))
        out = ((zg + zg * t) * zv).astype(BF16)
        o_ref[r0:r0 + rs, :] = out
        bits = pltpu.bitcast(out, jnp.uint32)
        acc = bits[0:8, :]
        for k in range(1, bits.shape[0] // 8):
            acc = acc | bits[8 * k:8 * k + 8, :]
        red = acc[:, 0:LANES]
        for k in range(1, acc.shape[1] // LANES):
            red = red | acc[:, k * LANES:(k + 1) * LANES]
        return (red >> 16) >> 16

    zeros = {}
    matmuls(0, None)
    if npieces > 1:
        matmuls(1, None)
    for r in range(npieces):
        zeros[r] = epilogue(r)
        if r + 2 < npieces:
            matmuls(r + 2, zeros[r])
    tail_ref[j] = yg_ref[tm:tm + pad, :]
    tail_ref[nj + j] = yv_ref[tm:tm + pad, :]


def _mm_up(hn, w_up, w_conv, b_conv, layer, seq):
    m, d = hn.shape
    f = w_up.shape[2] // 2
    tm = _tile(seq, UP_TM)
    tn = MXU_COLS if f % MXU_COLS == 0 else f
    nj = f // tn
    b3 = b_conv.reshape(b_conv.shape[0], 1, 2 * f)
    return pl.pallas_call(
        functools.partial(_mm_up_kernel, tm=tm, nj=nj, tiles_per_seq=seq // tm),
        grid=(m // tm, nj),
        in_specs=[pl.BlockSpec((tm, d), lambda i, j: (i, 0), pipeline_mode=pl.Buffered(1)),
                  pl.BlockSpec((None, d, tn), lambda i, j: (layer, 0, j)),
                  pl.BlockSpec((None, d, tn), lambda i, j: (layer, 0, nj + j)),
                  pl.BlockSpec((None, CONV_K, tn), lambda i, j: (layer, 0, j)),
                  pl.BlockSpec((None, CONV_K, tn), lambda i, j: (layer, 0, nj + j)),
                  pl.BlockSpec((None, 1, tn), lambda i, j: (layer, 0, j)),
                  pl.BlockSpec((None, 1, tn), lambda i, j: (layer, 0, nj + j))],
        out_specs=pl.BlockSpec((tm, tn), lambda i, j: (i, j)),
        out_shape=jax.ShapeDtypeStruct((m, f), BF16),
        scratch_shapes=[pltpu.VMEM((2 * nj, 8, tn), F32),
                        pltpu.VMEM((tm + 8, tn), F32), pltpu.VMEM((tm + 8, tn), F32)],
        compiler_params=_cparams("arbitrary", "arbitrary"), name="mm_up",
    )(hn, w_up, w_up, w_conv, w_conv, b3, b3)


def _mm_kernel(x_ref, w_ref, o_ref):
    o_ref[...] = _dot(x_ref[...], w_ref[...]).astype(o_ref.dtype)


def _mm_down(x, w, layer):
    m, k = x.shape
    n = w.shape[2]
    tm = _tile(m, 1024)
    tn = _tile(n, MXU_COLS)
    return pl.pallas_call(
        _mm_kernel, grid=(m // tm, n // tn),
        in_specs=[pl.BlockSpec((tm, k), lambda i, j: (i, 0), pipeline_mode=pl.Buffered(1)),
                  pl.BlockSpec((None, k, tn), lambda i, j: (layer, 0, j))],
        out_specs=pl.BlockSpec((tm, tn), lambda i, j: (i, j)),
        out_shape=jax.ShapeDtypeStruct((m, n), BRANCH_DTYPE),
        compiler_params=_cparams("parallel", "parallel"), name="mm_down",
    )(x, w)


def kernel(x, w_in, w_gate_lr2, b_gate, lam_q1, lam_k1, lam_q2, lam_k2, g_subln, g_gla, w_pool, pool_scale, w_o, w_up, w_conv, b_conv, w_down, g_pre_mix, g_post_mix, g_pre_ffn, g_post_ffn):
    batch, seq, d = x.shape
    depth = w_in.shape[0]
    m = batch * seq
    assert seq % CHUNK == 0 and w_in.shape[2] == U_COLS + GATE_RANK
    assert UC0 == GB0 + B_WIDTH

    w_in_t = jnp.swapaxes(w_in, 1, 2)
    w_down_b = w_down.astype(BF16)
    w_pool_b = w_pool.astype(BF16)

    h = x.reshape(m, d)
    branch = None
    for l in range(depth):
        w_gate = jnp.pad(w_gate_lr2[l], ((0, LANES - GATE_RANK), (0, 0))).astype(BF16)

        if l == 0:
            hn = _norm_first(h, g_pre_mix[0])
        else:
            h, hn = _resnorm(h, branch, g_post_ffn[l - 1], g_pre_mix[l])
        u = _mm_in(hn, w_in_t, l)
        z, uc = _mm_tail(hn, w_in_t, l)
        lam_init = 0.8 - 0.6 * math.exp(-0.3 * l)
        oa = _att(u, lam_q1[l], lam_k1[l], lam_q2[l], lam_k2[l], g_subln[l], lam_init, batch, seq)
        ob = _gla(u, z, w_gate, b_gate[l], g_gla[l], batch, seq)
        oc = _pool(uc, w_pool_b[l], pool_scale[l], batch, seq)
        mix = _mm_out(oa, ob, oc, w_o, l)

        h, hn = _resnorm(h, mix, g_post_mix[l], g_pre_ffn[l])
        f_in = _mm_up(hn, w_up, w_conv, b_conv, l, seq)
        branch = _mm_down(f_in, w_down_b, l)
    h = _resnorm_last(h, branch, g_post_ffn[depth - 1])
    return h.reshape(batch, seq, d)
```

```python
import functools
import math

import jax
import jax.numpy as jnp
from jax import lax
from jax.experimental import pallas as pl
from jax.experimental.pallas import tpu as pltpu

F32 = jnp.float32
BF16 = jnp.bfloat16
BRANCH_DTYPE = jnp.bfloat16

CHUNK = 64
EPS = 1e-6
A_HEADS = 8
A_DH = 128
A_WIDTH = A_HEADS * 2 * A_DH
B_HEADS = 4
B_DK = 128
B_DV = 256
B_WIDTH = B_HEADS * B_DV
GATE_RANK = 16
GATE_TAU = 16.0
POOL_WINDOWS = (2, 4, 8, 16)
N_POOL = len(POOL_WINDOWS)
POOL_CH = 256
C_WIDTH = N_POOL * POOL_CH
CONV_K = 3

QA0 = 0
KA0 = A_WIDTH
VA0 = 2 * A_WIDTH
QB0 = 3 * A_WIDTH
KB0 = QB0 + B_HEADS * B_DK
VB0 = KB0 + B_HEADS * B_DK
GB0 = VB0 + B_WIDTH
UC0 = GB0 + B_WIDTH
U_COLS = UC0 + C_WIDTH

LANES = 128
MXU_COLS = 256
VMEM_LIMIT = 56 * 1024 * 1024
POOL_HALO = 16
NEG_BIG = -1e30
LOG2E = math.log2(math.e)
ATT_TQ = 512
ATT_GROUP = 1024
UP_TM = 2048
UP_ROWS = 256
GLA_SUB = 16
TAIL_ROWS = 1152


def _cparams(*sem):
    return pltpu.CompilerParams(dimension_semantics=sem, vmem_limit_bytes=VMEM_LIMIT)


def _dot(a, b):
    return jnp.dot(a, b, preferred_element_type=F32)


def _dot_nt(a, b):
    return lax.dot_general(a, b, (((1,), (1,)), ((), ())), preferred_element_type=F32)


def _dot_tn(a, b):
    return lax.dot_general(a, b, (((0,), (0,)), ((), ())), preferred_element_type=F32)


def _rms(x, g):
    ms = jnp.mean(x * x, axis=-1, keepdims=True)
    return x * lax.rsqrt(ms + EPS) * g


def _tile(n, want):
    t = min(n, want)
    assert n % t == 0, (n, want)
    return t


def _norm_first_kernel(h_ref, gpre_ref, hn_ref):
    hn_ref[...] = _rms(h_ref[...], gpre_ref[...]).astype(BF16)


def _resnorm_kernel(h_ref, br_ref, gpost_ref, gpre_ref, hout_ref, hn_ref):
    h = h_ref[...] + _rms(br_ref[...].astype(F32), gpost_ref[...])
    hout_ref[...] = h
    hn_ref[...] = _rms(h, gpre_ref[...]).astype(BF16)


def _resnorm_last_kernel(h_ref, br_ref, gpost_ref, hout_ref):
    hout_ref[...] = h_ref[...] + _rms(br_ref[...].astype(F32), gpost_ref[...])


def _norm_first(h, gpre):
    m, d = h.shape
    tr = _tile(m, 256)
    row = pl.BlockSpec((tr, d), lambda i: (i, 0))
    vec = pl.BlockSpec((1, d), lambda i: (0, 0))
    return pl.pallas_call(
        _norm_first_kernel, grid=(m // tr,), in_specs=[row, vec], out_specs=row,
        out_shape=jax.ShapeDtypeStruct((m, d), BF16),
        compiler_params=_cparams("parallel"), name="norm_first",
    )(h, gpre.reshape(1, d))


def _resnorm(h, br, gpost, gpre):
    m, d = h.shape
    tr = _tile(m, 256)
    row = pl.BlockSpec((tr, d), lambda i: (i, 0))
    vec = pl.BlockSpec((1, d), lambda i: (0, 0))
    return pl.pallas_call(
        _resnorm_kernel, grid=(m // tr,), in_specs=[row, row, vec, vec], out_specs=[row, row],
        out_shape=[jax.ShapeDtypeStruct((m, d), F32), jax.ShapeDtypeStruct((m, d), BF16)],
        compiler_params=_cparams("parallel"), name="resnorm",
    )(h, br, gpost.reshape(1, d), gpre.reshape(1, d))


def _resnorm_last(h, br, gpost):
    m, d = h.shape
    tr = _tile(m, 256)
    row = pl.BlockSpec((tr, d), lambda i: (i, 0))
    vec = pl.BlockSpec((1, d), lambda i: (0, 0))
    return pl.pallas_call(
        _resnorm_last_kernel, grid=(m // tr,), in_specs=[row, row, vec], out_specs=row,
        out_shape=jax.ShapeDtypeStruct((m, d), F32),
        compiler_params=_cparams("parallel"), name="resnorm_last",
    )(h, br, gpost.reshape(1, d))


def _mm_in_kernel(x_ref, w_ref, u_ref):
    u_ref[...] = _dot_nt(x_ref[...], w_ref[...].astype(BF16)).astype(BF16)


def _mm_in(hn, w_in_t, layer):
    m, d = hn.shape
    tm = _tile(m, 1024)
    tn = 512
    assert UC0 % tn == 0
    return pl.pallas_call(
        _mm_in_kernel, grid=(m // tm, UC0 // tn),
        in_specs=[pl.BlockSpec((tm, d), lambda i, j: (i, 0)),
                  pl.BlockSpec((None, tn, d), lambda i, j: (layer, j, 0))],
        out_specs=pl.BlockSpec((tm, tn), lambda i, j: (i, j)),
        out_shape=jax.ShapeDtypeStruct((m, UC0), BF16),
        compiler_params=_cparams("parallel", "arbitrary"), name="mm_in",
    )(hn, w_in_t)


def _mm_tail_kernel(x_ref, w_ref, z_ref, uc_ref):
    x = x_ref[...]
    z_ref[...] = _dot_nt(x, w_ref[0:LANES, :].astype(BF16))
    uc_ref[...] = _dot_nt(x, w_ref[GATE_RANK:GATE_RANK + C_WIDTH, :].astype(BF16)).astype(BF16)


def _mm_tail(hn, w_in_t, layer):
    m, d = hn.shape
    tm = _tile(m, 1024)
    rows = TAIL_ROWS
    assert UC0 % rows == 0 and rows >= GATE_RANK + C_WIDTH and GATE_RANK % 8 == 0
    return pl.pallas_call(
        _mm_tail_kernel, grid=(m // tm,),
        in_specs=[pl.BlockSpec((tm, d), lambda i: (i, 0)),
                  pl.BlockSpec((None, rows, d), lambda i: (layer, UC0 // rows, 0),
                               pipeline_mode=pl.Buffered(1))],
        out_specs=[pl.BlockSpec((tm, LANES), lambda i: (i, 0)),
                   pl.BlockSpec((tm, C_WIDTH), lambda i: (i, 0))],
        out_shape=[jax.ShapeDtypeStruct((m, LANES), F32), jax.ShapeDtypeStruct((m, C_WIDTH), BF16)],
        compiler_params=_cparams("arbitrary"), name="mm_tail",
    )(hn, w_in_t)


def _att_kernel(lq1_ref, lk1_ref, lq2_ref, lk2_ref, g_ref, q_ref, k_ref, v_ref, o_ref,
                s1_ref, s2_ref, acc1_ref, acc2_ref, bias_ref, diag_ref, *, lam_init, t, G):
    head = pl.program_id(1)
    qi = pl.program_id(2)
    qscale = LOG2E / math.sqrt(A_DH)
    slope = lax.bitcast_convert_type(
        jnp.full((1, 1), (126 - head) << 23, jnp.int32), F32) * LOG2E

    q = q_ref[...].astype(F32) * qscale
    qt1 = q[:, :A_DH].T.astype(BF16)
    qt2 = q[:, A_DH:].T.astype(BF16)

    def fold8(x, op):
        return op(x.reshape(G // 8, 8, t), axis=0)

    @pl.when(qi == 0)
    def _():
        kk = lax.broadcasted_iota(jnp.int32, (G, t), 0)
        bias_ref[...] = slope * kk.astype(F32)
        cshift = CHUNK.bit_length() - 1
        kd = lax.broadcasted_iota(jnp.int32, (t, t), 0)
        rd = lax.broadcasted_iota(jnp.int32, (t, t), 1)
        corr = slope * (rd - jnp.abs(rd - kd) - kd).astype(F32)
        diag_ref[...] = jnp.where((kd >> cshift) <= (rd >> cshift), corr, NEG_BIG)

    qs = qi * t
    gl = qs // G

    def offset(g):
        return slope * (g * G - qs).astype(F32)

    def score_body(g, carry):
        m1, m2 = carry
        k0 = pl.multiple_of(g * G, G)
        kt = k_ref[pl.ds(k0, G), :]
        s1 = _dot(kt[:, :A_DH], qt1) + bias_ref[...]
        s2 = _dot(kt[:, A_DH:], qt2) + bias_ref[...]
        s1_ref[pl.ds(k0, G), :] = s1
        s2_ref[pl.ds(k0, G), :] = s2
        return (jnp.maximum(m1, fold8(s1, jnp.max) + offset(g)),
                jnp.maximum(m2, fold8(s2, jnp.max) + offset(g)))

    l0 = pl.multiple_of(gl * G, G)
    jd = (qs - l0) // t
    kl = k_ref[pl.ds(l0, G), :]
    sl1 = _dot(kl[:, :A_DH], qt1) + bias_ref[...]
    sl2 = _dot(kl[:, A_DH:], qt2) + bias_ref[...]
    s1_ref[pl.ds(l0, G), :] = sl1
    s2_ref[pl.ds(l0, G), :] = sl2

    def fold8t(x, op):
        return op(x.reshape(t // 8, 8, t), axis=0)

    m1 = jnp.full((8, t), NEG_BIG, F32)
    m2 = m1
    for j in range(G // t - 1):
        skip = jnp.where(j < jd, 0.0, NEG_BIG)
        m1 = jnp.maximum(m1, fold8t(sl1[j * t:(j + 1) * t], jnp.max) + skip)
        m2 = jnp.maximum(m2, fold8t(sl2[j * t:(j + 1) * t], jnp.max) + skip)
    d0 = pl.multiple_of(qs, t)
    sd1 = s1_ref[pl.ds(d0, t), :] + diag_ref[...]
    sd2 = s2_ref[pl.ds(d0, t), :] + diag_ref[...]
    s1_ref[pl.ds(d0, t), :] = sd1
    s2_ref[pl.ds(d0, t), :] = sd2
    for j in range(1, G // t):
        @pl.when(j > jd)
        def _():
            j0 = pl.multiple_of(l0 + j * t, t)
            s1_ref[pl.ds(j0, t), :] = jnp.full((t, t), NEG_BIG, F32)
            s2_ref[pl.ds(j0, t), :] = jnp.full((t, t), NEG_BIG, F32)
    m1 = jnp.maximum(m1, fold8t(sd1, jnp.max)) + offset(gl)
    m2 = jnp.maximum(m2, fold8t(sd2, jnp.max)) + offset(gl)
    m1, m2 = lax.fori_loop(0, gl, score_body, (m1, m2))
    m1 = jnp.max(m1, axis=0, keepdims=True)
    m2 = jnp.max(m2, axis=0, keepdims=True)

    acc1_ref[...] = jnp.zeros_like(acc1_ref)
    acc2_ref[...] = jnp.zeros_like(acc2_ref)

    def pv_body(g, carry):
        l1, l2 = carry
        k0 = pl.multiple_of(g * G, G)
        vt = v_ref[pl.ds(k0, G), :]
        p1 = jnp.exp2(s1_ref[pl.ds(k0, G), :] - (m1 - offset(g)))
        p2 = jnp.exp2(s2_ref[pl.ds(k0, G), :] - (m2 - offset(g)))
        acc1_ref[...] += _dot_tn(vt, p1.astype(BF16))
        acc2_ref[...] += _dot_tn(vt, p2.astype(BF16))
        return l1 + fold8(p1, jnp.sum), l2 + fold8(p2, jnp.sum)

    zero8 = jnp.zeros((8, t), F32)
    l1, l2 = lax.fori_loop(0, gl + 1, pv_body, (zero8, zero8))
    l1 = jnp.sum(l1, axis=0, keepdims=True)
    l2 = jnp.sum(l2, axis=0, keepdims=True)

    lam = (jnp.exp(jnp.sum(lq1_ref[...] * lk1_ref[...], axis=-1, keepdims=True))
           - jnp.exp(jnp.sum(lq2_ref[...] * lk2_ref[...], axis=-1, keepdims=True))
           + lam_init)
    ot = acc1_ref[...] / l1 - lam * (acc2_ref[...] / l2)
    ms = jnp.mean(ot * ot, axis=0, keepdims=True)
    o = (ot * lax.rsqrt(ms + EPS)).T
    o_ref[...] = (o * g_ref[...] * (1.0 - lam_init)).astype(BF16)


def _att(u, lq1, lk1, lq2, lk2, g_subln, lam_init, batch, seq):
    m = u.shape[0]
    t = _tile(seq, ATT_TQ)
    G = _tile(seq, ATT_GROUP)
    nq = seq // t
    e = 2 * A_DH
    assert G % t == 0
    vec = pl.BlockSpec((1, A_DH), lambda b, h, i: (0, 0))
    return pl.pallas_call(
        functools.partial(_att_kernel, lam_init=lam_init, t=t, G=G),
        grid=(batch, A_HEADS, nq),
        in_specs=[vec, vec, vec, vec,
                  pl.BlockSpec((1, e), lambda b, h, i: (0, 0)),
                  pl.BlockSpec((t, e), lambda b, h, i: (b * nq + i, QA0 // e + h)),
                  pl.BlockSpec((seq, e), lambda b, h, i: (b, KA0 // e + h)),
                  pl.BlockSpec((seq, e), lambda b, h, i: (b, VA0 // e + h))],
        out_specs=pl.BlockSpec((t, e), lambda b, h, i: (b * nq + i, h)),
        out_shape=jax.ShapeDtypeStruct((m, A_WIDTH), BF16),
        scratch_shapes=[pltpu.VMEM((seq, t), F32), pltpu.VMEM((seq, t), F32),
                        pltpu.VMEM((e, t), F32), pltpu.VMEM((e, t), F32),
                        pltpu.VMEM((G, t), F32), pltpu.VMEM((t, t), F32)],
        compiler_params=_cparams("arbitrary", "arbitrary", "arbitrary"), name="diff_attn",
    )(lq1.reshape(1, A_DH), lk1.reshape(1, A_DH), lq2.reshape(1, A_DH), lk2.reshape(1, A_DH),
      g_subln.reshape(1, e), u, u, u)


def _gla_kernel(z_ref, wg_ref, bg_ref, gn_ref, q_ref, k_ref, v_ref, gb_ref, o_ref, *, seq):
    L = CHUNK
    wg = wg_ref[...]
    bg = bg_ref[...]
    gn = gn_ref[...]
    rows = lax.broadcasted_iota(jnp.int32, (L, B_DK), 0)
    ti = lax.broadcasted_iota(jnp.int32, (L, L), 0)
    si = lax.broadcasted_iota(jnp.int32, (L, L), 1)
    SUB = GLA_SUB
    assert L == 4 * SUB
    sshift = SUB.bit_length() - 1
    bi = ti >> sshift
    bj = si >> sshift
    sd = si - (ti - (ti & (SUB - 1)))
    m_diag = (bi == bj) & (si <= ti)
    m_near = (bi == bj + 1) & ((bi & 1) == 1)
    m_far = (bi >= 2) & (bj <= 1)
    qscale = B_DK ** -0.5

    def chunk(c, state_t):
        r0 = pl.multiple_of(c * L, L)
        x = _dot(z_ref[pl.ds(r0, L), :].astype(BF16), wg) + bg
        la = (jnp.minimum(x, 0.0) - jnp.log(1.0 + jnp.exp(-jnp.abs(x)))) * (1.0 / GATE_TAU)
        b = la
        sh = 1
        while sh < L:
            b = b + jnp.where(rows >= sh, pltpu.roll(b, sh, axis=0), 0.0)
            sh *= 2
        b = b * LOG2E
        b_last = b[L - 1:L, :]
        qf = q_ref[pl.ds(r0, L), :].astype(F32) * qscale
        kc = k_ref[pl.ds(r0, L), :]
        kf = kc.astype(F32)
        vc = v_ref[pl.ds(r0, L), :]

        def rescaled(ref):
            qr = (qf * jnp.exp2(jnp.minimum(b - ref, 0.0))).astype(BF16)
            kr = (kf * jnp.exp2(jnp.minimum(ref - b, 0.0))).astype(BF16)
            return _dot_nt(qr, kr)

        half = L // 2
        a_far = rescaled(b[half - 1:half, :])
        ref_near = jnp.where(rows < half, b[SUB - 1:SUB, :], b[half + SUB - 1:half + SUB, :])
        a_near = rescaled(ref_near)
        a = jnp.zeros((L, L), F32)
        for s in range(SUB):
            ref = jnp.concatenate(
                [jnp.broadcast_to(b[i * SUB + s:i * SUB + s + 1, :], (SUB, B_DK)) for i in range(L // SUB)],
                axis=0)
            e = jnp.exp2(jnp.minimum(b - ref, 0.0))
            rs = _dot_nt((qf * e).astype(BF16), kc)
            a = jnp.where(sd == s, rs, a)
        a = jnp.where(m_far, a_far, jnp.where(m_near, a_near, jnp.where(m_diag, a, 0.0)))

        qe = (qf * jnp.exp2(b)).astype(BF16)
        o = _dot(a.astype(BF16), vc) + _dot_nt(qe, state_t.astype(BF16))
        ke = (kf * jnp.exp2(b_last - b)).astype(BF16)
        new_state = jnp.exp2(b_last) * state_t + _dot_tn(vc, ke)

        gate = gb_ref[pl.ds(r0, L), :].astype(F32)
        y = _rms(o, gn) * (gate / (1.0 + jnp.exp(-gate)))
        o_ref[pl.ds(r0, L), :] = y.astype(BF16)
        return new_state

    nc = seq // L
    unroll = 4 if nc % 4 == 0 else 1

    def chunks(c, state_t):
        for k in range(unroll):
            state_t = chunk(c * unroll + k, state_t)
        return state_t

    lax.fori_loop(0, nc // unroll, chunks, jnp.zeros((B_DV, B_DK), F32))


def _gla(u, z, w_gate, b_gate, g_gla, batch, seq):
    m = u.shape[0]
    return pl.pallas_call(
        functools.partial(_gla_kernel, seq=seq),
        grid=(batch, B_HEADS),
        in_specs=[pl.BlockSpec((seq, LANES), lambda b, h: (b, 0)),
                  pl.BlockSpec((LANES, B_DK), lambda b, h: (0, h)),
                  pl.BlockSpec((1, B_DK), lambda b, h: (0, h)),
                  pl.BlockSpec((1, B_DV), lambda b, h: (0, h)),
                  pl.BlockSpec((seq, B_DK), lambda b, h: (b, QB0 // B_DK + h)),
                  pl.BlockSpec((seq, B_DK), lambda b, h: (b, KB0 // B_DK + h)),
                  pl.BlockSpec((seq, B_DV), lambda b, h: (b, VB0 // B_DV + h)),
                  pl.BlockSpec((seq, B_DV), lambda b, h: (b, GB0 // B_DV + h))],
        out_specs=pl.BlockSpec((seq, B_DV), lambda b, h: (b, h)),
        out_shape=jax.ShapeDtypeStruct((m, B_WIDTH), BF16),
        compiler_params=_cparams("parallel", "parallel"), name="gla",
    )(z, w_gate, b_gate.reshape(1, -1), g_gla.reshape(1, -1), u, u, u, u)


def _pool_kernel(u_ref, w_ref, sc_ref, o_ref, carry_ref, *, tr):
    ti = pl.program_id(1)

    @pl.when(ti == 0)
    def _():
        carry_ref[...] = jnp.zeros_like(carry_ref)

    x = u_ref[...].astype(F32)
    ext = jnp.concatenate([carry_ref[...], x], axis=0)
    carry_ref[...] = x[tr - POOL_HALO:, :]
    pos = ti * tr + lax.broadcasted_iota(jnp.int32, (tr, 1), 0)
    for g, w in enumerate(POOL_WINDOWS):
        lo, hi = g * POOL_CH, (g + 1) * POOL_CH
        s = ext[:, lo:hi]
        sh = 1
        while sh < w:
            s = s + pltpu.roll(s, sh, axis=0)
            sh *= 2
        cnt = jnp.minimum(pos + 1, w).astype(F32)
        d = s[POOL_HALO:, :] / cnt - x[:, lo:hi]
        y = _dot(d.astype(BF16), w_ref[g]) * sc_ref[:, lo:hi]
        o_ref[:, lo:hi] = y.astype(BF16)


def _pool(uc, w_pool, pool_scale, batch, seq):
    m = uc.shape[0]
    tr = _tile(seq, 512)
    nt = seq // tr
    assert tr >= POOL_HALO and max(POOL_WINDOWS) <= POOL_HALO
    return pl.pallas_call(
        functools.partial(_pool_kernel, tr=tr),
        grid=(batch, nt),
        in_specs=[pl.BlockSpec((tr, C_WIDTH), lambda b, i: (b * nt + i, 0)),
                  pl.BlockSpec((N_POOL, POOL_CH, POOL_CH), lambda b, i: (0, 0, 0)),
                  pl.BlockSpec((1, C_WIDTH), lambda b, i: (0, 0))],
        out_specs=pl.BlockSpec((tr, C_WIDTH), lambda b, i: (b * nt + i, 0)),
        out_shape=jax.ShapeDtypeStruct((m, C_WIDTH), BF16),
        scratch_shapes=[pltpu.VMEM((POOL_HALO, C_WIDTH), F32)],
        compiler_params=_cparams("parallel", "arbitrary"), name="pool",
    )(uc, w_pool, pool_scale.reshape(1, C_WIDTH))


def _mm_out_kernel(a_ref, b_ref, c_ref, wa_ref, wb_ref, wc_ref, o_ref):
    o_ref[...] = (_dot(a_ref[...], wa_ref[...].astype(BF16))
                  + _dot(b_ref[...], wb_ref[...].astype(BF16))
                  + _dot(c_ref[...], wc_ref[...].astype(BF16))).astype(o_ref.dtype)


def _mm_out(oa, ob, oc, w_o, layer):
    m = oa.shape[0]
    n = w_o.shape[2]
    tm = _tile(m, 1024)
    tn = _tile(n, 512)
    assert A_WIDTH % B_WIDTH == 0 and B_WIDTH == C_WIDTH
    return pl.pallas_call(
        _mm_out_kernel, grid=(m // tm, n // tn),
        in_specs=[pl.BlockSpec((tm, A_WIDTH), lambda i, j: (i, 0)),
                  pl.BlockSpec((tm, B_WIDTH), lambda i, j: (i, 0)),
                  pl.BlockSpec((tm, C_WIDTH), lambda i, j: (i, 0)),
                  pl.BlockSpec((None, A_WIDTH, tn), lambda i, j: (layer, 0, j)),
                  pl.BlockSpec((None, B_WIDTH, tn), lambda i, j: (layer, A_WIDTH // B_WIDTH, j)),
                  pl.BlockSpec((None, C_WIDTH, tn), lambda i, j: (layer, A_WIDTH // C_WIDTH + 1, j))],
        out_specs=pl.BlockSpec((tm, tn), lambda i, j: (i, j)),
        out_shape=jax.ShapeDtypeStruct((m, n), BRANCH_DTYPE),
        compiler_params=_cparams("parallel", "parallel"), name="mm_out",
    )(oa, ob, oc, w_o, w_o, w_o)


GELU_C = math.sqrt(2.0 / math.pi)


def _mm_up_kernel(x_ref, wg_ref, wv_ref, cg_ref, cv_ref, bg_ref, bv_ref, o_ref,
                  tail_ref, yg_ref, yv_ref, *, tm, nj, tiles_per_seq):
    i = pl.program_id(0)
    j = pl.program_id(1)
    seq_start = (i % tiles_per_seq) == 0
    pad = 8

    @pl.when(seq_start)
    def _():
        tail_ref[j] = jnp.zeros((pad, tail_ref.shape[2]), F32)
        tail_ref[nj + j] = jnp.zeros((pad, tail_ref.shape[2]), F32)

    rs = min(tm, UP_ROWS)
    npieces = tm // rs
    wg = wg_ref[...].astype(BF16)
    wv = wv_ref[...].astype(BF16)
    yg_ref[0:pad, :] = tail_ref[j]
    yv_ref[0:pad, :] = tail_ref[nj + j]
    cg = cg_ref[...]
    cv = cv_ref[...] * 0.5
    bg = bg_ref[...]
    bv = bv_ref[...] * 0.5

    def matmuls(r, zero):
        r0 = r * rs
        x = x_ref[r0:r0 + rs, :]
        if zero is not None:
            xb = pltpu.bitcast(x, jnp.uint32)
            x = pltpu.bitcast(xb | jnp.tile(zero, (xb.shape[0] // 8, xb.shape[1] // LANES)), BF16)
        yg_ref[pad + r0:pad + r0 + rs, :] = _dot(x, wg)
        yv_ref[pad + r0:pad + r0 + rs, :] = _dot(x, wv)

    def conv(y_ref, cw, b, r0):
        return (b + cw[2:3, :] * y_ref[pad + r0:pad + r0 + rs, :]
                + cw[1:2, :] * y_ref[pl.ds(pad + r0 - 1, rs), :]
                + cw[0:1, :] * y_ref[pl.ds(pad + r0 - 2, rs), :])

    def epilogue(r):
        r0 = r * rs
        zg = conv(yg_ref, cg, bg, r0)
        zv = conv(yv_ref, cv, bv, r0)
        t = jnp.tanh(zg * (GELU_C + (GELU_C * 0.044715) * (zg * zg)))
        out = ((zg + zg * t) * zv).astype(BF16)
        o_ref[r0:r0 + rs, :] = out
        bits = pltpu.bitcast(out, jnp.uint32)
        acc = bits[0:8, :]
        for k in range(1, bits.shape[0] // 8):
            acc = acc | bits[8 * k:8 * k + 8, :]
        red = acc[:, 0:LANES]
        for k in range(1, acc.shape[1] // LANES):
            red = red | acc[:, k * LANES:(k + 1) * LANES]
        return (red >> 16) >> 16

    zeros = {}
    matmuls(0, None)
    if npieces > 1:
        matmuls(1, None)
    for r in range(npieces):
        zeros[r] = epilogue(r)
        if r + 2 < npieces:
            matmuls(r + 2, zeros[r])
    tail_ref[j] = yg_ref[tm:tm + pad, :]
    tail_ref[nj + j] = yv_ref[tm:tm + pad, :]


def _mm_up(hn, w_up, w_conv, b_conv, layer, seq):
    m, d = hn.shape
    f = w_up.shape[2] // 2
    tm = _tile(seq, UP_TM)
    tn = MXU_COLS if f % MXU_COLS == 0 else f
    nj = f // tn
    b3 = b_conv.reshape(b_conv.shape[0], 1, 2 * f)
    return pl.pallas_call(
        functools.partial(_mm_up_kernel, tm=tm, nj=nj, tiles_per_seq=seq // tm),
        grid=(m // tm, nj),
        in_specs=[pl.BlockSpec((tm, d), lambda i, j: (i, 0), pipeline_mode=pl.Buffered(1)),
                  pl.BlockSpec((None, d, tn), lambda i, j: (layer, 0, j)),
                  pl.BlockSpec((None, d, tn), lambda i, j: (layer, 0, nj + j)),
                  pl.BlockSpec((None, CONV_K, tn), lambda i, j: (layer, 0, j)),
                  pl.BlockSpec((None, CONV_K, tn), lambda i, j: (layer, 0, nj + j)),
                  pl.BlockSpec((None, 1, tn), lambda i, j: (layer, 0, j)),
                  pl.BlockSpec((None, 1, tn), lambda i, j: (layer, 0, nj + j))],
        out_specs=pl.BlockSpec((tm, tn), lambda i, j: (i, j)),
        out_shape=jax.ShapeDtypeStruct((m, f), BF16),
        scratch_shapes=[pltpu.VMEM((2 * nj, 8, tn), F32),
                        pltpu.VMEM((tm + 8, tn), F32), pltpu.VMEM((tm + 8, tn), F32)],
        compiler_params=_cparams("arbitrary", "arbitrary"), name="mm_up",
    )(hn, w_up, w_up, w_conv, w_conv, b3, b3)


def _mm_kernel(x_ref, w_ref, o_ref):
    o_ref[...] = _dot(x_ref[...], w_ref[...]).astype(o_ref.dtype)


def _mm_down(x, w, layer):
    m, k = x.shape
    n = w.shape[2]
    tm = _tile(m, 1024)
    tn = _tile(n, MXU_COLS)
    return pl.pallas_call(
        _mm_kernel, grid=(m // tm, n // tn),
        in_specs=[pl.BlockSpec((tm, k), lambda i, j: (i, 0), pipeline_mode=pl.Buffered(1)),
                  pl.BlockSpec((None, k, tn), lambda i, j: (layer, 0, j))],
        out_specs=pl.BlockSpec((tm, tn), lambda i, j: (i, j)),
        out_shape=jax.ShapeDtypeStruct((m, n), BRANCH_DTYPE),
        compiler_params=_cparams("parallel", "parallel"), name="mm_down",
    )(x, w)


def kernel(x, w_in, w_gate_lr2, b_gate, lam_q1, lam_k1, lam_q2, lam_k2, g_subln, g_gla, w_pool, pool_scale, w_o, w_up, w_conv, b_conv, w_down, g_pre_mix, g_post_mix, g_pre_ffn, g_post_ffn):
    batch, seq, d = x.shape
    depth = w_in.shape[0]
    m = batch * seq
    assert seq % CHUNK == 0 and w_in.shape[2] == U_COLS + GATE_RANK
    assert UC0 == GB0 + B_WIDTH

    w_in_t = jnp.swapaxes(w_in, 1, 2)
    w_down_b = w_down.astype(BF16)
    w_pool_b = w_pool.astype(BF16)

    h = x.reshape(m, d)
    branch = None
    for l in range(depth):
        w_gate = jnp.pad(w_gate_lr2[l], ((0, LANES - GATE_RANK), (0, 0))).astype(BF16)

        if l == 0:
            hn = _norm_first(h, g_pre_mix[0])
        else:
            h, hn = _resnorm(h, branch, g_post_ffn[l - 1], g_pre_mix[l])
        u = _mm_in(hn, w_in_t, l)
        z, uc = _mm_tail(hn, w_in_t, l)
        lam_init = 0.8 - 0.6 * math.exp(-0.3 * l)
        oa = _att(u, lam_q1[l], lam_k1[l], lam_q2[l], lam_k2[l], g_subln[l], lam_init, batch, seq)
        ob = _gla(u, z, w_gate, b_gate[l], g_gla[l], batch, seq)
        oc = _pool(uc, w_pool_b[l], pool_scale[l], batch, seq)
        mix = _mm_out(oa, ob, oc, w_o, l)

        h, hn = _resnorm(h, mix, g_post_mix[l], g_pre_ffn[l])
        f_in = _mm_up(hn, w_up, w_conv, b_conv, l, seq)
        branch = _mm_down(f_in, w_down_b, l)
    h = _resnorm_last(h, branch, g_post_ffn[depth - 1])
    return h.reshape(batch, seq, d)
```

```python
import functools
import math

import jax
import jax.numpy as jnp
from jax import lax
from jax.experimental import pallas as pl
from jax.experimental.pallas import tpu as pltpu

F32 = jnp.float32
BF16 = jnp.bfloat16
BRANCH_DTYPE = jnp.bfloat16

CHUNK = 64
EPS = 1e-6
A_HEADS = 8
A_DH = 128
A_WIDTH = A_HEADS * 2 * A_DH
B_HEADS = 4
B_DK = 128
B_DV = 256
B_WIDTH = B_HEADS * B_DV
GATE_RANK = 16
GATE_TAU = 16.0
POOL_WINDOWS = (2, 4, 8, 16)
N_POOL = len(POOL_WINDOWS)
POOL_CH = 256
C_WIDTH = N_POOL * POOL_CH
CONV_K = 3

QA0 = 0
KA0 = A_WIDTH
VA0 = 2 * A_WIDTH
QB0 = 3 * A_WIDTH
KB0 = QB0 + B_HEADS * B_DK
VB0 = KB0 + B_HEADS * B_DK
GB0 = VB0 + B_WIDTH
UC0 = GB0 + B_WIDTH
U_COLS = UC0 + C_WIDTH

LANES = 128
MXU_COLS = 256
VMEM_LIMIT = 56 * 1024 * 1024
POOL_HALO = 16
NEG_BIG = -1e30
LOG2E = math.log2(math.e)
ATT_TQ = 512
ATT_GROUP = 1024
UP_TM = 2048
UP_ROWS = 256
GLA_SUB = 16
TAIL_ROWS = 1152


def _cparams(*sem):
    return pltpu.CompilerParams(dimension_semantics=sem, vmem_limit_bytes=VMEM_LIMIT)


def _dot(a, b):
    return jnp.dot(a, b, preferred_element_type=F32)


def _dot_nt(a, b):
    return lax.dot_general(a, b, (((1,), (1,)), ((), ())), preferred_element_type=F32)


def _dot_tn(a, b):
    return lax.dot_general(a, b, (((0,), (0,)), ((), ())), preferred_element_type=F32)


def _rms(x, g):
    ms = jnp.mean(x * x, axis=-1, keepdims=True)
    return x * lax.rsqrt(ms + EPS) * g


def _tile(n, want):
    t = min(n, want)
    assert n % t == 0, (n, want)
    return t


def _norm_first_kernel(h_ref, gpre_ref, hn_ref):
    hn_ref[...] = _rms(h_ref[...], gpre_ref[...]).astype(BF16)


def _resnorm_kernel(h_ref, br_ref, gpost_ref, gpre_ref, hout_ref, hn_ref):
    h = h_ref[...] + _rms(br_ref[...].astype(F32), gpost_ref[...])
    hout_ref[...] = h
    hn_ref[...] = _rms(h, gpre_ref[...]).astype(BF16)


def _resnorm_last_kernel(h_ref, br_ref, gpost_ref, hout_ref):
    hout_ref[...] = h_ref[...] + _rms(br_ref[...].astype(F32), gpost_ref[...])


def _norm_first(h, gpre):
    m, d = h.shape
    tr = _tile(m, 256)
    row = pl.BlockSpec((tr, d), lambda i: (i, 0))
    vec = pl.BlockSpec((1, d), lambda i: (0, 0))
    return pl.pallas_call(
        _norm_first_kernel, grid=(m // tr,), in_specs=[row, vec], out_specs=row,
        out_shape=jax.ShapeDtypeStruct((m, d), BF16),
        compiler_params=_cparams("parallel"), name="norm_first",
    )(h, gpre.reshape(1, d))


def _resnorm(h, br, gpost, gpre):
    m, d = h.shape
    tr = _tile(m, 256)
    row = pl.BlockSpec((tr, d), lambda i: (i, 0))
    vec = pl.BlockSpec((1, d), lambda i: (0, 0))
    return pl.pallas_call(
        _resnorm_kernel, grid=(m // tr,), in_specs=[row, row, vec, vec], out_specs=[row, row],
        out_shape=[jax.ShapeDtypeStruct((m, d), F32), jax.ShapeDtypeStruct((m, d), BF16)],
        compiler_params=_cparams("parallel"), name="resnorm",
    )(h, br, gpost.reshape(1, d), gpre.reshape(1, d))


def _resnorm_last(h, br, gpost):
    m, d = h.shape
    tr = _tile(m, 256)
    row = pl.BlockSpec((tr, d), lambda i: (i, 0))
    vec = pl.BlockSpec((1, d), lambda i: (0, 0))
    return pl.pallas_call(
        _resnorm_last_kernel, grid=(m // tr,), in_specs=[row, row, vec], out_specs=row,
        out_shape=jax.ShapeDtypeStruct((m, d), F32),
        compiler_params=_cparams("parallel"), name="resnorm_last",
    )(h, br, gpost.reshape(1, d))


def _mm_in_kernel(x_ref, w_ref, u_ref):
    u_ref[...] = _dot_nt(x_ref[...], w_ref[...].astype(BF16)).astype(BF16)


def _mm_in(hn, w_in_t, layer):
    m, d = hn.shape
    tm = _tile(m, 1024)
    tn = 512
    assert UC0 % tn == 0
    return pl.pallas_call(
        _mm_in_kernel, grid=(m // tm, UC0 // tn),
        in_specs=[pl.BlockSpec((tm, d), lambda i, j: (i, 0)),
                  pl.BlockSpec((None, tn, d), lambda i, j: (layer, j, 0))],
        out_specs=pl.BlockSpec((tm, tn), lambda i, j: (i, j)),
        out_shape=jax.ShapeDtypeStruct((m, UC0), BF16),
        compiler_params=_cparams("parallel", "arbitrary"), name="mm_in",
    )(hn, w_in_t)


def _mm_tail_kernel(x_ref, w_ref, z_ref, uc_ref):
    x = x_ref[...]
    z_ref[...] = _dot_nt(x, w_ref[0:LANES, :].astype(BF16))
    uc_ref[...] = _dot_nt(x, w_ref[GATE_RANK:GATE_RANK + C_WIDTH, :].astype(BF16)).astype(BF16)


def _mm_tail(hn, w_in_t, layer):
    m, d = hn.shape
    tm = _tile(m, 1024)
    rows = TAIL_ROWS
    assert UC0 % rows == 0 and rows >= GATE_RANK + C_WIDTH and GATE_RANK % 8 == 0
    return pl.pallas_call(
        _mm_tail_kernel, grid=(m // tm,),
        in_specs=[pl.BlockSpec((tm, d), lambda i: (i, 0)),
                  pl.BlockSpec((None, rows, d), lambda i: (layer, UC0 // rows, 0),
                               pipeline_mode=pl.Buffered(1))],
        out_specs=[pl.BlockSpec((tm, LANES), lambda i: (i, 0)),
                   pl.BlockSpec((tm, C_WIDTH), lambda i: (i, 0))],
        out_shape=[jax.ShapeDtypeStruct((m, LANES), F32), jax.ShapeDtypeStruct((m, C_WIDTH), BF16)],
        compiler_params=_cparams("arbitrary"), name="mm_tail",
    )(hn, w_in_t)


def _att_kernel(lq1_ref, lk1_ref, lq2_ref, lk2_ref, g_ref, q_ref, k_ref, v_ref, o_ref,
                s1_ref, s2_ref, acc1_ref, acc2_ref, bias_ref, diag_ref, *, lam_init, t, G):
    head = pl.program_id(1)
    qi = pl.program_id(2)
    qscale = LOG2E / math.sqrt(A_DH)
    slope = lax.bitcast_convert_type(
        jnp.full((1, 1), (126 - head) << 23, jnp.int32), F32) * LOG2E

    q = q_ref[...].astype(F32) * qscale
    qt1 = q[:, :A_DH].T.astype(BF16)
    qt2 = q[:, A_DH:].T.astype(BF16)

    def fold8(x, op):
        return op(x.reshape(G // 8, 8, t), axis=0)

    @pl.when(qi == 0)
    def _():
        kk = lax.broadcasted_iota(jnp.int32, (G, t), 0)
        bias_ref[...] = slope * kk.astype(F32)
        cshift = CHUNK.bit_length() - 1
        kd = lax.broadcasted_iota(jnp.int32, (t, t), 0)
        rd = lax.broadcasted_iota(jnp.int32, (t, t), 1)
        corr = slope * (rd - jnp.abs(rd - kd) - kd).astype(F32)
        diag_ref[...] = jnp.where((kd >> cshift) <= (rd >> cshift), corr, NEG_BIG)

    qs = qi * t
    gl = qs // G

    def offset(g):
        return slope * (g * G - qs).astype(F32)

    def score_body(g, carry):
        m1, m2 = carry
        k0 = pl.multiple_of(g * G, G)
        kt = k_ref[pl.ds(k0, G), :]
        s1 = _dot(kt[:, :A_DH], qt1) + bias_ref[...]
        s2 = _dot(kt[:, A_DH:], qt2) + bias_ref[...]
        s1_ref[pl.ds(k0, G), :] = s1
        s2_ref[pl.ds(k0, G), :] = s2
        return (jnp.maximum(m1, fold8(s1, jnp.max) + offset(g)),
                jnp.maximum(m2, fold8(s2, jnp.max) + offset(g)))

    l0 = pl.multiple_of(gl * G, G)
    jd = (qs - l0) // t
    kl = k_ref[pl.ds(l0, G), :]
    sl1 = _dot(kl[:, :A_DH], qt1) + bias_ref[...]
    sl2 = _dot(kl[:, A_DH:], qt2) + bias_ref[...]
    s1_ref[pl.ds(l0, G), :] = sl1
    s2_ref[pl.ds(l0, G), :] = sl2

    def fold8t(x, op):
        return op(x.reshape(t // 8, 8, t), axis=0)

    m1 = jnp.full((8, t), NEG_BIG, F32)
    m2 = m1
    for j in range(G // t - 1):
        skip = jnp.where(j < jd, 0.0, NEG_BIG)
        m1 = jnp.maximum(m1, fold8t(sl1[j * t:(j + 1) * t], jnp.max) + skip)
        m2 = jnp.maximum(m2, fold8t(sl2[j * t:(j + 1) * t], jnp.max) + skip)
    d0 = pl.multiple_of(qs, t)
    sd1 = s1_ref[pl.ds(d0, t), :] + diag_ref[...]
    sd2 = s2_ref[pl.ds(d0, t), :] + diag_ref[...]
    s1_ref[pl.ds(d0, t), :] = sd1
    s2_ref[pl.ds(d0, t), :] = sd2
    for j in range(1, G // t):
        @pl.when(j > jd)
        def _():
            j0 = pl.multiple_of(l0 + j * t, t)
            s1_ref[pl.ds(j0, t), :] = jnp.full((t, t), NEG_BIG, F32)
            s2_ref[pl.ds(j0, t), :] = jnp.full((t, t), NEG_BIG, F32)
    m1 = jnp.maximum(m1, fold8t(sd1, jnp.max)) + offset(gl)
    m2 = jnp.maximum(m2, fold8t(sd2, jnp.max)) + offset(gl)
    m1, m2 = lax.fori_loop(0, gl, score_body, (m1, m2))
    m1 = jnp.max(m1, axis=0, keepdims=True)
    m2 = jnp.max(m2, axis=0, keepdims=True)

    acc1_ref[...] = jnp.zeros_like(acc1_ref)
    acc2_ref[...] = jnp.zeros_like(acc2_ref)

    def pv_body(g, carry):
        l1, l2 = carry
        k0 = pl.multiple_of(g * G, G)
        vt = v_ref[pl.ds(k0, G), :]
        p1 = jnp.exp2(s1_ref[pl.ds(k0, G), :] - (m1 - offset(g)))
        p2 = jnp.exp2(s2_ref[pl.ds(k0, G), :] - (m2 - offset(g)))
        acc1_ref[...] += _dot_tn(vt, p1.astype(BF16))
        acc2_ref[...] += _dot_tn(vt, p2.astype(BF16))
        return l1 + fold8(p1, jnp.sum), l2 + fold8(p2, jnp.sum)

    zero8 = jnp.zeros((8, t), F32)
    l1, l2 = lax.fori_loop(0, gl + 1, pv_body, (zero8, zero8))
    l1 = jnp.sum(l1, axis=0, keepdims=True)
    l2 = jnp.sum(l2, axis=0, keepdims=True)

    lam = (jnp.exp(jnp.sum(lq1_ref[...] * lk1_ref[...], axis=-1, keepdims=True))
           - jnp.exp(jnp.sum(lq2_ref[...] * lk2_ref[...], axis=-1, keepdims=True))
           + lam_init)
    ot = acc1_ref[...] / l1 - lam * (acc2_ref[...] / l2)
    ms = jnp.mean(ot * ot, axis=0, keepdims=True)
    o = (ot * lax.rsqrt(ms + EPS)).T
    o_ref[...] = (o * g_ref[...] * (1.0 - lam_init)).astype(BF16)


def _att(u, lq1, lk1, lq2, lk2, g_subln, lam_init, batch, seq):
    m = u.shape[0]
    t = _tile(seq, ATT_TQ)
    G = _tile(seq, ATT_GROUP)
    nq = seq // t
    e = 2 * A_DH
    assert G % t == 0
    vec = pl.BlockSpec((1, A_DH), lambda b, h, i: (0, 0))
    return pl.pallas_call(
        functools.partial(_att_kernel, lam_init=lam_init, t=t, G=G),
        grid=(batch, A_HEADS, nq),
        in_specs=[vec, vec, vec, vec,
                  pl.BlockSpec((1, e), lambda b, h, i: (0, 0)),
                  pl.BlockSpec((t, e), lambda b, h, i: (b * nq + i, QA0 // e + h)),
                  pl.BlockSpec((seq, e), lambda b, h, i: (b, KA0 // e + h)),
                  pl.BlockSpec((seq, e), lambda b, h, i: (b, VA0 // e + h))],
        out_specs=pl.BlockSpec((t, e), lambda b, h, i: (b * nq + i, h)),
        out_shape=jax.ShapeDtypeStruct((m, A_WIDTH), BF16),
        scratch_shapes=[pltpu.VMEM((seq, t), F32), pltpu.VMEM((seq, t), F32),
                        pltpu.VMEM((e, t), F32), pltpu.VMEM((e, t), F32),
                        pltpu.VMEM((G, t), F32), pltpu.VMEM((t, t), F32)],
        compiler_params=_cparams("arbitrary", "arbitrary", "arbitrary"), name="diff_attn",
    )(lq1.reshape(1, A_DH), lk1.reshape(1, A_DH), lq2.reshape(1, A_DH), lk2.reshape(1, A_DH),
      g_subln.reshape(1, e), u, u, u)


def _gla_kernel(z_ref, wg_ref, bg_ref, gn_ref, q_ref, k_ref, v_ref, gb_ref, o_ref, *, seq):
    L = CHUNK
    wg = wg_ref[...]
    bg = bg_ref[...]
    gn = gn_ref[...]
    rows = lax.broadcasted_iota(jnp.int32, (L, B_DK), 0)
    ti = lax.broadcasted_iota(jnp.int32, (L, L), 0)
    si = lax.broadcasted_iota(jnp.int32, (L, L), 1)
    SUB = GLA_SUB
    assert L == 4 * SUB
    sshift = SUB.bit_length() - 1
    bi = ti >> sshift
    bj = si >> sshift
    sd = si - (ti - (ti & (SUB - 1)))
    m_diag = (bi == bj) & (si <= ti)
    m_near = (bi == bj + 1) & ((bi & 1) == 1)
    m_far = (bi >= 2) & (bj <= 1)
    qscale = B_DK ** -0.5

    def chunk(c, state_t):
        r0 = pl.multiple_of(c * L, L)
        x = _dot(z_ref[pl.ds(r0, L), :].astype(BF16), wg) + bg
        la = (jnp.minimum(x, 0.0) - jnp.log(1.0 + jnp.exp(-jnp.abs(x)))) * (1.0 / GATE_TAU)
        b = la
        sh = 1
        while sh < L:
            b = b + jnp.where(rows >= sh, pltpu.roll(b, sh, axis=0), 0.0)
            sh *= 2
        b = b * LOG2E
        b_last = b[L - 1:L, :]
        qf = q_ref[pl.ds(r0, L), :].astype(F32) * qscale
        kc = k_ref[pl.ds(r0, L), :]
        kf = kc.astype(F32)
        vc = v_ref[pl.ds(r0, L), :]

        def rescaled(ref):
            qr = (qf * jnp.exp2(jnp.minimum(b - ref, 0.0))).astype(BF16)
            kr = (kf * jnp.exp2(jnp.minimum(ref - b, 0.0))).astype(BF16)
            return _dot_nt(qr, kr)

        half = L // 2
        a_far = rescaled(b[half - 1:half, :])
        ref_near = jnp.where(rows < half, b[SUB - 1:SUB, :], b[half + SUB - 1:half + SUB, :])
        a_near = rescaled(ref_near)
        a = jnp.zeros((L, L), F32)
        for s in range(SUB):
            ref = jnp.concatenate(
                [jnp.broadcast_to(b[i * SUB + s:i * SUB + s + 1, :], (SUB, B_DK)) for i in range(L // SUB)],
                axis=0)
            e = jnp.exp2(jnp.minimum(b - ref, 0.0))
            rs = _dot_nt((qf * e).astype(BF16), kc)
            a = jnp.where(sd == s, rs, a)
        a = jnp.where(m_far, a_far, jnp.where(m_near, a_near, jnp.where(m_diag, a, 0.0)))

        qe = (qf * jnp.exp2(b)).astype(BF16)
        o = _dot(a.astype(BF16), vc) + _dot_nt(qe, state_t.astype(BF16))
        ke = (kf * jnp.exp2(b_last - b)).astype(BF16)
        new_state = jnp.exp2(b_last) * state_t + _dot_tn(vc, ke)

        gate = gb_ref[pl.ds(r0, L), :].astype(F32)
        y = _rms(o, gn) * (gate / (1.0 + jnp.exp(-gate)))
        o_ref[pl.ds(r0, L), :] = y.astype(BF16)
        return new_state

    nc = seq // L
    unroll = 8 if nc % 8 == 0 else 1

    def chunks(c, state_t):
        for k in range(unroll):
            state_t = chunk(c * unroll + k, state_t)
        return state_t

    lax.fori_loop(0, nc // unroll, chunks, jnp.zeros((B_DV, B_DK), F32))


def _gla(u, z, w_gate, b_gate, g_gla, batch, seq):
    m = u.shape[0]
    return pl.pallas_call(
        functools.partial(_gla_kernel, seq=seq),
        grid=(batch, B_HEADS),
        in_specs=[pl.BlockSpec((seq, LANES), lambda b, h: (b, 0)),
                  pl.BlockSpec((LANES, B_DK), lambda b, h: (0, h)),
                  pl.BlockSpec((1, B_DK), lambda b, h: (0, h)),
                  pl.BlockSpec((1, B_DV), lambda b, h: (0, h)),
                  pl.BlockSpec((seq, B_DK), lambda b, h: (b, QB0 // B_DK + h)),
                  pl.BlockSpec((seq, B_DK), lambda b, h: (b, KB0 // B_DK + h)),
                  pl.BlockSpec((seq, B_DV), lambda b, h: (b, VB0 // B_DV + h)),
                  pl.BlockSpec((seq, B_DV), lambda b, h: (b, GB0 // B_DV + h))],
        out_specs=pl.BlockSpec((seq, B_DV), lambda b, h: (b, h)),
        out_shape=jax.ShapeDtypeStruct((m, B_WIDTH), BF16),
        compiler_params=_cparams("parallel", "parallel"), name="gla",
    )(z, w_gate, b_gate.reshape(1, -1), g_gla.reshape(1, -1), u, u, u, u)


def _pool_kernel(u_ref, w_ref, sc_ref, o_ref, carry_ref, *, tr):
    ti = pl.program_id(1)

    @pl.when(ti == 0)
    def _():
        carry_ref[...] = jnp.zeros_like(carry_ref)

    x = u_ref[...].astype(F32)
    ext = jnp.concatenate([carry_ref[...], x], axis=0)
    carry_ref[...] = x[tr - POOL_HALO:, :]
    pos = ti * tr + lax.broadcasted_iota(jnp.int32, (tr, 1), 0)
    for g, w in enumerate(POOL_WINDOWS):
        lo, hi = g * POOL_CH, (g + 1) * POOL_CH
        s = ext[:, lo:hi]
        sh = 1
        while sh < w:
            s = s + pltpu.roll(s, sh, axis=0)
            sh *= 2
        cnt = jnp.minimum(pos + 1, w).astype(F32)
        d = s[POOL_HALO:, :] / cnt - x[:, lo:hi]
        y = _dot(d.astype(BF16), w_ref[g]) * sc_ref[:, lo:hi]
        o_ref[:, lo:hi] = y.astype(BF16)


def _pool(uc, w_pool, pool_scale, batch, seq):
    m = uc.shape[0]
    tr = _tile(seq, 512)
    nt = seq // tr
    assert tr >= POOL_HALO and max(POOL_WINDOWS) <= POOL_HALO
    return pl.pallas_call(
        functools.partial(_pool_kernel, tr=tr),
        grid=(batch, nt),
        in_specs=[pl.BlockSpec((tr, C_WIDTH), lambda b, i: (b * nt + i, 0)),
                  pl.BlockSpec((N_POOL, POOL_CH, POOL_CH), lambda b, i: (0, 0, 0)),
                  pl.BlockSpec((1, C_WIDTH), lambda b, i: (0, 0))],
        out_specs=pl.BlockSpec((tr, C_WIDTH), lambda b, i: (b * nt + i, 0)),
        out_shape=jax.ShapeDtypeStruct((m, C_WIDTH), BF16),
        scratch_shapes=[pltpu.VMEM((POOL_HALO, C_WIDTH), F32)],
        compiler_params=_cparams("parallel", "arbitrary"), name="pool",
    )(uc, w_pool, pool_scale.reshape(1, C_WIDTH))


def _mm_out_kernel(a_ref, b_ref, c_ref, wa_ref, wb_ref, wc_ref, o_ref):
    o_ref[...] = (_dot(a_ref[...], wa_ref[...].astype(BF16))
                  + _dot(b_ref[...], wb_ref[...].astype(BF16))
                  + _dot(c_ref[...], wc_ref[...].astype(BF16))).astype(o_ref.dtype)


def _mm_out(oa, ob, oc, w_o, layer):
    m = oa.shape[0]
    n = w_o.shape[2]
    tm = _tile(m, 1024)
    tn = _tile(n, 512)
    assert A_WIDTH % B_WIDTH == 0 and B_WIDTH == C_WIDTH
    return pl.pallas_call(
        _mm_out_kernel, grid=(m // tm, n // tn),
        in_specs=[pl.BlockSpec((tm, A_WIDTH), lambda i, j: (i, 0)),
                  pl.BlockSpec((tm, B_WIDTH), lambda i, j: (i, 0)),
                  pl.BlockSpec((tm, C_WIDTH), lambda i, j: (i, 0)),
                  pl.BlockSpec((None, A_WIDTH, tn), lambda i, j: (layer, 0, j)),
                  pl.BlockSpec((None, B_WIDTH, tn), lambda i, j: (layer, A_WIDTH // B_WIDTH, j)),
                  pl.BlockSpec((None, C_WIDTH, tn), lambda i, j: (layer, A_WIDTH // C_WIDTH + 1, j))],
        out_specs=pl.BlockSpec((tm, tn), lambda i, j: (i, j)),
        out_shape=jax.ShapeDtypeStruct((m, n), BRANCH_DTYPE),
        compiler_params=_cparams("parallel", "parallel"), name="mm_out",
    )(oa, ob, oc, w_o, w_o, w_o)


GELU_C = math.sqrt(2.0 / math.pi)


def _mm_up_kernel(x_ref, wg_ref, wv_ref, cg_ref, cv_ref, bg_ref, bv_ref, o_ref,
                  tail_ref, yg_ref, yv_ref, *, tm, nj, tiles_per_seq):
    i = pl.program_id(0)
    j = pl.program_id(1)
    seq_start = (i % tiles_per_seq) == 0
    pad = 8

    @pl.when(seq_start)
    def _():
        tail_ref[j] = jnp.zeros((pad, tail_ref.shape[2]), F32)
        tail_ref[nj + j] = jnp.zeros((pad, tail_ref.shape[2]), F32)

    rs = min(tm, UP_ROWS)
    npieces = tm // rs
    wg = wg_ref[...].astype(BF16)
    wv = wv_ref[...].astype(BF16)
    yg_ref[0:pad, :] = tail_ref[j]
    yv_ref[0:pad, :] = tail_ref[nj + j]
    cg = cg_ref[...]
    cv = cv_ref[...] * 0.5
    bg = bg_ref[...]
    bv = bv_ref[...] * 0.5

    def matmuls(r, zero):
        r0 = r * rs
        x = x_ref[r0:r0 + rs, :]
        if zero is not None:
            xb = pltpu.bitcast(x, jnp.uint32)
            x = pltpu.bitcast(xb | jnp.tile(zero, (xb.shape[0] // 8, xb.shape[1] // LANES)), BF16)
        yg_ref[pad + r0:pad + r0 + rs, :] = _dot(x, wg)
        yv_ref[pad + r0:pad + r0 + rs, :] = _dot(x, wv)

    def conv(y_ref, cw, b, r0):
        return (b + cw[2:3, :] * y_ref[pad + r0:pad + r0 + rs, :]
                + cw[1:2, :] * y_ref[pl.ds(pad + r0 - 1, rs), :]
                + cw[0:1, :] * y_ref[pl.ds(pad + r0 - 2, rs), :])

    def epilogue(r):
        r0 = r * rs
        zg = conv(yg_ref, cg, bg, r0)
        zv = conv(yv_ref, cv, bv, r0)
        t = jnp.tanh(zg * (GELU_C + (GELU_C * 0.044715) * (zg * zg)))
        out = ((zg + zg * t) * zv).astype(BF16)
        o_ref[r0:r0 + rs, :] = out
        bits = pltpu.bitcast(out, jnp.uint32)
        acc = bits[0:8, :]
        for k in range(1, bits.shape[0] // 8):
            acc = acc | bits[8 * k:8 * k + 8, :]
        red = acc[:, 0:LANES]
        for k in range(1, acc.shape[1] // LANES):
            red = red | acc[:, k * LANES:(k + 1) * LANES]
        return (red >> 16) >> 16

    zeros = {}
    matmuls(0, None)
    if npieces > 1:
        matmuls(1, None)
    for r in range(npieces):
        zeros[r] = epilogue(r)
        if r + 2 < npieces:
            matmuls(r + 2, zeros[r])
    tail_ref[j] = yg_ref[tm:tm + pad, :]
    tail_ref[nj + j] = yv_ref[tm:tm + pad, :]


def _mm_up(hn, w_up, w_conv, b_conv, layer, seq):
    m, d = hn.shape
    f = w_up.shape[2] // 2
    tm = _tile(seq, UP_TM)
    tn = MXU_COLS if f % MXU_COLS == 0 else f
    nj = f // tn
    b3 = b_conv.reshape(b_conv.shape[0], 1, 2 * f)
    return pl.pallas_call(
        functools.partial(_mm_up_kernel, tm=tm, nj=nj, tiles_per_seq=seq // tm),
        grid=(m // tm, nj),
        in_specs=[pl.BlockSpec((tm, d), lambda i, j: (i, 0), pipeline_mode=pl.Buffered(1)),
                  pl.BlockSpec((None, d, tn), lambda i, j: (layer, 0, j)),
                  pl.BlockSpec((None, d, tn), lambda i, j: (layer, 0, nj + j)),
                  pl.BlockSpec((None, CONV_K, tn), lambda i, j: (layer, 0, j)),
                  pl.BlockSpec((None, CONV_K, tn), lambda i, j: (layer, 0, nj + j)),
                  pl.BlockSpec((None, 1, tn), lambda i, j: (layer, 0, j)),
                  pl.BlockSpec((None, 1, tn), lambda i, j: (layer, 0, nj + j))],
        out_specs=pl.BlockSpec((tm, tn), lambda i, j: (i, j)),
        out_shape=jax.ShapeDtypeStruct((m, f), BF16),
        scratch_shapes=[pltpu.VMEM((2 * nj, 8, tn), F32),
                        pltpu.VMEM((tm + 8, tn), F32), pltpu.VMEM((tm + 8, tn), F32)],
        compiler_params=_cparams("arbitrary", "arbitrary"), name="mm_up",
    )(hn, w_up, w_up, w_conv, w_conv, b3, b3)


def _mm_kernel(x_ref, w_ref, o_ref):
    o_ref[...] = _dot(x_ref[...], w_ref[...].astype(BF16)).astype(o_ref.dtype)


def _mm_down(x, w, layer):
    m, k = x.shape
    n = w.shape[2]
    tm = _tile(m, 1024)
    tn = _tile(n, MXU_COLS)
    return pl.pallas_call(
        _mm_kernel, grid=(m // tm, n // tn),
        in_specs=[pl.BlockSpec((tm, k), lambda i, j: (i, 0), pipeline_mode=pl.Buffered(1)),
                  pl.BlockSpec((None, k, tn), lambda i, j: (layer, 0, j))],
        out_specs=pl.BlockSpec((tm, tn), lambda i, j: (i, j)),
        out_shape=jax.ShapeDtypeStruct((m, n), BRANCH_DTYPE),
        compiler_params=_cparams("parallel", "parallel"), name="mm_down",
    )(x, w)


def kernel(x, w_in, w_gate_lr2, b_gate, lam_q1, lam_k1, lam_q2, lam_k2, g_subln, g_gla, w_pool, pool_scale, w_o, w_up, w_conv, b_conv, w_down, g_pre_mix, g_post_mix, g_pre_ffn, g_post_ffn):
    batch, seq, d = x.shape
    depth = w_in.shape[0]
    m = batch * seq
    assert seq % CHUNK == 0 and w_in.shape[2] == U_COLS + GATE_RANK
    assert UC0 == GB0 + B_WIDTH

    w_in_t = jnp.swapaxes(w_in, 1, 2)
    w_pool_b = w_pool.astype(BF16)

    h = x.reshape(m, d)
    branch = None
    for l in range(depth):
        w_gate = jnp.pad(w_gate_lr2[l], ((0, LANES - GATE_RANK), (0, 0))).astype(BF16)

        if l == 0:
            hn = _norm_first(h, g_pre_mix[0])
        else:
            h, hn = _resnorm(h, branch, g_post_ffn[l - 1], g_pre_mix[l])
        u = _mm_in(hn, w_in_t, l)
        z, uc = _mm_tail(hn, w_in_t, l)
        lam_init = 0.8 - 0.6 * math.exp(-0.3 * l)
        oa = _att(u, lam_q1[l], lam_k1[l], lam_q2[l], lam_k2[l], g_subln[l], lam_init, batch, seq)
        ob = _gla(u, z, w_gate, b_gate[l], g_gla[l], batch, seq)
        oc = _pool(uc, w_pool_b[l], pool_scale[l], batch, seq)
        mix = _mm_out(oa, ob, oc, w_o, l)

        h, hn = _resnorm(h, mix, g_post_mix[l], g_pre_ffn[l])
        f_in = _mm_up(hn, w_up, w_conv, b_conv, l, seq)
        branch = _mm_down(f_in, w_down, l)
    h = _resnorm_last(h, branch, g_post_ffn[depth - 1])
    return h.reshape(batch, seq, d)
```

```python
import functools
import math

import jax
import jax.numpy as jnp
from jax import lax
from jax.experimental import pallas as pl
from jax.experimental.pallas import tpu as pltpu

F32 = jnp.float32
BF16 = jnp.bfloat16
BRANCH_DTYPE = jnp.bfloat16

CHUNK = 64
EPS = 1e-6
A_HEADS = 8
A_DH = 128
A_WIDTH = A_HEADS * 2 * A_DH
B_HEADS = 4
B_DK = 128
B_DV = 256
B_WIDTH = B_HEADS * B_DV
GATE_RANK = 16
GATE_TAU = 16.0
POOL_WINDOWS = (2, 4, 8, 16)
N_POOL = len(POOL_WINDOWS)
POOL_CH = 256
C_WIDTH = N_POOL * POOL_CH
CONV_K = 3

QA0 = 0
KA0 = A_WIDTH
VA0 = 2 * A_WIDTH
QB0 = 3 * A_WIDTH
KB0 = QB0 + B_HEADS * B_DK
VB0 = KB0 + B_HEADS * B_DK
GB0 = VB0 + B_WIDTH
UC0 = GB0 + B_WIDTH
U_COLS = UC0 + C_WIDTH

LANES = 128
MXU_COLS = 256
VMEM_LIMIT = 56 * 1024 * 1024
POOL_HALO = 16
NEG_BIG = -1e30
LOG2E = math.log2(math.e)
NORM_ROWS = 256
ATT_TQ = 512
ATT_GROUP = 1024
UP_TM = 2048
UP_ROWS = 256
UP_LAG = 2
GLA_SUB = 16
TAIL_ROWS = 1152


def _cparams(*sem):
    return pltpu.CompilerParams(dimension_semantics=sem, vmem_limit_bytes=VMEM_LIMIT)


def _dot(a, b):
    return jnp.dot(a, b, preferred_element_type=F32)


def _dot_nt(a, b):
    return lax.dot_general(a, b, (((1,), (1,)), ((), ())), preferred_element_type=F32)


def _dot_tn(a, b):
    return lax.dot_general(a, b, (((0,), (0,)), ((), ())), preferred_element_type=F32)


def _rms(x, g):
    ms = jnp.mean(x * x, axis=-1, keepdims=True)
    return x * lax.rsqrt(ms + EPS) * g


def _tile(n, want):
    t = min(n, want)
    assert n % t == 0, (n, want)
    return t


def _norm_first_kernel(h_ref, gpre_ref, hn_ref):
    hn_ref[...] = _rms(h_ref[...], gpre_ref[...]).astype(BF16)


def _resnorm_kernel(h_ref, br_ref, gpost_ref, gpre_ref, hout_ref, hn_ref):
    h = h_ref[...] + _rms(br_ref[...].astype(F32), gpost_ref[...])
    hout_ref[...] = h
    hn_ref[...] = _rms(h, gpre_ref[...]).astype(BF16)


def _resnorm_last_kernel(h_ref, br_ref, gpost_ref, hout_ref):
    hout_ref[...] = h_ref[...] + _rms(br_ref[...].astype(F32), gpost_ref[...])


def _norm_first(h, gpre):
    m, d = h.shape
    tr = _tile(m, NORM_ROWS)
    row = pl.BlockSpec((tr, d), lambda i: (i, 0))
    vec = pl.BlockSpec((1, d), lambda i: (0, 0))
    return pl.pallas_call(
        _norm_first_kernel, grid=(m // tr,), in_specs=[row, vec], out_specs=row,
        out_shape=jax.ShapeDtypeStruct((m, d), BF16),
        compiler_params=_cparams("parallel"), name="norm_first",
    )(h, gpre.reshape(1, d))


def _resnorm(h, br, gpost, gpre):
    m, d = h.shape
    tr = _tile(m, NORM_ROWS)
    row = pl.BlockSpec((tr, d), lambda i: (i, 0))
    vec = pl.BlockSpec((1, d), lambda i: (0, 0))
    return pl.pallas_call(
        _resnorm_kernel, grid=(m // tr,), in_specs=[row, row, vec, vec], out_specs=[row, row],
        out_shape=[jax.ShapeDtypeStruct((m, d), F32), jax.ShapeDtypeStruct((m, d), BF16)],
        compiler_params=_cparams("parallel"), name="resnorm",
    )(h, br, gpost.reshape(1, d), gpre.reshape(1, d))


def _resnorm_last(h, br, gpost):
    m, d = h.shape
    tr = _tile(m, NORM_ROWS)
    row = pl.BlockSpec((tr, d), lambda i: (i, 0))
    vec = pl.BlockSpec((1, d), lambda i: (0, 0))
    return pl.pallas_call(
        _resnorm_last_kernel, grid=(m // tr,), in_specs=[row, row, vec], out_specs=row,
        out_shape=jax.ShapeDtypeStruct((m, d), F32),
        compiler_params=_cparams("parallel"), name="resnorm_last",
    )(h, br, gpost.reshape(1, d))


def _mm_in_kernel(x_ref, w_ref, u_ref):
    u_ref[...] = _dot_nt(x_ref[...], w_ref[...].astype(BF16)).astype(BF16)


def _mm_in(hn, w_in_t, layer):
    m, d = hn.shape
    tm = _tile(m, 1024)
    tn = 512
    assert UC0 % tn == 0
    return pl.pallas_call(
        _mm_in_kernel, grid=(m // tm, UC0 // tn),
        in_specs=[pl.BlockSpec((tm, d), lambda i, j: (i, 0)),
                  pl.BlockSpec((None, tn, d), lambda i, j: (layer, j, 0))],
        out_specs=pl.BlockSpec((tm, tn), lambda i, j: (i, j)),
        out_shape=jax.ShapeDtypeStruct((m, UC0), BF16),
        compiler_params=_cparams("parallel", "arbitrary"), name="mm_in",
    )(hn, w_in_t)


def _mm_tail_kernel(x_ref, w_ref, z_ref, uc_ref):
    x = x_ref[...]
    z_ref[...] = _dot_nt(x, w_ref[0:LANES, :].astype(BF16))
    uc_ref[...] = _dot_nt(x, w_ref[GATE_RANK:GATE_RANK + C_WIDTH, :].astype(BF16)).astype(BF16)


def _mm_tail(hn, w_in_t, layer):
    m, d = hn.shape
    tm = _tile(m, 1024)
    rows = TAIL_ROWS
    assert UC0 % rows == 0 and rows >= GATE_RANK + C_WIDTH and GATE_RANK % 8 == 0
    return pl.pallas_call(
        _mm_tail_kernel, grid=(m // tm,),
        in_specs=[pl.BlockSpec((tm, d), lambda i: (i, 0)),
                  pl.BlockSpec((None, rows, d), lambda i: (layer, UC0 // rows, 0),
                               pipeline_mode=pl.Buffered(1))],
        out_specs=[pl.BlockSpec((tm, LANES), lambda i: (i, 0)),
                   pl.BlockSpec((tm, C_WIDTH), lambda i: (i, 0))],
        out_shape=[jax.ShapeDtypeStruct((m, LANES), F32), jax.ShapeDtypeStruct((m, C_WIDTH), BF16)],
        compiler_params=_cparams("arbitrary"), name="mm_tail",
    )(hn, w_in_t)


def _att_kernel(lq1_ref, lk1_ref, lq2_ref, lk2_ref, g_ref, q_ref, k_ref, v_ref, o_ref,
                s1_ref, s2_ref, acc1_ref, acc2_ref, bias_ref, diag_ref, *, lam_init, t, G):
    head = pl.program_id(1)
    qi = pl.program_id(2)
    qscale = LOG2E / math.sqrt(A_DH)
    slope = lax.bitcast_convert_type(
        jnp.full((1, 1), (126 - head) << 23, jnp.int32), F32) * LOG2E

    q = q_ref[...].astype(F32) * qscale
    qt1 = q[:, :A_DH].T.astype(BF16)
    qt2 = q[:, A_DH:].T.astype(BF16)

    def fold8(x, op):
        return op(x.reshape(G // 8, 8, t), axis=0)

    @pl.when(qi == 0)
    def _():
        kk = lax.broadcasted_iota(jnp.int32, (G, t), 0)
        bias_ref[...] = slope * kk.astype(F32)
        cshift = CHUNK.bit_length() - 1
        kd = lax.broadcasted_iota(jnp.int32, (t, t), 0)
        rd = lax.broadcasted_iota(jnp.int32, (t, t), 1)
        corr = slope * (rd - jnp.abs(rd - kd) - kd).astype(F32)
        diag_ref[...] = jnp.where((kd >> cshift) <= (rd >> cshift), corr, NEG_BIG)

    qs = qi * t
    gl = qs // G

    def offset(g):
        return slope * (g * G - qs).astype(F32)

    def score_body(g, carry):
        m1, m2 = carry
        k0 = pl.multiple_of(g * G, G)
        kt = k_ref[pl.ds(k0, G), :]
        s1 = _dot(kt[:, :A_DH], qt1) + bias_ref[...]
        s2 = _dot(kt[:, A_DH:], qt2) + bias_ref[...]
        s1_ref[pl.ds(k0, G), :] = s1
        s2_ref[pl.ds(k0, G), :] = s2
        return (jnp.maximum(m1, fold8(s1, jnp.max) + offset(g)),
                jnp.maximum(m2, fold8(s2, jnp.max) + offset(g)))

    l0 = pl.multiple_of(gl * G, G)
    jd = (qs - l0) // t
    kl = k_ref[pl.ds(l0, G), :]
    sl1 = _dot(kl[:, :A_DH], qt1) + bias_ref[...]
    sl2 = _dot(kl[:, A_DH:], qt2) + bias_ref[...]
    s1_ref[pl.ds(l0, G), :] = sl1
    s2_ref[pl.ds(l0, G), :] = sl2

    def fold8t(x, op):
        return op(x.reshape(t // 8, 8, t), axis=0)

    m1 = jnp.full((8, t), NEG_BIG, F32)
    m2 = m1
    for j in range(G // t - 1):
        skip = jnp.where(j < jd, 0.0, NEG_BIG)
        m1 = jnp.maximum(m1, fold8t(sl1[j * t:(j + 1) * t], jnp.max) + skip)
        m2 = jnp.maximum(m2, fold8t(sl2[j * t:(j + 1) * t], jnp.max) + skip)
    d0 = pl.multiple_of(qs, t)
    sd1 = s1_ref[pl.ds(d0, t), :] + diag_ref[...]
    sd2 = s2_ref[pl.ds(d0, t), :] + diag_ref[...]
    s1_ref[pl.ds(d0, t), :] = sd1
    s2_ref[pl.ds(d0, t), :] = sd2
    for j in range(1, G // t):
        @pl.when(j > jd)
        def _():
            j0 = pl.multiple_of(l0 + j * t, t)
            s1_ref[pl.ds(j0, t), :] = jnp.full((t, t), NEG_BIG, F32)
            s2_ref[pl.ds(j0, t), :] = jnp.full((t, t), NEG_BIG, F32)
    m1 = jnp.maximum(m1, fold8t(sd1, jnp.max)) + offset(gl)
    m2 = jnp.maximum(m2, fold8t(sd2, jnp.max)) + offset(gl)
    m1, m2 = lax.fori_loop(0, gl, score_body, (m1, m2))
    m1 = jnp.max(m1, axis=0, keepdims=True)
    m2 = jnp.max(m2, axis=0, keepdims=True)

    acc1_ref[...] = jnp.zeros_like(acc1_ref)
    acc2_ref[...] = jnp.zeros_like(acc2_ref)

    def pv_body(g, carry):
        l1, l2 = carry
        k0 = pl.multiple_of(g * G, G)
        vt = v_ref[pl.ds(k0, G), :]
        p1 = jnp.exp2(s1_ref[pl.ds(k0, G), :] - (m1 - offset(g)))
        p2 = jnp.exp2(s2_ref[pl.ds(k0, G), :] - (m2 - offset(g)))
        acc1_ref[...] += _dot_tn(vt, p1.astype(BF16))
        acc2_ref[...] += _dot_tn(vt, p2.astype(BF16))
        return l1 + fold8(p1, jnp.sum), l2 + fold8(p2, jnp.sum)

    zero8 = jnp.zeros((8, t), F32)
    l1, l2 = lax.fori_loop(0, gl + 1, pv_body, (zero8, zero8))
    l1 = jnp.sum(l1, axis=0, keepdims=True)
    l2 = jnp.sum(l2, axis=0, keepdims=True)

    lam = (jnp.exp(jnp.sum(lq1_ref[...] * lk1_ref[...], axis=-1, keepdims=True))
           - jnp.exp(jnp.sum(lq2_ref[...] * lk2_ref[...], axis=-1, keepdims=True))
           + lam_init)
    ot = acc1_ref[...] * (1.0 / l1) - acc2_ref[...] * (lam / l2)
    ms = jnp.mean(ot * ot, axis=0, keepdims=True)
    o = (ot * lax.rsqrt(ms + EPS)).T
    o_ref[...] = (o * g_ref[...] * (1.0 - lam_init)).astype(BF16)


def _att(u, lq1, lk1, lq2, lk2, g_subln, lam_init, batch, seq):
    m = u.shape[0]
    t = _tile(seq, ATT_TQ)
    G = _tile(seq, ATT_GROUP)
    nq = seq // t
    e = 2 * A_DH
    assert G % t == 0
    vec = pl.BlockSpec((1, A_DH), lambda b, h, i: (0, 0))
    return pl.pallas_call(
        functools.partial(_att_kernel, lam_init=lam_init, t=t, G=G),
        grid=(batch, A_HEADS, nq),
        in_specs=[vec, vec, vec, vec,
                  pl.BlockSpec((1, e), lambda b, h, i: (0, 0)),
                  pl.BlockSpec((t, e), lambda b, h, i: (b * nq + i, QA0 // e + h)),
                  pl.BlockSpec((seq, e), lambda b, h, i: (b, KA0 // e + h)),
                  pl.BlockSpec((seq, e), lambda b, h, i: (b, VA0 // e + h))],
        out_specs=pl.BlockSpec((t, e), lambda b, h, i: (b * nq + i, h)),
        out_shape=jax.ShapeDtypeStruct((m, A_WIDTH), BF16),
        scratch_shapes=[pltpu.VMEM((seq, t), F32), pltpu.VMEM((seq, t), F32),
                        pltpu.VMEM((e, t), F32), pltpu.VMEM((e, t), F32),
                        pltpu.VMEM((G, t), F32), pltpu.VMEM((t, t), F32)],
        compiler_params=_cparams("arbitrary", "arbitrary", "arbitrary"), name="diff_attn",
    )(lq1.reshape(1, A_DH), lk1.reshape(1, A_DH), lq2.reshape(1, A_DH), lk2.reshape(1, A_DH),
      g_subln.reshape(1, e), u, u, u)


def _gla_kernel(z_ref, wg_ref, bg_ref, gn_ref, q_ref, k_ref, v_ref, gb_ref, o_ref, *, seq):
    L = CHUNK
    wg = wg_ref[...]
    bg = bg_ref[...]
    gn = gn_ref[...]
    rows = lax.broadcasted_iota(jnp.int32, (L, B_DK), 0)
    ti = lax.broadcasted_iota(jnp.int32, (L, L), 0)
    si = lax.broadcasted_iota(jnp.int32, (L, L), 1)
    SUB = GLA_SUB
    assert L == 4 * SUB
    sshift = SUB.bit_length() - 1
    bi = ti >> sshift
    bj = si >> sshift
    sd = si - (ti - (ti & (SUB - 1)))
    m_diag = (bi == bj) & (si <= ti)
    m_near = (bi == bj + 1) & ((bi & 1) == 1)
    m_far = (bi >= 2) & (bj <= 1)
    qscale = B_DK ** -0.5

    def chunk(c, state_t):
        r0 = pl.multiple_of(c * L, L)
        x = _dot(z_ref[pl.ds(r0, L), :].astype(BF16), wg) + bg
        la = (jnp.minimum(x, 0.0) - jnp.log(1.0 + jnp.exp(-jnp.abs(x)))) * (1.0 / GATE_TAU)
        b = la
        sh = 1
        while sh < L:
            b = b + jnp.where(rows >= sh, pltpu.roll(b, sh, axis=0), 0.0)
            sh *= 2
        b = b * LOG2E
        b_last = b[L - 1:L, :]
        qf = q_ref[pl.ds(r0, L), :].astype(F32) * qscale
        kc = k_ref[pl.ds(r0, L), :]
        kf = kc.astype(F32)
        vc = v_ref[pl.ds(r0, L), :]

        def rescaled(ref):
            qr = (qf * jnp.exp2(jnp.minimum(b - ref, 0.0))).astype(BF16)
            kr = (kf * jnp.exp2(jnp.minimum(ref - b, 0.0))).astype(BF16)
            return _dot_nt(qr, kr)

        half = L // 2
        a_far = rescaled(b[half - 1:half, :])
        ref_near = jnp.where(rows < half, b[SUB - 1:SUB, :], b[half + SUB - 1:half + SUB, :])
        a_near = rescaled(ref_near)
        a = jnp.zeros((L, L), F32)
        for s in range(SUB):
            ref = jnp.concatenate(
                [jnp.broadcast_to(b[i * SUB + s:i * SUB + s + 1, :], (SUB, B_DK)) for i in range(L // SUB)],
                axis=0)
            e = jnp.exp2(jnp.minimum(b - ref, 0.0))
            rs = _dot_nt((qf * e).astype(BF16), kc)
            a = jnp.where(sd == s, rs, a)
        a = jnp.where(m_far, a_far, jnp.where(m_near, a_near, jnp.where(m_diag, a, 0.0)))

        qe = (qf * jnp.exp2(b)).astype(BF16)
        o = _dot(a.astype(BF16), vc) + _dot_nt(qe, state_t.astype(BF16))
        ke = (kf * jnp.exp2(b_last - b)).astype(BF16)
        new_state = jnp.exp2(b_last) * state_t + _dot_tn(vc, ke)

        gate = gb_ref[pl.ds(r0, L), :].astype(F32)
        y = _rms(o, gn) * (gate / (1.0 + jnp.exp(-gate)))
        o_ref[pl.ds(r0, L), :] = y.astype(BF16)
        return new_state

    nc = seq // L
    unroll = 8 if nc % 8 == 0 else 1

    def chunks(c, state_t):
        for k in range(unroll):
            state_t = chunk(c * unroll + k, state_t)
        return state_t

    lax.fori_loop(0, nc // unroll, chunks, jnp.zeros((B_DV, B_DK), F32))


def _gla(u, z, w_gate, b_gate, g_gla, batch, seq):
    m = u.shape[0]
    return pl.pallas_call(
        functools.partial(_gla_kernel, seq=seq),
        grid=(batch, B_HEADS),
        in_specs=[pl.BlockSpec((seq, LANES), lambda b, h: (b, 0)),
                  pl.BlockSpec((LANES, B_DK), lambda b, h: (0, h)),
                  pl.BlockSpec((1, B_DK), lambda b, h: (0, h)),
                  pl.BlockSpec((1, B_DV), lambda b, h: (0, h)),
                  pl.BlockSpec((seq, B_DK), lambda b, h: (b, QB0 // B_DK + h)),
                  pl.BlockSpec((seq, B_DK), lambda b, h: (b, KB0 // B_DK + h)),
                  pl.BlockSpec((seq, B_DV), lambda b, h: (b, VB0 // B_DV + h)),
                  pl.BlockSpec((seq, B_DV), lambda b, h: (b, GB0 // B_DV + h))],
        out_specs=pl.BlockSpec((seq, B_DV), lambda b, h: (b, h)),
        out_shape=jax.ShapeDtypeStruct((m, B_WIDTH), BF16),
        compiler_params=_cparams("parallel", "parallel"), name="gla",
    )(z, w_gate, b_gate.reshape(1, -1), g_gla.reshape(1, -1), u, u, u, u)


def _pool_kernel(u_ref, w_ref, sc_ref, o_ref, carry_ref, *, tr):
    ti = pl.program_id(1)

    @pl.when(ti == 0)
    def _():
        carry_ref[...] = jnp.zeros_like(carry_ref)

    x = u_ref[...].astype(F32)
    ext = jnp.concatenate([carry_ref[...], x], axis=0)
    carry_ref[...] = x[tr - POOL_HALO:, :]
    pos = ti * tr + lax.broadcasted_iota(jnp.int32, (tr, 1), 0)
    for g, w in enumerate(POOL_WINDOWS):
        lo, hi = g * POOL_CH, (g + 1) * POOL_CH
        s = ext[:, lo:hi]
        sh = 1
        while sh < w:
            s = s + pltpu.roll(s, sh, axis=0)
            sh *= 2
        cnt = jnp.minimum(pos + 1, w).astype(F32)
        d = s[POOL_HALO:, :] / cnt - x[:, lo:hi]
        y = _dot(d.astype(BF16), w_ref[g]) * sc_ref[:, lo:hi]
        o_ref[:, lo:hi] = y.astype(BF16)


def _pool(uc, w_pool, pool_scale, batch, seq):
    m = uc.shape[0]
    tr = _tile(seq, 512)
    nt = seq // tr
    assert tr >= POOL_HALO and max(POOL_WINDOWS) <= POOL_HALO
    return pl.pallas_call(
        functools.partial(_pool_kernel, tr=tr),
        grid=(batch, nt),
        in_specs=[pl.BlockSpec((tr, C_WIDTH), lambda b, i: (b * nt + i, 0)),
                  pl.BlockSpec((N_POOL, POOL_CH, POOL_CH), lambda b, i: (0, 0, 0)),
                  pl.BlockSpec((1, C_WIDTH), lambda b, i: (0, 0))],
        out_specs=pl.BlockSpec((tr, C_WIDTH), lambda b, i: (b * nt + i, 0)),
        out_shape=jax.ShapeDtypeStruct((m, C_WIDTH), BF16),
        scratch_shapes=[pltpu.VMEM((POOL_HALO, C_WIDTH), F32)],
        compiler_params=_cparams("parallel", "arbitrary"), name="pool",
    )(uc, w_pool, pool_scale.reshape(1, C_WIDTH))


def _mm_out_kernel(a_ref, b_ref, c_ref, wa_ref, wb_ref, wc_ref, o_ref):
    o_ref[...] = (_dot(a_ref[...], wa_ref[...].astype(BF16))
                  + _dot(b_ref[...], wb_ref[...].astype(BF16))
                  + _dot(c_ref[...], wc_ref[...].astype(BF16))).astype(o_ref.dtype)


def _mm_out(oa, ob, oc, w_o, layer):
    m = oa.shape[0]
    n = w_o.shape[2]
    tm = _tile(m, 1024)
    tn = _tile(n, 512)
    assert A_WIDTH % B_WIDTH == 0 and B_WIDTH == C_WIDTH
    return pl.pallas_call(
        _mm_out_kernel, grid=(m // tm, n // tn),
        in_specs=[pl.BlockSpec((tm, A_WIDTH), lambda i, j: (i, 0)),
                  pl.BlockSpec((tm, B_WIDTH), lambda i, j: (i, 0)),
                  pl.BlockSpec((tm, C_WIDTH), lambda i, j: (i, 0)),
                  pl.BlockSpec((None, A_WIDTH, tn), lambda i, j: (layer, 0, j)),
                  pl.BlockSpec((None, B_WIDTH, tn), lambda i, j: (layer, A_WIDTH // B_WIDTH, j)),
                  pl.BlockSpec((None, C_WIDTH, tn), lambda i, j: (layer, A_WIDTH // C_WIDTH + 1, j))],
        out_specs=pl.BlockSpec((tm, tn), lambda i, j: (i, j)),
        out_shape=jax.ShapeDtypeStruct((m, n), BRANCH_DTYPE),
        compiler_params=_cparams("parallel", "parallel"), name="mm_out",
    )(oa, ob, oc, w_o, w_o, w_o)


GELU_C = math.sqrt(2.0 / math.pi)


def _mm_up_kernel(x_ref, wg_ref, wv_ref, cg_ref, cv_ref, bg_ref, bv_ref, o_ref,
                  tail_ref, yg_ref, yv_ref, *, tm, nj, tiles_per_seq):
    i = pl.program_id(0)
    j = pl.program_id(1)
    seq_start = (i % tiles_per_seq) == 0
    pad = 8

    @pl.when(seq_start)
    def _():
        tail_ref[j] = jnp.zeros((pad, tail_ref.shape[2]), F32)
        tail_ref[nj + j] = jnp.zeros((pad, tail_ref.shape[2]), F32)

    rs0 = min(tm, UP_ROWS)
    pieces = [(r0, rs0) for r0 in range(0, tm, rs0)]
    npieces = len(pieces)
    wg = wg_ref[...].astype(BF16)
    wv = wv_ref[...].astype(BF16)
    yg_ref[0:pad, :] = tail_ref[j]
    yv_ref[0:pad, :] = tail_ref[nj + j]
    cg = cg_ref[...]
    cv = cv_ref[...] * 0.5
    bg = bg_ref[...]
    bv = bv_ref[...] * 0.5

    def matmuls(r, zero):
        r0, rs = pieces[r]
        x = x_ref[r0:r0 + rs, :]
        if zero is not None:
            xb = pltpu.bitcast(x, jnp.uint32)
            x = pltpu.bitcast(xb | jnp.tile(zero, (xb.shape[0] // 8, xb.shape[1] // LANES)), BF16)
        yg_ref[pad + r0:pad + r0 + rs, :] = _dot(x, wg)
        yv_ref[pad + r0:pad + r0 + rs, :] = _dot(x, wv)

    def conv(y_ref, cw, b, r0, rs):
        return (b + cw[2:3, :] * y_ref[pad + r0:pad + r0 + rs, :]
                + cw[1:2, :] * y_ref[pl.ds(pad + r0 - 1, rs), :]
                + cw[0:1, :] * y_ref[pl.ds(pad + r0 - 2, rs), :])

    def epilogue(r):
        r0, rs = pieces[r]
        zg = conv(yg_ref, cg, bg, r0, rs)
        zv = conv(yv_ref, cv, bv, r0, rs)
        t = jnp.tanh(zg * (GELU_C + (GELU_C * 0.044715) * (zg * zg)))
        out = ((zg + zg * t) * zv).astype(BF16)
        o_ref[r0:r0 + rs, :] = out
        bits = pltpu.bitcast(out, jnp.uint32)
        acc = bits[0:8, :]
        for k in range(1, bits.shape[0] // 8):
            acc = acc | bits[8 * k:8 * k + 8, :]
        red = acc[:, 0:LANES]
        for k in range(1, acc.shape[1] // LANES):
            red = red | acc[:, k * LANES:(k + 1) * LANES]
        return (red >> 16) >> 16

    lag = UP_LAG
    for r in range(min(lag, npieces)):
        matmuls(r, None)
    for r in range(npieces):
        zero = epilogue(r)
        if r + lag < npieces:
            matmuls(r + lag, zero)
    tail_ref[j] = yg_ref[tm:tm + pad, :]
    tail_ref[nj + j] = yv_ref[tm:tm + pad, :]


def _mm_up(hn, w_up, w_conv, b_conv, layer, seq):
    m, d = hn.shape
    f = w_up.shape[2] // 2
    tm = _tile(seq, UP_TM)
    tn = MXU_COLS if f % MXU_COLS == 0 else f
    nj = f // tn
    b3 = b_conv.reshape(b_conv.shape[0], 1, 2 * f)
    return pl.pallas_call(
        functools.partial(_mm_up_kernel, tm=tm, nj=nj, tiles_per_seq=seq // tm),
        grid=(m // tm, nj),
        in_specs=[pl.BlockSpec((tm, d), lambda i, j: (i, 0), pipeline_mode=pl.Buffered(1)),
                  pl.BlockSpec((None, d, tn), lambda i, j: (layer, 0, j)),
                  pl.BlockSpec((None, d, tn), lambda i, j: (layer, 0, nj + j)),
                  pl.BlockSpec((None, CONV_K, tn), lambda i, j: (layer, 0, j)),
                  pl.BlockSpec((None, CONV_K, tn), lambda i, j: (layer, 0, nj + j)),
                  pl.BlockSpec((None, 1, tn), lambda i, j: (layer, 0, j)),
                  pl.BlockSpec((None, 1, tn), lambda i, j: (layer, 0, nj + j))],
        out_specs=pl.BlockSpec((tm, tn), lambda i, j: (i, j)),
        out_shape=jax.ShapeDtypeStruct((m, f), BF16),
        scratch_shapes=[pltpu.VMEM((2 * nj, 8, tn), F32),
                        pltpu.VMEM((tm + 8, tn), F32), pltpu.VMEM((tm + 8, tn), F32)],
        compiler_params=_cparams("arbitrary", "arbitrary"), name="mm_up",
    )(hn, w_up, w_up, w_conv, w_conv, b3, b3)


def _mm_kernel(x_ref, w_ref, o_ref):
    o_ref[...] = _dot(x_ref[...], w_ref[...].astype(BF16)).astype(o_ref.dtype)


def _mm_down(x, w, layer):
    m, k = x.shape
    n = w.shape[2]
    tm = _tile(m, 1024)
    tn = _tile(n, MXU_COLS)
    return pl.pallas_call(
        _mm_kernel, grid=(m // tm, n // tn),
        in_specs=[pl.BlockSpec((tm, k), lambda i, j: (i, 0), pipeline_mode=pl.Buffered(1)),
                  pl.BlockSpec((None, k, tn), lambda i, j: (layer, 0, j))],
        out_specs=pl.BlockSpec((tm, tn), lambda i, j: (i, j)),
        out_shape=jax.ShapeDtypeStruct((m, n), BRANCH_DTYPE),
        compiler_params=_cparams("parallel", "parallel"), name="mm_down",
    )(x, w)


def kernel(x, w_in, w_gate_lr2, b_gate, lam_q1, lam_k1, lam_q2, lam_k2, g_subln, g_gla, w_pool, pool_scale, w_o, w_up, w_conv, b_conv, w_down, g_pre_mix, g_post_mix, g_pre_ffn, g_post_ffn):
    batch, seq, d = x.shape
    depth = w_in.shape[0]
    m = batch * seq
    assert seq % CHUNK == 0 and w_in.shape[2] == U_COLS + GATE_RANK
    assert UC0 == GB0 + B_WIDTH

    w_in_t = jnp.swapaxes(w_in, 1, 2)
    w_pool_b = w_pool.astype(BF16)

    h = x.reshape(m, d)
    branch = None
    for l in range(depth):
        w_gate = jnp.pad(w_gate_lr2[l], ((0, LANES - GATE_RANK), (0, 0))).astype(BF16)

        if l == 0:
            hn = _norm_first(h, g_pre_mix[0])
        else:
            h, hn = _resnorm(h, branch, g_post_ffn[l - 1], g_pre_mix[l])
        u = _mm_in(hn, w_in_t, l)
        z, uc = _mm_tail(hn, w_in_t, l)
        lam_init = 0.8 - 0.6 * math.exp(-0.3 * l)
        oa = _att(u, lam_q1[l], lam_k1[l], lam_q2[l], lam_k2[l], g_subln[l], lam_init, batch, seq)
        ob = _gla(u, z, w_gate, b_gate[l], g_gla[l], batch, seq)
        oc = _pool(uc, w_pool_b[l], pool_scale[l], batch, seq)
        mix = _mm_out(oa, ob, oc, w_o, l)

        h, hn = _resnorm(h, mix, g_post_mix[l], g_pre_ffn[l])
        f_in = _mm_up(hn, w_up, w_conv, b_conv, l, seq)
        branch = _mm_down(f_in, w_down, l)
    h = _resnorm_last(h, branch, g_post_ffn[depth - 1])
    return h.reshape(batch, seq, d)
```

```python
import functools
import math

import jax
import jax.numpy as jnp
from jax import lax
from jax.experimental import pallas as pl
from jax.experimental.pallas import tpu as pltpu

F32 = jnp.float32
BF16 = jnp.bfloat16
BRANCH_DTYPE = jnp.bfloat16

CHUNK = 64
EPS = 1e-6
A_HEADS = 8
A_DH = 128
A_WIDTH = A_HEADS * 2 * A_DH
B_HEADS = 4
B_DK = 128
B_DV = 256
B_WIDTH = B_HEADS * B_DV
GATE_RANK = 16
GATE_TAU = 16.0
POOL_WINDOWS = (2, 4, 8, 16)
N_POOL = len(POOL_WINDOWS)
POOL_CH = 256
C_WIDTH = N_POOL * POOL_CH
CONV_K = 3

QA0 = 0
KA0 = A_WIDTH
VA0 = 2 * A_WIDTH
QB0 = 3 * A_WIDTH
KB0 = QB0 + B_HEADS * B_DK
VB0 = KB0 + B_HEADS * B_DK
GB0 = VB0 + B_WIDTH
UC0 = GB0 + B_WIDTH
U_COLS = UC0 + C_WIDTH

LANES = 128
MXU_COLS = 256
VMEM_LIMIT = 56 * 1024 * 1024
POOL_HALO = 16
NEG_BIG = -1e30
LOG2E = math.log2(math.e)
NORM_ROWS = 256
ATT_TQ = 512
ATT_GROUP = 1024
UP_TM = 2048
UP_ROWS = 256
UP_LAG = 2
GLA_SUB = 16
TAIL_ROWS = 1152


def _cparams(*sem):
    return pltpu.CompilerParams(dimension_semantics=sem, vmem_limit_bytes=VMEM_LIMIT)


def _dot(a, b):
    return jnp.dot(a, b, preferred_element_type=F32)


def _dot_nt(a, b):
    return lax.dot_general(a, b, (((1,), (1,)), ((), ())), preferred_element_type=F32)


def _dot_tn(a, b):
    return lax.dot_general(a, b, (((0,), (0,)), ((), ())), preferred_element_type=F32)


def _rms(x, g):
    ms = jnp.mean(x * x, axis=-1, keepdims=True)
    return x * lax.rsqrt(ms + EPS) * g


def _tile(n, want):
    t = min(n, want)
    assert n % t == 0, (n, want)
    return t


def _norm_first_kernel(h_ref, gpre_ref, hn_ref):
    hn_ref[...] = _rms(h_ref[...], gpre_ref[...]).astype(BF16)


def _resnorm_kernel(h_ref, br_ref, gpost_ref, gpre_ref, hout_ref, hn_ref):
    h = h_ref[...] + _rms(br_ref[...].astype(F32), gpost_ref[...])
    hout_ref[...] = h
    hn_ref[...] = _rms(h, gpre_ref[...]).astype(BF16)


def _resnorm_last_kernel(h_ref, br_ref, gpost_ref, hout_ref):
    hout_ref[...] = h_ref[...] + _rms(br_ref[...].astype(F32), gpost_ref[...])


def _norm_first(h, gpre):
    m, d = h.shape
    tr = _tile(m, NORM_ROWS)
    row = pl.BlockSpec((tr, d), lambda i: (i, 0))
    vec = pl.BlockSpec((1, d), lambda i: (0, 0))
    return pl.pallas_call(
        _norm_first_kernel, grid=(m // tr,), in_specs=[row, vec], out_specs=row,
        out_shape=jax.ShapeDtypeStruct((m, d), BF16),
        compiler_params=_cparams("parallel"), name="norm_first",
    )(h, gpre.reshape(1, d))


def _resnorm(h, br, gpost, gpre):
    m, d = h.shape
    tr = _tile(m, NORM_ROWS)
    row = pl.BlockSpec((tr, d), lambda i: (i, 0))
    vec = pl.BlockSpec((1, d), lambda i: (0, 0))
    return pl.pallas_call(
        _resnorm_kernel, grid=(m // tr,), in_specs=[row, row, vec, vec], out_specs=[row, row],
        out_shape=[jax.ShapeDtypeStruct((m, d), F32), jax.ShapeDtypeStruct((m, d), BF16)],
        compiler_params=_cparams("parallel"), name="resnorm",
    )(h, br, gpost.reshape(1, d), gpre.reshape(1, d))


def _resnorm_last(h, br, gpost):
    m, d = h.shape
    tr = _tile(m, NORM_ROWS)
    row = pl.BlockSpec((tr, d), lambda i: (i, 0))
    vec = pl.BlockSpec((1, d), lambda i: (0, 0))
    return pl.pallas_call(
        _resnorm_last_kernel, grid=(m // tr,), in_specs=[row, row, vec], out_specs=row,
        out_shape=jax.ShapeDtypeStruct((m, d), F32),
        compiler_params=_cparams("parallel"), name="resnorm_last",
    )(h, br, gpost.reshape(1, d))


def _mm_in_kernel(x_ref, w_ref, u_ref):
    u_ref[...] = _dot_nt(x_ref[...], w_ref[...].astype(BF16)).astype(BF16)


def _mm_in(hn, w_in_t, layer):
    m, d = hn.shape
    tm = _tile(m, 1024)
    tn = 512
    assert UC0 % tn == 0
    return pl.pallas_call(
        _mm_in_kernel, grid=(m // tm, UC0 // tn),
        in_specs=[pl.BlockSpec((tm, d), lambda i, j: (i, 0)),
                  pl.BlockSpec((None, tn, d), lambda i, j: (layer, j, 0))],
        out_specs=pl.BlockSpec((tm, tn), lambda i, j: (i, j)),
        out_shape=jax.ShapeDtypeStruct((m, UC0), BF16),
        compiler_params=_cparams("parallel", "arbitrary"), name="mm_in",
    )(hn, w_in_t)


def _mm_tail_kernel(x_ref, w_ref, z_ref, uc_ref):
    x = x_ref[...]
    z_ref[...] = _dot_nt(x, w_ref[0:LANES, :].astype(BF16))
    uc_ref[...] = _dot_nt(x, w_ref[GATE_RANK:GATE_RANK + C_WIDTH, :].astype(BF16)).astype(BF16)


def _mm_tail(hn, w_in_t, layer):
    m, d = hn.shape
    tm = _tile(m, 1024)
    rows = TAIL_ROWS
    assert UC0 % rows == 0 and rows >= GATE_RANK + C_WIDTH and GATE_RANK % 8 == 0
    return pl.pallas_call(
        _mm_tail_kernel, grid=(m // tm,),
        in_specs=[pl.BlockSpec((tm, d), lambda i: (i, 0)),
                  pl.BlockSpec((None, rows, d), lambda i: (layer, UC0 // rows, 0),
                               pipeline_mode=pl.Buffered(1))],
        out_specs=[pl.BlockSpec((tm, LANES), lambda i: (i, 0)),
                   pl.BlockSpec((tm, C_WIDTH), lambda i: (i, 0))],
        out_shape=[jax.ShapeDtypeStruct((m, LANES), F32), jax.ShapeDtypeStruct((m, C_WIDTH), BF16)],
        compiler_params=_cparams("arbitrary"), name="mm_tail",
    )(hn, w_in_t)


def _att_kernel(lq1_ref, lk1_ref, lq2_ref, lk2_ref, g_ref, q_ref, k_ref, v_ref, o_ref,
                s1_ref, s2_ref, acc1_ref, acc2_ref, bias_ref, *, lam_init, t, G):
    head = pl.program_id(1)
    qi = pl.program_id(2)
    qscale = LOG2E / math.sqrt(A_DH)
    slope = lax.bitcast_convert_type(
        jnp.full((1, 1), (126 - head) << 23, jnp.int32), F32) * LOG2E

    q = q_ref[...].astype(F32) * qscale
    qt1 = q[:, :A_DH].T.astype(BF16)
    qt2 = q[:, A_DH:].T.astype(BF16)

    def fold8(x, op):
        return op(x.reshape(G // 8, 8, t), axis=0)

    nt = G // t

    @pl.when(qi == 0)
    def _():
        cshift = CHUNK.bit_length() - 1
        kd = lax.broadcasted_iota(jnp.int32, (t, t), 0)
        rd = lax.broadcasted_iota(jnp.int32, (t, t), 1)
        corr = slope * (rd - jnp.abs(rd - kd) - kd).astype(F32)
        diag = jnp.where((kd >> cshift) <= (rd >> cshift), corr, NEG_BIG)
        for i in range(nt):
            base = slope * (kd + i * t).astype(F32)
            bias_ref[0, i * t:(i + 1) * t, :] = base
            for j in range(nt):
                val = base if i < j else (base + diag if i == j else jnp.full((t, t), NEG_BIG, F32))
                bias_ref[1 + j, i * t:(i + 1) * t, :] = val

    qs = qi * t
    gl = qs // G
    jd = (qs - gl * G) // t

    def offset(g):
        return slope * (g * G - qs).astype(F32)

    def score_body(g, carry):
        m1, m2 = carry
        k0 = pl.multiple_of(g * G, G)
        kt = k_ref[pl.ds(k0, G), :]
        bias = bias_ref[jnp.where(g < gl, 0, 1 + jd)]
        s1 = _dot(kt[:, :A_DH], qt1) + bias
        s2 = _dot(kt[:, A_DH:], qt2) + bias
        s1_ref[pl.ds(k0, G), :] = s1
        s2_ref[pl.ds(k0, G), :] = s2
        return (jnp.maximum(m1, fold8(s1, jnp.max) + offset(g)),
                jnp.maximum(m2, fold8(s2, jnp.max) + offset(g)))

    neg8 = jnp.full((8, t), NEG_BIG, F32)
    m1, m2 = lax.fori_loop(0, gl + 1, score_body, (neg8, neg8))
    m1 = jnp.max(m1, axis=0, keepdims=True)
    m2 = jnp.max(m2, axis=0, keepdims=True)

    acc1_ref[...] = jnp.zeros_like(acc1_ref)
    acc2_ref[...] = jnp.zeros_like(acc2_ref)

    def pv_body(g, carry):
        l1, l2 = carry
        k0 = pl.multiple_of(g * G, G)
        vt = v_ref[pl.ds(k0, G), :]
        p1 = jnp.exp2(s1_ref[pl.ds(k0, G), :] - (m1 - offset(g)))
        p2 = jnp.exp2(s2_ref[pl.ds(k0, G), :] - (m2 - offset(g)))
        acc1_ref[...] += _dot_tn(vt, p1.astype(BF16))
        acc2_ref[...] += _dot_tn(vt, p2.astype(BF16))
        return l1 + fold8(p1, jnp.sum), l2 + fold8(p2, jnp.sum)

    zero8 = jnp.zeros((8, t), F32)
    l1, l2 = lax.fori_loop(0, gl + 1, pv_body, (zero8, zero8))
    l1 = jnp.sum(l1, axis=0, keepdims=True)
    l2 = jnp.sum(l2, axis=0, keepdims=True)

    lam = (jnp.exp(jnp.sum(lq1_ref[...] * lk1_ref[...], axis=-1, keepdims=True))
           - jnp.exp(jnp.sum(lq2_ref[...] * lk2_ref[...], axis=-1, keepdims=True))
           + lam_init)
    ot = acc1_ref[...] * (1.0 / l1) - acc2_ref[...] * (lam / l2)
    ms = jnp.mean(ot * ot, axis=0, keepdims=True)
    o = (ot * lax.rsqrt(ms + EPS)).T
    o_ref[...] = (o * g_ref[...] * (1.0 - lam_init)).astype(BF16)


def _att(u, lq1, lk1, lq2, lk2, g_subln, lam_init, batch, seq):
    m = u.shape[0]
    t = _tile(seq, ATT_TQ)
    G = _tile(seq, ATT_GROUP)
    nq = seq // t
    e = 2 * A_DH
    assert G % t == 0
    vec = pl.BlockSpec((1, A_DH), lambda b, h, i: (0, 0))
    return pl.pallas_call(
        functools.partial(_att_kernel, lam_init=lam_init, t=t, G=G),
        grid=(batch, A_HEADS, nq),
        in_specs=[vec, vec, vec, vec,
                  pl.BlockSpec((1, e), lambda b, h, i: (0, 0)),
                  pl.BlockSpec((t, e), lambda b, h, i: (b * nq + i, QA0 // e + h)),
                  pl.BlockSpec((seq, e), lambda b, h, i: (b, KA0 // e + h)),
                  pl.BlockSpec((seq, e), lambda b, h, i: (b, VA0 // e + h))],
        out_specs=pl.BlockSpec((t, e), lambda b, h, i: (b * nq + i, h)),
        out_shape=jax.ShapeDtypeStruct((m, A_WIDTH), BF16),
        scratch_shapes=[pltpu.VMEM((seq, t), F32), pltpu.VMEM((seq, t), F32),
                        pltpu.VMEM((e, t), F32), pltpu.VMEM((e, t), F32),
                        pltpu.VMEM((1 + G // t, G, t), F32)],
        compiler_params=_cparams("arbitrary", "arbitrary", "arbitrary"), name="diff_attn",
    )(lq1.reshape(1, A_DH), lk1.reshape(1, A_DH), lq2.reshape(1, A_DH), lk2.reshape(1, A_DH),
      g_subln.reshape(1, e), u, u, u)


def _gla_kernel(z_ref, wg_ref, bg_ref, gn_ref, q_ref, k_ref, v_ref, gb_ref, o_ref, *, seq):
    L = CHUNK
    wg = wg_ref[...]
    bg = bg_ref[...]
    gn = gn_ref[...]
    rows = lax.broadcasted_iota(jnp.int32, (L, B_DK), 0)
    ti = lax.broadcasted_iota(jnp.int32, (L, L), 0)
    si = lax.broadcasted_iota(jnp.int32, (L, L), 1)
    SUB = GLA_SUB
    assert L == 4 * SUB
    sshift = SUB.bit_length() - 1
    bi = ti >> sshift
    bj = si >> sshift
    sd = si - (ti - (ti & (SUB - 1)))
    m_diag = (bi == bj) & (si <= ti)
    m_near = (bi == bj + 1) & ((bi & 1) == 1)
    m_far = (bi >= 2) & (bj <= 1)
    qscale = B_DK ** -0.5

    def chunk(c, state_t):
        r0 = pl.multiple_of(c * L, L)
        x = _dot(z_ref[pl.ds(r0, L), :].astype(BF16), wg) + bg
        la = (jnp.minimum(x, 0.0) - jnp.log(1.0 + jnp.exp(-jnp.abs(x)))) * (1.0 / GATE_TAU)
        b = la
        sh = 1
        while sh < L:
            b = b + jnp.where(rows >= sh, pltpu.roll(b, sh, axis=0), 0.0)
            sh *= 2
        b = b * LOG2E
        b_last = b[L - 1:L, :]
        qf = q_ref[pl.ds(r0, L), :].astype(F32) * qscale
        kc = k_ref[pl.ds(r0, L), :]
        kf = kc.astype(F32)
        vc = v_ref[pl.ds(r0, L), :]

        def rescaled(ref):
            qr = (qf * jnp.exp2(jnp.minimum(b - ref, 0.0))).astype(BF16)
            kr = (kf * jnp.exp2(jnp.minimum(ref - b, 0.0))).astype(BF16)
            return _dot_nt(qr, kr)

        half = L // 2
        a_far = rescaled(b[half - 1:half, :])
        ref_near = jnp.where(rows < half, b[SUB - 1:SUB, :], b[half + SUB - 1:half + SUB, :])
        a_near = rescaled(ref_near)
        a = jnp.zeros((L, L), F32)
        for s in range(SUB):
            ref = jnp.concatenate(
                [jnp.broadcast_to(b[i * SUB + s:i * SUB + s + 1, :], (SUB, B_DK)) for i in range(L // SUB)],
                axis=0)
            e = jnp.exp2(jnp.minimum(b - ref, 0.0))
            rs = _dot_nt((qf * e).astype(BF16), kc)
            a = jnp.where(sd == s, rs, a)
        a = jnp.where(m_far, a_far, jnp.where(m_near, a_near, jnp.where(m_diag, a, 0.0)))

        qe = (qf * jnp.exp2(b)).astype(BF16)
        o = _dot(a.astype(BF16), vc) + _dot_nt(qe, state_t.astype(BF16))
        ke = (kf * jnp.exp2(b_last - b)).astype(BF16)
        new_state = jnp.exp2(b_last) * state_t + _dot_tn(vc, ke)

        gate = gb_ref[pl.ds(r0, L), :].astype(F32)
        y = _rms(o, gn) * (gate / (1.0 + jnp.exp(-gate)))
        o_ref[pl.ds(r0, L), :] = y.astype(BF16)
        return new_state

    nc = seq // L
    unroll = 8 if nc % 8 == 0 else 1

    def chunks(c, state_t):
        for k in range(unroll):
            state_t = chunk(c * unroll + k, state_t)
        return state_t

    lax.fori_loop(0, nc // unroll, chunks, jnp.zeros((B_DV, B_DK), F32))


def _gla(u, z, w_gate, b_gate, g_gla, batch, seq):
    m = u.shape[0]
    return pl.pallas_call(
        functools.partial(_gla_kernel, seq=seq),
        grid=(batch, B_HEADS),
        in_specs=[pl.BlockSpec((seq, LANES), lambda b, h: (b, 0)),
                  pl.BlockSpec((LANES, B_DK), lambda b, h: (0, h)),
                  pl.BlockSpec((1, B_DK), lambda b, h: (0, h)),
                  pl.BlockSpec((1, B_DV), lambda b, h: (0, h)),
                  pl.BlockSpec((seq, B_DK), lambda b, h: (b, QB0 // B_DK + h)),
                  pl.BlockSpec((seq, B_DK), lambda b, h: (b, KB0 // B_DK + h)),
                  pl.BlockSpec((seq, B_DV), lambda b, h: (b, VB0 // B_DV + h)),
                  pl.BlockSpec((seq, B_DV), lambda b, h: (b, GB0 // B_DV + h))],
        out_specs=pl.BlockSpec((seq, B_DV), lambda b, h: (b, h)),
        out_shape=jax.ShapeDtypeStruct((m, B_WIDTH), BF16),
        compiler_params=_cparams("parallel", "parallel"), name="gla",
    )(z, w_gate, b_gate.reshape(1, -1), g_gla.reshape(1, -1), u, u, u, u)


def _pool_kernel(u_ref, w_ref, sc_ref, o_ref, carry_ref, *, tr):
    ti = pl.program_id(1)

    @pl.when(ti == 0)
    def _():
        carry_ref[...] = jnp.zeros_like(carry_ref)

    x = u_ref[...].astype(F32)
    ext = jnp.concatenate([carry_ref[...], x], axis=0)
    carry_ref[...] = x[tr - POOL_HALO:, :]
    pos = ti * tr + lax.broadcasted_iota(jnp.int32, (tr, 1), 0)
    for g, w in enumerate(POOL_WINDOWS):
        lo, hi = g * POOL_CH, (g + 1) * POOL_CH
        s = ext[:, lo:hi]
        sh = 1
        while sh < w:
            s = s + pltpu.roll(s, sh, axis=0)
            sh *= 2
        cnt = jnp.minimum(pos + 1, w).astype(F32)
        d = s[POOL_HALO:, :] / cnt - x[:, lo:hi]
        y = _dot(d.astype(BF16), w_ref[g]) * sc_ref[:, lo:hi]
        o_ref[:, lo:hi] = y.astype(BF16)


def _pool(uc, w_pool, pool_scale, batch, seq):
    m = uc.shape[0]
    tr = _tile(seq, 512)
    nt = seq // tr
    assert tr >= POOL_HALO and max(POOL_WINDOWS) <= POOL_HALO
    return pl.pallas_call(
        functools.partial(_pool_kernel, tr=tr),
        grid=(batch, nt),
        in_specs=[pl.BlockSpec((tr, C_WIDTH), lambda b, i: (b * nt + i, 0)),
                  pl.BlockSpec((N_POOL, POOL_CH, POOL_CH), lambda b, i: (0, 0, 0)),
                  pl.BlockSpec((1, C_WIDTH), lambda b, i: (0, 0))],
        out_specs=pl.BlockSpec((tr, C_WIDTH), lambda b, i: (b * nt + i, 0)),
        out_shape=jax.ShapeDtypeStruct((m, C_WIDTH), BF16),
        scratch_shapes=[pltpu.VMEM((POOL_HALO, C_WIDTH), F32)],
        compiler_params=_cparams("parallel", "arbitrary"), name="pool",
    )(uc, w_pool, pool_scale.reshape(1, C_WIDTH))


def _mm_out_kernel(a_ref, b_ref, c_ref, wa_ref, wb_ref, wc_ref, o_ref):
    o_ref[...] = (_dot(a_ref[...], wa_ref[...].astype(BF16))
                  + _dot(b_ref[...], wb_ref[...].astype(BF16))
                  + _dot(c_ref[...], wc_ref[...].astype(BF16))).astype(o_ref.dtype)


def _mm_out(oa, ob, oc, w_o, layer):
    m = oa.shape[0]
    n = w_o.shape[2]
    tm = _tile(m, 1024)
    tn = _tile(n, 512)
    assert A_WIDTH % B_WIDTH == 0 and B_WIDTH == C_WIDTH
    return pl.pallas_call(
        _mm_out_kernel, grid=(m // tm, n // tn),
        in_specs=[pl.BlockSpec((tm, A_WIDTH), lambda i, j: (i, 0)),
                  pl.BlockSpec((tm, B_WIDTH), lambda i, j: (i, 0)),
                  pl.BlockSpec((tm, C_WIDTH), lambda i, j: (i, 0)),
                  pl.BlockSpec((None, A_WIDTH, tn), lambda i, j: (layer, 0, j)),
                  pl.BlockSpec((None, B_WIDTH, tn), lambda i, j: (layer, A_WIDTH // B_WIDTH, j)),
                  pl.BlockSpec((None, C_WIDTH, tn), lambda i, j: (layer, A_WIDTH // C_WIDTH + 1, j))],
        out_specs=pl.BlockSpec((tm, tn), lambda i, j: (i, j)),
        out_shape=jax.ShapeDtypeStruct((m, n), BRANCH_DTYPE),
        compiler_params=_cparams("parallel", "parallel"), name="mm_out",
    )(oa, ob, oc, w_o, w_o, w_o)


GELU_C = math.sqrt(2.0 / math.pi)


def _mm_up_kernel(x_ref, wg_ref, wv_ref, cg_ref, cv_ref, bg_ref, bv_ref, o_ref,
                  tail_ref, yg_ref, yv_ref, *, tm, nj, tiles_per_seq):
    i = pl.program_id(0)
    j = pl.program_id(1)
    seq_start = (i % tiles_per_seq) == 0
    pad = 8

    @pl.when(seq_start)
    def _():
        tail_ref[j] = jnp.zeros((pad, tail_ref.shape[2]), F32)
        tail_ref[nj + j] = jnp.zeros((pad, tail_ref.shape[2]), F32)

    rs0 = min(tm, UP_ROWS)
    pieces = [(r0, rs0) for r0 in range(0, tm, rs0)]
    npieces = len(pieces)
    wg = wg_ref[...].astype(BF16)
    wv = wv_ref[...].astype(BF16)
    yg_ref[0:pad, :] = tail_ref[j]
    yv_ref[0:pad, :] = tail_ref[nj + j]
    cg = cg_ref[...]
    cv = cv_ref[...] * 0.5
    bg = bg_ref[...]
    bv = bv_ref[...] * 0.5

    def matmuls(r, zero):
        r0, rs = pieces[r]
        x = x_ref[r0:r0 + rs, :]
        if zero is not None:
            xb = pltpu.bitcast(x, jnp.uint32)
            x = pltpu.bitcast(xb | jnp.tile(zero, (xb.shape[0] // 8, xb.shape[1] // LANES)), BF16)
        yg_ref[pad + r0:pad + r0 + rs, :] = _dot(x, wg)
        yv_ref[pad + r0:pad + r0 + rs, :] = _dot(x, wv)

    def conv(y_ref, cw, b, r0, rs):
        return (b + cw[2:3, :] * y_ref[pad + r0:pad + r0 + rs, :]
                + cw[1:2, :] * y_ref[pl.ds(pad + r0 - 1, rs), :]
                + cw[0:1, :] * y_ref[pl.ds(pad + r0 - 2, rs), :])

    def epilogue(r):
        r0, rs = pieces[r]
        zg = conv(yg_ref, cg, bg, r0, rs)
        zv = conv(yv_ref, cv, bv, r0, rs)
        t = jnp.tanh(zg * (GELU_C + (GELU_C * 0.044715) * (zg * zg)))
        out = ((zg + zg * t) * zv).astype(BF16)
        o_ref[r0:r0 + rs, :] = out
        bits = pltpu.bitcast(out, jnp.uint32)
        acc = bits[0:8, :]
        for k in range(1, bits.shape[0] // 8):
            acc = acc | bits[8 * k:8 * k + 8, :]
        red = acc[:, 0:LANES]
        for k in range(1, acc.shape[1] // LANES):
            red = red | acc[:, k * LANES:(k + 1) * LANES]
        return (red >> 16) >> 16

    lag = UP_LAG
    for r in range(min(lag, npieces)):
        matmuls(r, None)
    for r in range(npieces):
        zero = epilogue(r)
        if r + lag < npieces:
            matmuls(r + lag, zero)
    tail_ref[j] = yg_ref[tm:tm + pad, :]
    tail_ref[nj + j] = yv_ref[tm:tm + pad, :]


def _mm_up(hn, w_up, w_conv, b_conv, layer, seq):
    m, d = hn.shape
    f = w_up.shape[2] // 2
    tm = _tile(seq, UP_TM)
    tn = MXU_COLS if f % MXU_COLS == 0 else f
    nj = f // tn
    b3 = b_conv.reshape(b_conv.shape[0], 1, 2 * f)
    return pl.pallas_call(
        functools.partial(_mm_up_kernel, tm=tm, nj=nj, tiles_per_seq=seq // tm),
        grid=(m // tm, nj),
        in_specs=[pl.BlockSpec((tm, d), lambda i, j: (i, 0), pipeline_mode=pl.Buffered(1)),
                  pl.BlockSpec((None, d, tn), lambda i, j: (layer, 0, j)),
                  pl.BlockSpec((None, d, tn), lambda i, j: (layer, 0, nj + j)),
                  pl.BlockSpec((None, CONV_K, tn), lambda i, j: (layer, 0, j)),
                  pl.BlockSpec((None, CONV_K, tn), lambda i, j: (layer, 0, nj + j)),
                  pl.BlockSpec((None, 1, tn), lambda i, j: (layer, 0, j)),
                  pl.BlockSpec((None, 1, tn), lambda i, j: (layer, 0, nj + j))],
        out_specs=pl.BlockSpec((tm, tn), lambda i, j: (i, j)),
        out_shape=jax.ShapeDtypeStruct((m, f), BF16),
        scratch_shapes=[pltpu.VMEM((2 * nj, 8, tn), F32),
                        pltpu.VMEM((tm + 8, tn), F32), pltpu.VMEM((tm + 8, tn), F32)],
        compiler_params=_cparams("arbitrary", "arbitrary"), name="mm_up",
    )(hn, w_up, w_up, w_conv, w_conv, b3, b3)


def _mm_kernel(x_ref, w_ref, o_ref):
    o_ref[...] = _dot(x_ref[...], w_ref[...].astype(BF16)).astype(o_ref.dtype)


def _mm_down(x, w, layer):
    m, k = x.shape
    n = w.shape[2]
    tm = _tile(m, 1024)
    tn = _tile(n, MXU_COLS)
    return pl.pallas_call(
        _mm_kernel, grid=(m // tm, n // tn),
        in_specs=[pl.BlockSpec((tm, k), lambda i, j: (i, 0), pipeline_mode=pl.Buffered(1)),
                  pl.BlockSpec((None, k, tn), lambda i, j: (layer, 0, j))],
        out_specs=pl.BlockSpec((tm, tn), lambda i, j: (i, j)),
        out_shape=jax.ShapeDtypeStruct((m, n), BRANCH_DTYPE),
        compiler_params=_cparams("parallel", "parallel"), name="mm_down",
    )(x, w)


def kernel(x, w_in, w_gate_lr2, b_gate, lam_q1, lam_k1, lam_q2, lam_k2, g_subln, g_gla, w_pool, pool_scale, w_o, w_up, w_conv, b_conv, w_down, g_pre_mix, g_post_mix, g_pre_ffn, g_post_ffn):
    batch, seq, d = x.shape
    depth = w_in.shape[0]
    m = batch * seq
    assert seq % CHUNK == 0 and w_in.shape[2] == U_COLS + GATE_RANK
    assert UC0 == GB0 + B_WIDTH

    w_in_t = jnp.swapaxes(w_in, 1, 2)
    w_pool_b = w_pool.astype(BF16)

    h = x.reshape(m, d)
    branch = None
    for l in range(depth):
        w_gate = jnp.pad(w_gate_lr2[l], ((0, LANES - GATE_RANK), (0, 0))).astype(BF16)

        if l == 0:
            hn = _norm_first(h, g_pre_mix[0])
        else:
            h, hn = _resnorm(h, branch, g_post_ffn[l - 1], g_pre_mix[l])
        u = _mm_in(hn, w_in_t, l)
        z, uc = _mm_tail(hn, w_in_t, l)
        lam_init = 0.8 - 0.6 * math.exp(-0.3 * l)
        oa = _att(u, lam_q1[l], lam_k1[l], lam_q2[l], lam_k2[l], g_subln[l], lam_init, batch, seq)
        ob = _gla(u, z, w_gate, b_gate[l], g_gla[l], batch, seq)
        oc = _pool(uc, w_pool_b[l], pool_scale[l], batch, seq)
        mix = _mm_out(oa, ob, oc, w_o, l)

        h, hn = _resnorm(h, mix, g_post_mix[l], g_pre_ffn[l])
        f_in = _mm_up(hn, w_up, w_conv, b_conv, l, seq)
        branch = _mm_down(f_in, w_down, l)
    h = _resnorm_last(h, branch, g_post_ffn[depth - 1])
    return h.reshape(batch, seq, d)
```

```python
import functools
import math

import jax
import jax.numpy as jnp
from jax import lax
from jax.experimental import pallas as pl
from jax.experimental.pallas import tpu as pltpu

F32 = jnp.float32
BF16 = jnp.bfloat16
BRANCH_DTYPE = jnp.bfloat16

CHUNK = 64
EPS = 1e-6
A_HEADS = 8
A_DH = 128
A_WIDTH = A_HEADS * 2 * A_DH
B_HEADS = 4
B_DK = 128
B_DV = 256
B_WIDTH = B_HEADS * B_DV
GATE_RANK = 16
GATE_TAU = 16.0
POOL_WINDOWS = (2, 4, 8, 16)
N_POOL = len(POOL_WINDOWS)
POOL_CH = 256
C_WIDTH = N_POOL * POOL_CH
CONV_K = 3

QA0 = 0
KA0 = A_WIDTH
VA0 = 2 * A_WIDTH
QB0 = 3 * A_WIDTH
KB0 = QB0 + B_HEADS * B_DK
VB0 = KB0 + B_HEADS * B_DK
GB0 = VB0 + B_WIDTH
UC0 = GB0 + B_WIDTH
U_COLS = UC0 + C_WIDTH

LANES = 128
MXU_COLS = 256
VMEM_LIMIT = 56 * 1024 * 1024
POOL_HALO = 16
NEG_BIG = -1e30
LOG2E = math.log2(math.e)
NORM_ROWS = 256
ATT_TQ = 512
ATT_GROUP = 1024
UP_TM = 2048
UP_ROWS = 256
UP_LAG = 2
GLA_SUB = 16
TAIL_ROWS = 1152


def _cparams(*sem):
    return pltpu.CompilerParams(dimension_semantics=sem, vmem_limit_bytes=VMEM_LIMIT)


def _dot(a, b):
    return jnp.dot(a, b, preferred_element_type=F32)


def _dot_nt(a, b):
    return lax.dot_general(a, b, (((1,), (1,)), ((), ())), preferred_element_type=F32)


def _dot_tn(a, b):
    return lax.dot_general(a, b, (((0,), (0,)), ((), ())), preferred_element_type=F32)


def _rms(x, g):
    ms = jnp.mean(x * x, axis=-1, keepdims=True)
    return x * lax.rsqrt(ms + EPS) * g


def _tile(n, want):
    t = min(n, want)
    assert n % t == 0, (n, want)
    return t


def _norm_first_kernel(h_ref, gpre_ref, hn_ref):
    hn_ref[...] = _rms(h_ref[...], gpre_ref[...]).astype(BF16)


def _resnorm_kernel(h_ref, br_ref, gpost_ref, gpre_ref, hout_ref, hn_ref):
    h = h_ref[...] + _rms(br_ref[...].astype(F32), gpost_ref[...])
    hout_ref[...] = h
    hn_ref[...] = _rms(h, gpre_ref[...]).astype(BF16)


def _resnorm_last_kernel(h_ref, br_ref, gpost_ref, hout_ref):
    hout_ref[...] = h_ref[...] + _rms(br_ref[...].astype(F32), gpost_ref[...])


def _norm_first(h, gpre):
    m, d = h.shape
    tr = _tile(m, NORM_ROWS)
    row = pl.BlockSpec((tr, d), lambda i: (i, 0))
    vec = pl.BlockSpec((1, d), lambda i: (0, 0))
    return pl.pallas_call(
        _norm_first_kernel, grid=(m // tr,), in_specs=[row, vec], out_specs=row,
        out_shape=jax.ShapeDtypeStruct((m, d), BF16),
        compiler_params=_cparams("parallel"), name="norm_first",
    )(h, gpre.reshape(1, d))


def _resnorm(h, br, gpost, gpre):
    m, d = h.shape
    tr = _tile(m, NORM_ROWS)
    row = pl.BlockSpec((tr, d), lambda i: (i, 0))
    vec = pl.BlockSpec((1, d), lambda i: (0, 0))
    return pl.pallas_call(
        _resnorm_kernel, grid=(m // tr,), in_specs=[row, row, vec, vec], out_specs=[row, row],
        out_shape=[jax.ShapeDtypeStruct((m, d), F32), jax.ShapeDtypeStruct((m, d), BF16)],
        compiler_params=_cparams("parallel"), name="resnorm",
    )(h, br, gpost.reshape(1, d), gpre.reshape(1, d))


def _resnorm_last(h, br, gpost):
    m, d = h.shape
    tr = _tile(m, NORM_ROWS)
    row = pl.BlockSpec((tr, d), lambda i: (i, 0))
    vec = pl.BlockSpec((1, d), lambda i: (0, 0))
    return pl.pallas_call(
        _resnorm_last_kernel, grid=(m // tr,), in_specs=[row, row, vec], out_specs=row,
        out_shape=jax.ShapeDtypeStruct((m, d), F32),
        compiler_params=_cparams("parallel"), name="resnorm_last",
    )(h, br, gpost.reshape(1, d))


def _mm_in_kernel(x_ref, w_ref, u_ref):
    u_ref[...] = _dot_nt(x_ref[...], w_ref[...].astype(BF16)).astype(BF16)


def _mm_in(hn, w_in_t, layer):
    m, d = hn.shape
    tm = _tile(m, 1024)
    tn = 512
    assert UC0 % tn == 0
    return pl.pallas_call(
        _mm_in_kernel, grid=(m // tm, UC0 // tn),
        in_specs=[pl.BlockSpec((tm, d), lambda i, j: (i, 0)),
                  pl.BlockSpec((None, tn, d), lambda i, j: (layer, j, 0))],
        out_specs=pl.BlockSpec((tm, tn), lambda i, j: (i, j)),
        out_shape=jax.ShapeDtypeStruct((m, UC0), BF16),
        compiler_params=_cparams("parallel", "arbitrary"), name="mm_in",
    )(hn, w_in_t)


def _mm_tail_kernel(x_ref, w_ref, z_ref, uc_ref):
    x = x_ref[...]
    z_ref[...] = _dot_nt(x, w_ref[0:LANES, :].astype(BF16))
    uc_ref[...] = _dot_nt(x, w_ref[GATE_RANK:GATE_RANK + C_WIDTH, :].astype(BF16)).astype(BF16)


def _mm_tail(hn, w_in_t, layer):
    m, d = hn.shape
    tm = _tile(m, 1024)
    rows = TAIL_ROWS
    assert UC0 % rows == 0 and rows >= GATE_RANK + C_WIDTH and GATE_RANK % 8 == 0
    return pl.pallas_call(
        _mm_tail_kernel, grid=(m // tm,),
        in_specs=[pl.BlockSpec((tm, d), lambda i: (i, 0)),
                  pl.BlockSpec((None, rows, d), lambda i: (layer, UC0 // rows, 0),
                               pipeline_mode=pl.Buffered(1))],
        out_specs=[pl.BlockSpec((tm, LANES), lambda i: (i, 0)),
                   pl.BlockSpec((tm, C_WIDTH), lambda i: (i, 0))],
        out_shape=[jax.ShapeDtypeStruct((m, LANES), F32), jax.ShapeDtypeStruct((m, C_WIDTH), BF16)],
        compiler_params=_cparams("arbitrary"), name="mm_tail",
    )(hn, w_in_t)


def _att_kernel(lq1_ref, lk1_ref, lq2_ref, lk2_ref, g_ref, q_ref, k_ref, v_ref, o_ref,
                s1_ref, s2_ref, acc1_ref, acc2_ref, bias_ref, stat_ref, *, lam_init, t, G):
    head = pl.program_id(1)
    qi = pl.program_id(2)
    qscale = LOG2E / math.sqrt(A_DH)
    slope = lax.bitcast_convert_type(
        jnp.full((1, 1), (126 - head) << 23, jnp.int32), F32) * LOG2E

    q = q_ref[...].astype(F32) * qscale
    qt1 = q[:, :A_DH].T.astype(BF16)
    qt2 = q[:, A_DH:].T.astype(BF16)

    def fold8(x, op):
        return op(x.reshape(x.shape[0] // 8, 8, t), axis=0)

    nt = G // t

    @pl.when(qi == 0)
    def _():
        cshift = CHUNK.bit_length() - 1
        kd = lax.broadcasted_iota(jnp.int32, (t, t), 0)
        rd = lax.broadcasted_iota(jnp.int32, (t, t), 1)
        corr = slope * (rd - jnp.abs(rd - kd) - kd).astype(F32)
        diag = jnp.where((kd >> cshift) <= (rd >> cshift), corr, NEG_BIG)
        for i in range(nt):
            base = slope * (kd + i * t).astype(F32)
            bias_ref[0, i * t:(i + 1) * t, :] = base
            for j in range(nt):
                val = base if i < j else (base + diag if i == j else jnp.full((t, t), NEG_BIG, F32))
                bias_ref[1 + j, i * t:(i + 1) * t, :] = val

    qs = qi * t
    gl = qs // G
    jd = (qs - gl * G) // t

    def offset(g):
        return slope * (g * G - qs).astype(F32)

    l0 = pl.multiple_of(gl * G, G)

    def scores(k0, rows, bias, off, carry):
        m1, m2 = carry
        kt = k_ref[pl.ds(k0, rows), :]
        s1 = _dot(kt[:, :A_DH], qt1) + bias
        s2 = _dot(kt[:, A_DH:], qt2) + bias
        s1_ref[pl.ds(k0, rows), :] = s1
        s2_ref[pl.ds(k0, rows), :] = s2
        return (jnp.maximum(m1, fold8(s1, jnp.max) + off), jnp.maximum(m2, fold8(s2, jnp.max) + off))

    def score_body(g, carry):
        return scores(pl.multiple_of(g * G, G), G, bias_ref[0], offset(g), carry)

    neg8 = jnp.full((8, t), NEG_BIG, F32)
    m_groups = lax.fori_loop(0, gl, score_body, (neg8, neg8))
    for j in range(nt):
        @pl.when(jd == j)
        def _():
            rows = (j + 1) * t
            m1, m2 = scores(l0, rows, bias_ref[1 + j, 0:rows, :], offset(gl), m_groups)
            stat_ref[0] = m1
            stat_ref[1] = m2
    m1 = jnp.max(stat_ref[0], axis=0, keepdims=True)
    m2 = jnp.max(stat_ref[1], axis=0, keepdims=True)

    acc1_ref[...] = jnp.zeros_like(acc1_ref)
    acc2_ref[...] = jnp.zeros_like(acc2_ref)

    def probs(k0, rows, off, carry):
        l1, l2 = carry
        vt = v_ref[pl.ds(k0, rows), :]
        p1 = jnp.exp2(s1_ref[pl.ds(k0, rows), :] - (m1 - off))
        p2 = jnp.exp2(s2_ref[pl.ds(k0, rows), :] - (m2 - off))
        acc1_ref[...] += _dot_tn(vt, p1.astype(BF16))
        acc2_ref[...] += _dot_tn(vt, p2.astype(BF16))
        return l1 + fold8(p1, jnp.sum), l2 + fold8(p2, jnp.sum)

    def pv_body(g, carry):
        return probs(pl.multiple_of(g * G, G), G, offset(g), carry)

    zero8 = jnp.zeros((8, t), F32)
    l_groups = lax.fori_loop(0, gl, pv_body, (zero8, zero8))
    for j in range(nt):
        @pl.when(jd == j)
        def _():
            l1, l2 = probs(l0, (j + 1) * t, offset(gl), l_groups)
            stat_ref[0] = l1
            stat_ref[1] = l2
    l1 = jnp.sum(stat_ref[0], axis=0, keepdims=True)
    l2 = jnp.sum(stat_ref[1], axis=0, keepdims=True)

    lam = (jnp.exp(jnp.sum(lq1_ref[...] * lk1_ref[...], axis=-1, keepdims=True))
           - jnp.exp(jnp.sum(lq2_ref[...] * lk2_ref[...], axis=-1, keepdims=True))
           + lam_init)
    ot = acc1_ref[...] * (1.0 / l1) - acc2_ref[...] * (lam / l2)
    ms = jnp.mean(ot * ot, axis=0, keepdims=True)
    o = (ot * lax.rsqrt(ms + EPS)).T
    o_ref[...] = (o * g_ref[...] * (1.0 - lam_init)).astype(BF16)


def _att(u, lq1, lk1, lq2, lk2, g_subln, lam_init, batch, seq):
    m = u.shape[0]
    t = _tile(seq, ATT_TQ)
    G = _tile(seq, ATT_GROUP)
    nq = seq // t
    e = 2 * A_DH
    assert G % t == 0
    vec = pl.BlockSpec((1, A_DH), lambda b, h, i: (0, 0))
    return pl.pallas_call(
        functools.partial(_att_kernel, lam_init=lam_init, t=t, G=G),
        grid=(batch, A_HEADS, nq),
        in_specs=[vec, vec, vec, vec,
                  pl.BlockSpec((1, e), lambda b, h, i: (0, 0)),
                  pl.BlockSpec((t, e), lambda b, h, i: (b * nq + i, QA0 // e + h)),
                  pl.BlockSpec((seq, e), lambda b, h, i: (b, KA0 // e + h)),
                  pl.BlockSpec((seq, e), lambda b, h, i: (b, VA0 // e + h))],
        out_specs=pl.BlockSpec((t, e), lambda b, h, i: (b * nq + i, h)),
        out_shape=jax.ShapeDtypeStruct((m, A_WIDTH), BF16),
        scratch_shapes=[pltpu.VMEM((seq, t), F32), pltpu.VMEM((seq, t), F32),
                        pltpu.VMEM((e, t), F32), pltpu.VMEM((e, t), F32),
                        pltpu.VMEM((1 + G // t, G, t), F32), pltpu.VMEM((2, 8, t), F32)],
        compiler_params=_cparams("arbitrary", "arbitrary", "arbitrary"), name="diff_attn",
    )(lq1.reshape(1, A_DH), lk1.reshape(1, A_DH), lq2.reshape(1, A_DH), lk2.reshape(1, A_DH),
      g_subln.reshape(1, e), u, u, u)


def _gla_kernel(z_ref, wg_ref, bg_ref, gn_ref, q_ref, k_ref, v_ref, gb_ref, o_ref, *, seq):
    L = CHUNK
    wg = wg_ref[...]
    bg = bg_ref[...]
    gn = gn_ref[...]
    rows = lax.broadcasted_iota(jnp.int32, (L, B_DK), 0)
    ti = lax.broadcasted_iota(jnp.int32, (L, L), 0)
    si = lax.broadcasted_iota(jnp.int32, (L, L), 1)
    SUB = GLA_SUB
    assert L == 4 * SUB
    sshift = SUB.bit_length() - 1
    bi = ti >> sshift
    bj = si >> sshift
    sd = si - (ti - (ti & (SUB - 1)))
    m_diag = (bi == bj) & (si <= ti)
    m_near = (bi == bj + 1) & ((bi & 1) == 1)
    m_far = (bi >= 2) & (bj <= 1)
    qscale = B_DK ** -0.5

    def chunk(c, state_t):
        r0 = pl.multiple_of(c * L, L)
        x = _dot(z_ref[pl.ds(r0, L), :].astype(BF16), wg) + bg
        la = (jnp.minimum(x, 0.0) - jnp.log(1.0 + jnp.exp(-jnp.abs(x)))) * (1.0 / GATE_TAU)
        b = la
        sh = 1
        while sh < L:
            b = b + jnp.where(rows >= sh, pltpu.roll(b, sh, axis=0), 0.0)
            sh *= 2
        b = b * LOG2E
        b_last = b[L - 1:L, :]
        qf = q_ref[pl.ds(r0, L), :].astype(F32) * qscale
        kc = k_ref[pl.ds(r0, L), :]
        kf = kc.astype(F32)
        vc = v_ref[pl.ds(r0, L), :]

        def rescaled(ref):
            qr = (qf * jnp.exp2(jnp.minimum(b - ref, 0.0))).astype(BF16)
            kr = (kf * jnp.exp2(jnp.minimum(ref - b, 0.0))).astype(BF16)
            return _dot_nt(qr, kr)

        half = L // 2
        a_far = rescaled(b[half - 1:half, :])
        ref_near = jnp.where(rows < half, b[SUB - 1:SUB, :], b[half + SUB - 1:half + SUB, :])
        a_near = rescaled(ref_near)
        a = jnp.zeros((L, L), F32)
        for s in range(SUB):
            ref = jnp.concatenate(
                [jnp.broadcast_to(b[i * SUB + s:i * SUB + s + 1, :], (SUB, B_DK)) for i in range(L // SUB)],
                axis=0)
            e = jnp.exp2(jnp.minimum(b - ref, 0.0))
            rs = _dot_nt((qf * e).astype(BF16), kc)
            a = jnp.where(sd == s, rs, a)
        a = jnp.where(m_far, a_far, jnp.where(m_near, a_near, jnp.where(m_diag, a, 0.0)))

        qe = (qf * jnp.exp2(b)).astype(BF16)
        o = _dot(a.astype(BF16), vc) + _dot_nt(qe, state_t.astype(BF16))
        ke = (kf * jnp.exp2(b_last - b)).astype(BF16)
        new_state = jnp.exp2(b_last) * state_t + _dot_tn(vc, ke)

        gate = gb_ref[pl.ds(r0, L), :].astype(F32)
        y = _rms(o, gn) * (gate / (1.0 + jnp.exp(-gate)))
        o_ref[pl.ds(r0, L), :] = y.astype(BF16)
        return new_state

    nc = seq // L
    unroll = 8 if nc % 8 == 0 else 1

    def chunks(c, state_t):
        for k in range(unroll):
            state_t = chunk(c * unroll + k, state_t)
        return state_t

    lax.fori_loop(0, nc // unroll, chunks, jnp.zeros((B_DV, B_DK), F32))


def _gla(u, z, w_gate, b_gate, g_gla, batch, seq):
    m = u.shape[0]
    return pl.pallas_call(
        functools.partial(_gla_kernel, seq=seq),
        grid=(batch, B_HEADS),
        in_specs=[pl.BlockSpec((seq, LANES), lambda b, h: (b, 0)),
                  pl.BlockSpec((LANES, B_DK), lambda b, h: (0, h)),
                  pl.BlockSpec((1, B_DK), lambda b, h: (0, h)),
                  pl.BlockSpec((1, B_DV), lambda b, h: (0, h)),
                  pl.BlockSpec((seq, B_DK), lambda b, h: (b, QB0 // B_DK + h)),
                  pl.BlockSpec((seq, B_DK), lambda b, h: (b, KB0 // B_DK + h)),
                  pl.BlockSpec((seq, B_DV), lambda b, h: (b, VB0 // B_DV + h)),
                  pl.BlockSpec((seq, B_DV), lambda b, h: (b, GB0 // B_DV + h))],
        out_specs=pl.BlockSpec((seq, B_DV), lambda b, h: (b, h)),
        out_shape=jax.ShapeDtypeStruct((m, B_WIDTH), BF16),
        compiler_params=_cparams("parallel", "parallel"), name="gla",
    )(z, w_gate, b_gate.reshape(1, -1), g_gla.reshape(1, -1), u, u, u, u)


def _pool_kernel(u_ref, w_ref, sc_ref, o_ref, carry_ref, *, tr):
    ti = pl.program_id(1)

    @pl.when(ti == 0)
    def _():
        carry_ref[...] = jnp.zeros_like(carry_ref)

    x = u_ref[...].astype(F32)
    ext = jnp.concatenate([carry_ref[...], x], axis=0)
    carry_ref[...] = x[tr - POOL_HALO:, :]
    pos = ti * tr + lax.broadcasted_iota(jnp.int32, (tr, 1), 0)
    for g, w in enumerate(POOL_WINDOWS):
        lo, hi = g * POOL_CH, (g + 1) * POOL_CH
        s = ext[:, lo:hi]
        sh = 1
        while sh < w:
            s = s + pltpu.roll(s, sh, axis=0)
            sh *= 2
        cnt = jnp.minimum(pos + 1, w).astype(F32)
        d = s[POOL_HALO:, :] / cnt - x[:, lo:hi]
        y = _dot(d.astype(BF16), w_ref[g]) * sc_ref[:, lo:hi]
        o_ref[:, lo:hi] = y.astype(BF16)


def _pool(uc, w_pool, pool_scale, batch, seq):
    m = uc.shape[0]
    tr = _tile(seq, 512)
    nt = seq // tr
    assert tr >= POOL_HALO and max(POOL_WINDOWS) <= POOL_HALO
    return pl.pallas_call(
        functools.partial(_pool_kernel, tr=tr),
        grid=(batch, nt),
        in_specs=[pl.BlockSpec((tr, C_WIDTH), lambda b, i: (b * nt + i, 0)),
                  pl.BlockSpec((N_POOL, POOL_CH, POOL_CH), lambda b, i: (0, 0, 0)),
                  pl.BlockSpec((1, C_WIDTH), lambda b, i: (0, 0))],
        out_specs=pl.BlockSpec((tr, C_WIDTH), lambda b, i: (b * nt + i, 0)),
        out_shape=jax.ShapeDtypeStruct((m, C_WIDTH), BF16),
        scratch_shapes=[pltpu.VMEM((POOL_HALO, C_WIDTH), F32)],
        compiler_params=_cparams("parallel", "arbitrary"), name="pool",
    )(uc, w_pool, pool_scale.reshape(1, C_WIDTH))


def _mm_out_kernel(a_ref, b_ref, c_ref, wa_ref, wb_ref, wc_ref, o_ref):
    o_ref[...] = (_dot(a_ref[...], wa_ref[...].astype(BF16))
                  + _dot(b_ref[...], wb_ref[...].astype(BF16))
                  + _dot(c_ref[...], wc_ref[...].astype(BF16))).astype(o_ref.dtype)


def _mm_out(oa, ob, oc, w_o, layer):
    m = oa.shape[0]
    n = w_o.shape[2]
    tm = _tile(m, 1024)
    tn = _tile(n, 512)
    assert A_WIDTH % B_WIDTH == 0 and B_WIDTH == C_WIDTH
    return pl.pallas_call(
        _mm_out_kernel, grid=(m // tm, n // tn),
        in_specs=[pl.BlockSpec((tm, A_WIDTH), lambda i, j: (i, 0)),
                  pl.BlockSpec((tm, B_WIDTH), lambda i, j: (i, 0)),
                  pl.BlockSpec((tm, C_WIDTH), lambda i, j: (i, 0)),
                  pl.BlockSpec((None, A_WIDTH, tn), lambda i, j: (layer, 0, j)),
                  pl.BlockSpec((None, B_WIDTH, tn), lambda i, j: (layer, A_WIDTH // B_WIDTH, j)),
                  pl.BlockSpec((None, C_WIDTH, tn), lambda i, j: (layer, A_WIDTH // C_WIDTH + 1, j))],
        out_specs=pl.BlockSpec((tm, tn), lambda i, j: (i, j)),
        out_shape=jax.ShapeDtypeStruct((m, n), BRANCH_DTYPE),
        compiler_params=_cparams("parallel", "parallel"), name="mm_out",
    )(oa, ob, oc, w_o, w_o, w_o)


GELU_C = math.sqrt(2.0 / math.pi)


def _mm_up_kernel(x_ref, wg_ref, wv_ref, cg_ref, cv_ref, bg_ref, bv_ref, o_ref,
                  tail_ref, yg_ref, yv_ref, *, tm, nj, tiles_per_seq):
    i = pl.program_id(0)
    j = pl.program_id(1)
    seq_start = (i % tiles_per_seq) == 0
    pad = 8

    @pl.when(seq_start)
    def _():
        tail_ref[j] = jnp.zeros((pad, tail_ref.shape[2]), F32)
        tail_ref[nj + j] = jnp.zeros((pad, tail_ref.shape[2]), F32)

    rs0 = min(tm, UP_ROWS)
    pieces = [(r0, rs0) for r0 in range(0, tm, rs0)]
    npieces = len(pieces)
    wg = wg_ref[...].astype(BF16)
    wv = wv_ref[...].astype(BF16)
    yg_ref[0:pad, :] = tail_ref[j]
    yv_ref[0:pad, :] = tail_ref[nj + j]
    cg = cg_ref[...]
    cv = cv_ref[...] * 0.5
    bg = bg_ref[...]
    bv = bv_ref[...] * 0.5

    def matmuls(r, zero):
        r0, rs = pieces[r]
        x = x_ref[r0:r0 + rs, :]
        if zero is not None:
            xb = pltpu.bitcast(x, jnp.uint32)
            x = pltpu.bitcast(xb | jnp.tile(zero, (xb.shape[0] // 8, xb.shape[1] // LANES)), BF16)
        yg_ref[pad + r0:pad + r0 + rs, :] = _dot(x, wg)
        yv_ref[pad + r0:pad + r0 + rs, :] = _dot(x, wv)

    def conv(y_ref, cw, b, r0, rs):
        return (b + cw[2:3, :] * y_ref[pad + r0:pad + r0 + rs, :]
                + cw[1:2, :] * y_ref[pl.ds(pad + r0 - 1, rs), :]
                + cw[0:1, :] * y_ref[pl.ds(pad + r0 - 2, rs), :])

    def epilogue(r):
        r0, rs = pieces[r]
        zg = conv(yg_ref, cg, bg, r0, rs)
        zv = conv(yv_ref, cv, bv, r0, rs)
        t = jnp.tanh(zg * (GELU_C + (GELU_C * 0.044715) * (zg * zg)))
        out = ((zg + zg * t) * zv).astype(BF16)
        o_ref[r0:r0 + rs, :] = out
        bits = pltpu.bitcast(out, jnp.uint32)
        acc = bits[0:8, :]
        for k in range(1, bits.shape[0] // 8):
            acc = acc | bits[8 * k:8 * k + 8, :]
        red = acc[:, 0:LANES]
        for k in range(1, acc.shape[1] // LANES):
            red = red | acc[:, k * LANES:(k + 1) * LANES]
        return (red >> 16) >> 16

    lag = UP_LAG
    for r in range(min(lag, npieces)):
        matmuls(r, None)
    for r in range(npieces):
        zero = epilogue(r)
        if r + lag < npieces:
            matmuls(r + lag, zero)
    tail_ref[j] = yg_ref[tm:tm + pad, :]
    tail_ref[nj + j] = yv_ref[tm:tm + pad, :]


def _mm_up(hn, w_up, w_conv, b_conv, layer, seq):
    m, d = hn.shape
    f = w_up.shape[2] // 2
    tm = _tile(seq, UP_TM)
    tn = MXU_COLS if f % MXU_COLS == 0 else f
    nj = f // tn
    b3 = b_conv.reshape(b_conv.shape[0], 1, 2 * f)
    return pl.pallas_call(
        functools.partial(_mm_up_kernel, tm=tm, nj=nj, tiles_per_seq=seq // tm),
        grid=(m // tm, nj),
        in_specs=[pl.BlockSpec((tm, d), lambda i, j: (i, 0), pipeline_mode=pl.Buffered(1)),
                  pl.BlockSpec((None, d, tn), lambda i, j: (layer, 0, j)),
                  pl.BlockSpec((None, d, tn), lambda i, j: (layer, 0, nj + j)),
                  pl.BlockSpec((None, CONV_K, tn), lambda i, j: (layer, 0, j)),
                  pl.BlockSpec((None, CONV_K, tn), lambda i, j: (layer, 0, nj + j)),
                  pl.BlockSpec((None, 1, tn), lambda i, j: (layer, 0, j)),
                  pl.BlockSpec((None, 1, tn), lambda i, j: (layer, 0, nj + j))],
        out_specs=pl.BlockSpec((tm, tn), lambda i, j: (i, j)),
        out_shape=jax.ShapeDtypeStruct((m, f), BF16),
        scratch_shapes=[pltpu.VMEM((2 * nj, 8, tn), F32),
                        pltpu.VMEM((tm + 8, tn), F32), pltpu.VMEM((tm + 8, tn), F32)],
        compiler_params=_cparams("arbitrary", "arbitrary"), name="mm_up",
    )(hn, w_up, w_up, w_conv, w_conv, b3, b3)


def _mm_kernel(x_ref, w_ref, o_ref):
    o_ref[...] = _dot(x_ref[...], w_ref[...].astype(BF16)).astype(o_ref.dtype)


def _mm_down(x, w, layer):
    m, k = x.shape
    n = w.shape[2]
    tm = _tile(m, 1024)
    tn = _tile(n, MXU_COLS)
    return pl.pallas_call(
        _mm_kernel, grid=(m // tm, n // tn),
        in_specs=[pl.BlockSpec((tm, k), lambda i, j: (i, 0), pipeline_mode=pl.Buffered(1)),
                  pl.BlockSpec((None, k, tn), lambda i, j: (layer, 0, j))],
        out_specs=pl.BlockSpec((tm, tn), lambda i, j: (i, j)),
        out_shape=jax.ShapeDtypeStruct((m, n), BRANCH_DTYPE),
        compiler_params=_cparams("parallel", "parallel"), name="mm_down",
    )(x, w)


def kernel(x, w_in, w_gate_lr2, b_gate, lam_q1, lam_k1, lam_q2, lam_k2, g_subln, g_gla, w_pool, pool_scale, w_o, w_up, w_conv, b_conv, w_down, g_pre_mix, g_post_mix, g_pre_ffn, g_post_ffn):
    batch, seq, d = x.shape
    depth = w_in.shape[0]
    m = batch * seq
    assert seq % CHUNK == 0 and w_in.shape[2] == U_COLS + GATE_RANK
    assert UC0 == GB0 + B_WIDTH

    w_in_t = jnp.swapaxes(w_in, 1, 2)
    w_pool_b = w_pool.astype(BF16)

    h = x.reshape(m, d)
    branch = None
    for l in range(depth):
        w_gate = jnp.pad(w_gate_lr2[l], ((0, LANES - GATE_RANK), (0, 0))).astype(BF16)

        if l == 0:
            hn = _norm_first(h, g_pre_mix[0])
        else:
            h, hn = _resnorm(h, branch, g_post_ffn[l - 1], g_pre_mix[l])
        u = _mm_in(hn, w_in_t, l)
        z, uc = _mm_tail(hn, w_in_t, l)
        lam_init = 0.8 - 0.6 * math.exp(-0.3 * l)
        oa = _att(u, lam_q1[l], lam_k1[l], lam_q2[l], lam_k2[l], g_subln[l], lam_init, batch, seq)
        ob = _gla(u, z, w_gate, b_gate[l], g_gla[l], batch, seq)
        oc = _pool(uc, w_pool_b[l], pool_scale[l], batch, seq)
        mix = _mm_out(oa, ob, oc, w_o, l)

        h, hn = _resnorm(h, mix, g_post_mix[l], g_pre_ffn[l])
        f_in = _mm_up(hn, w_up, w_conv, b_conv, l, seq)
        branch = _mm_down(f_in, w_down, l)
    h = _resnorm_last(h, branch, g_post_ffn[depth - 1])
    return h.reshape(batch, seq, d)
```

```python
import functools
import math

import jax
import jax.numpy as jnp
from jax import lax
from jax.experimental import pallas as pl
from jax.experimental.pallas import tpu as pltpu

F32 = jnp.float32
BF16 = jnp.bfloat16
BRANCH_DTYPE = jnp.bfloat16

CHUNK = 64
EPS = 1e-6
A_HEADS = 8
A_DH = 128
A_WIDTH = A_HEADS * 2 * A_DH
B_HEADS = 4
B_DK = 128
B_DV = 256
B_WIDTH = B_HEADS * B_DV
GATE_RANK = 16
GATE_TAU = 16.0
POOL_WINDOWS = (2, 4, 8, 16)
N_POOL = len(POOL_WINDOWS)
POOL_CH = 256
C_WIDTH = N_POOL * POOL_CH
CONV_K = 3

QA0 = 0
KA0 = A_WIDTH
VA0 = 2 * A_WIDTH
QB0 = 3 * A_WIDTH
KB0 = QB0 + B_HEADS * B_DK
VB0 = KB0 + B_HEADS * B_DK
GB0 = VB0 + B_WIDTH
UC0 = GB0 + B_WIDTH
U_COLS = UC0 + C_WIDTH

LANES = 128
SUBLANES = 8
MXU_COLS = 256
VMEM_LIMIT = 56 * 1024 * 1024
POOL_HALO = 16
NEG_BIG = -1e30
LOG2E = math.log2(math.e)
NORM_ROWS = 256
ATT_TQ = 512
ATT_GROUP = 1024
UP_TM = 2048
UP_ROWS = 256
UP_LAG = 2
GLA_UNROLL = 8
GLA_SUB = 16
TAIL_ROWS = 1152


def _cparams(*sem):
    return pltpu.CompilerParams(dimension_semantics=sem, vmem_limit_bytes=VMEM_LIMIT)


def _dot(a, b):
    return jnp.dot(a, b, preferred_element_type=F32)


def _dot_nt(a, b):
    return lax.dot_general(a, b, (((1,), (1,)), ((), ())), preferred_element_type=F32)


def _dot_tn(a, b):
    return lax.dot_general(a, b, (((0,), (0,)), ((), ())), preferred_element_type=F32)


def _rms(x, g):
    ms = jnp.mean(x * x, axis=-1, keepdims=True)
    return x * lax.rsqrt(ms + EPS) * g


def _tile(n, want):
    t = min(n, want)
    assert n % t == 0, (n, want)
    return t


def _norm_first_kernel(h_ref, gpre_ref, hn_ref):
    hn_ref[...] = _rms(h_ref[...], gpre_ref[...]).astype(BF16)


def _resnorm_kernel(h_ref, br_ref, gpost_ref, gpre_ref, hout_ref, hn_ref):
    h = h_ref[...] + _rms(br_ref[...].astype(F32), gpost_ref[...])
    hout_ref[...] = h
    hn_ref[...] = _rms(h, gpre_ref[...]).astype(BF16)


def _resnorm_last_kernel(h_ref, br_ref, gpost_ref, hout_ref):
    hout_ref[...] = h_ref[...] + _rms(br_ref[...].astype(F32), gpost_ref[...])


def _norm_first(h, gpre):
    m, d = h.shape
    tr = _tile(m, NORM_ROWS)
    row = pl.BlockSpec((tr, d), lambda i: (i, 0))
    vec = pl.BlockSpec((1, d), lambda i: (0, 0))
    return pl.pallas_call(
        _norm_first_kernel, grid=(m // tr,), in_specs=[row, vec], out_specs=row,
        out_shape=jax.ShapeDtypeStruct((m, d), BF16),
        compiler_params=_cparams("parallel"), name="norm_first",
    )(h, gpre.reshape(1, d))


def _resnorm(h, br, gpost, gpre):
    m, d = h.shape
    tr = _tile(m, NORM_ROWS)
    row = pl.BlockSpec((tr, d), lambda i: (i, 0))
    vec = pl.BlockSpec((1, d), lambda i: (0, 0))
    return pl.pallas_call(
        _resnorm_kernel, grid=(m // tr,), in_specs=[row, row, vec, vec], out_specs=[row, row],
        out_shape=[jax.ShapeDtypeStruct((m, d), F32), jax.ShapeDtypeStruct((m, d), BF16)],
        compiler_params=_cparams("parallel"), name="resnorm",
    )(h, br, gpost.reshape(1, d), gpre.reshape(1, d))


def _resnorm_last(h, br, gpost):
    m, d = h.shape
    tr = _tile(m, NORM_ROWS)
    row = pl.BlockSpec((tr, d), lambda i: (i, 0))
    vec = pl.BlockSpec((1, d), lambda i: (0, 0))
    return pl.pallas_call(
        _resnorm_last_kernel, grid=(m // tr,), in_specs=[row, row, vec], out_specs=row,
        out_shape=jax.ShapeDtypeStruct((m, d), F32),
        compiler_params=_cparams("parallel"), name="resnorm_last",
    )(h, br, gpost.reshape(1, d))


def _mm_in_kernel(x_ref, w_ref, u_ref):
    u_ref[...] = _dot_nt(x_ref[...], w_ref[...].astype(BF16)).astype(BF16)


def _mm_in(hn, w_in_t, layer):
    m, d = hn.shape
    tm = _tile(m, 1024)
    tn = 512
    assert UC0 % tn == 0
    return pl.pallas_call(
        _mm_in_kernel, grid=(m // tm, UC0 // tn),
        in_specs=[pl.BlockSpec((tm, d), lambda i, j: (i, 0)),
                  pl.BlockSpec((None, tn, d), lambda i, j: (layer, j, 0))],
        out_specs=pl.BlockSpec((tm, tn), lambda i, j: (i, j)),
        out_shape=jax.ShapeDtypeStruct((m, UC0), BF16),
        compiler_params=_cparams("parallel", "arbitrary"), name="mm_in",
    )(hn, w_in_t)


def _mm_tail_kernel(x_ref, w_ref, z_ref, uc_ref):
    x = x_ref[...]
    z_ref[...] = _dot_nt(x, w_ref[0:LANES, :].astype(BF16))
    uc_ref[...] = _dot_nt(x, w_ref[GATE_RANK:GATE_RANK + C_WIDTH, :].astype(BF16)).astype(BF16)


def _mm_tail(hn, w_in_t, layer):
    m, d = hn.shape
    tm = _tile(m, 1024)
    rows = TAIL_ROWS
    assert UC0 % rows == 0 and rows >= GATE_RANK + C_WIDTH and GATE_RANK % SUBLANES == 0
    return pl.pallas_call(
        _mm_tail_kernel, grid=(m // tm,),
        in_specs=[pl.BlockSpec((tm, d), lambda i: (i, 0)),
                  pl.BlockSpec((None, rows, d), lambda i: (layer, UC0 // rows, 0),
                               pipeline_mode=pl.Buffered(1))],
        out_specs=[pl.BlockSpec((tm, LANES), lambda i: (i, 0)),
                   pl.BlockSpec((tm, C_WIDTH), lambda i: (i, 0))],
        out_shape=[jax.ShapeDtypeStruct((m, LANES), F32), jax.ShapeDtypeStruct((m, C_WIDTH), BF16)],
        compiler_params=_cparams("arbitrary"), name="mm_tail",
    )(hn, w_in_t)


def _att_kernel(lq1_ref, lk1_ref, lq2_ref, lk2_ref, g_ref, q_ref, k_ref, v_ref, o_ref,
                s1_ref, s2_ref, acc1_ref, acc2_ref, bias_ref, stat_ref, *, lam_init, t, G):
    head = pl.program_id(1)
    qi = pl.program_id(2)
    qscale = LOG2E / math.sqrt(A_DH)
    slope = lax.bitcast_convert_type(
        jnp.full((1, 1), (126 - head) << 23, jnp.int32), F32) * LOG2E

    q = q_ref[...].astype(F32) * qscale
    qt1 = q[:, :A_DH].T.astype(BF16)
    qt2 = q[:, A_DH:].T.astype(BF16)

    def fold8(x, op):
        return op(x.reshape(x.shape[0] // SUBLANES, SUBLANES, t), axis=0)

    nt = G // t

    @pl.when(qi == 0)
    def _():
        cshift = CHUNK.bit_length() - 1
        kd = lax.broadcasted_iota(jnp.int32, (t, t), 0)
        rd = lax.broadcasted_iota(jnp.int32, (t, t), 1)
        corr = slope * (rd - jnp.abs(rd - kd) - kd).astype(F32)
        diag = jnp.where((kd >> cshift) <= (rd >> cshift), corr, NEG_BIG)
        for i in range(nt):
            base = slope * (kd + i * t).astype(F32)
            bias_ref[0, i * t:(i + 1) * t, :] = base
            for j in range(i, nt):
                bias_ref[1 + j, i * t:(i + 1) * t, :] = base + diag if i == j else base

    qs = qi * t
    gl = qs // G
    jd = (qs - gl * G) // t

    def offset(g):
        return slope * (g * G - qs).astype(F32)

    l0 = pl.multiple_of(gl * G, G)

    def scores(k0, rows, bias, off, carry):
        m1, m2 = carry
        kt = k_ref[pl.ds(k0, rows), :]
        s1 = _dot(kt[:, :A_DH], qt1) + bias
        s2 = _dot(kt[:, A_DH:], qt2) + bias
        s1_ref[pl.ds(k0, rows), :] = s1
        s2_ref[pl.ds(k0, rows), :] = s2
        return (jnp.maximum(m1, fold8(s1, jnp.max) + off), jnp.maximum(m2, fold8(s2, jnp.max) + off))

    def score_body(g, carry):
        return scores(pl.multiple_of(g * G, G), G, bias_ref[0], offset(g), carry)

    neg8 = jnp.full((SUBLANES, t), NEG_BIG, F32)
    m_groups = lax.fori_loop(0, gl, score_body, (neg8, neg8))
    for j in range(nt):
        @pl.when(jd == j)
        def _():
            rows = (j + 1) * t
            m1, m2 = scores(l0, rows, bias_ref[1 + j, 0:rows, :], offset(gl), m_groups)
            stat_ref[0] = m1
            stat_ref[1] = m2
    m1 = jnp.max(stat_ref[0], axis=0, keepdims=True)
    m2 = jnp.max(stat_ref[1], axis=0, keepdims=True)

    acc1_ref[...] = jnp.zeros_like(acc1_ref)
    acc2_ref[...] = jnp.zeros_like(acc2_ref)

    def probs(k0, rows, off, carry):
        l1, l2 = carry
        vt = v_ref[pl.ds(k0, rows), :]
        p1 = jnp.exp2(s1_ref[pl.ds(k0, rows), :] - (m1 - off))
        p2 = jnp.exp2(s2_ref[pl.ds(k0, rows), :] - (m2 - off))
        acc1_ref[...] += _dot_tn(vt, p1.astype(BF16))
        acc2_ref[...] += _dot_tn(vt, p2.astype(BF16))
        return l1 + fold8(p1, jnp.sum), l2 + fold8(p2, jnp.sum)

    def pv_body(g, carry):
        return probs(pl.multiple_of(g * G, G), G, offset(g), carry)

    zero8 = jnp.zeros((SUBLANES, t), F32)
    l_groups = lax.fori_loop(0, gl, pv_body, (zero8, zero8))
    for j in range(nt):
        @pl.when(jd == j)
        def _():
            l1, l2 = probs(l0, (j + 1) * t, offset(gl), l_groups)
            stat_ref[0] = l1
            stat_ref[1] = l2
    l1 = jnp.sum(stat_ref[0], axis=0, keepdims=True)
    l2 = jnp.sum(stat_ref[1], axis=0, keepdims=True)

    lam = (jnp.exp(jnp.sum(lq1_ref[...] * lk1_ref[...], axis=-1, keepdims=True))
           - jnp.exp(jnp.sum(lq2_ref[...] * lk2_ref[...], axis=-1, keepdims=True))
           + lam_init)
    ot = acc1_ref[...] * (1.0 / l1) - acc2_ref[...] * (lam / l2)
    ms = jnp.mean(ot * ot, axis=0, keepdims=True)
    o = (ot * lax.rsqrt(ms + EPS)).T
    o_ref[...] = (o * g_ref[...] * (1.0 - lam_init)).astype(BF16)


def _att(u, lq1, lk1, lq2, lk2, g_subln, lam_init, batch, seq):
    m = u.shape[0]
    t = _tile(seq, ATT_TQ)
    G = _tile(seq, ATT_GROUP)
    nq = seq // t
    e = 2 * A_DH
    assert G % t == 0
    vec = pl.BlockSpec((1, A_DH), lambda b, h, i: (0, 0))
    return pl.pallas_call(
        functools.partial(_att_kernel, lam_init=lam_init, t=t, G=G),
        grid=(batch, A_HEADS, nq),
        in_specs=[vec, vec, vec, vec,
                  pl.BlockSpec((1, e), lambda b, h, i: (0, 0)),
                  pl.BlockSpec((t, e), lambda b, h, i: (b * nq + i, QA0 // e + h)),
                  pl.BlockSpec((seq, e), lambda b, h, i: (b, KA0 // e + h)),
                  pl.BlockSpec((seq, e), lambda b, h, i: (b, VA0 // e + h))],
        out_specs=pl.BlockSpec((t, e), lambda b, h, i: (b * nq + i, h)),
        out_shape=jax.ShapeDtypeStruct((m, A_WIDTH), BF16),
        scratch_shapes=[pltpu.VMEM((seq, t), F32), pltpu.VMEM((seq, t), F32),
                        pltpu.VMEM((e, t), F32), pltpu.VMEM((e, t), F32),
                        pltpu.VMEM((1 + G // t, G, t), F32), pltpu.VMEM((2, SUBLANES, t), F32)],
        compiler_params=_cparams("arbitrary", "arbitrary", "arbitrary"), name="diff_attn",
    )(lq1.reshape(1, A_DH), lk1.reshape(1, A_DH), lq2.reshape(1, A_DH), lk2.reshape(1, A_DH),
      g_subln.reshape(1, e), u, u, u)


def _gla_kernel(z_ref, wg_ref, bg_ref, gn_ref, q_ref, k_ref, v_ref, gb_ref, o_ref, *, seq):
    L = CHUNK
    wg = wg_ref[...]
    bg = bg_ref[...]
    gn = gn_ref[...]
    rows = lax.broadcasted_iota(jnp.int32, (L, B_DK), 0)
    ti = lax.broadcasted_iota(jnp.int32, (L, L), 0)
    si = lax.broadcasted_iota(jnp.int32, (L, L), 1)
    SUB = GLA_SUB
    assert L == 4 * SUB
    sshift = SUB.bit_length() - 1
    bi = ti >> sshift
    bj = si >> sshift
    sd = si - (ti - (ti & (SUB - 1)))
    m_diag = (bi == bj) & (si <= ti)
    m_near = (bi == bj + 1) & ((bi & 1) == 1)
    m_far = (bi >= 2) & (bj <= 1)
    qscale = B_DK ** -0.5

    def chunk(c, state_t):
        r0 = pl.multiple_of(c * L, L)
        x = _dot(z_ref[pl.ds(r0, L), :].astype(BF16), wg) + bg
        la = (jnp.minimum(x, 0.0) - jnp.log(1.0 + jnp.exp(-jnp.abs(x)))) * (1.0 / GATE_TAU)
        b = la
        sh = 1
        while sh < L:
            b = b + jnp.where(rows >= sh, pltpu.roll(b, sh, axis=0), 0.0)
            sh *= 2
        b = b * LOG2E
        b_last = b[L - 1:L, :]
        qf = q_ref[pl.ds(r0, L), :].astype(F32) * qscale
        kc = k_ref[pl.ds(r0, L), :]
        kf = kc.astype(F32)
        vc = v_ref[pl.ds(r0, L), :]

        def rescaled(ref):
            qr = (qf * jnp.exp2(jnp.minimum(b - ref, 0.0))).astype(BF16)
            kr = (kf * jnp.exp2(jnp.minimum(ref - b, 0.0))).astype(BF16)
            return _dot_nt(qr, kr)

        half = L // 2
        a_far = rescaled(b[half - 1:half, :])
        ref_near = jnp.where(rows < half, b[SUB - 1:SUB, :], b[half + SUB - 1:half + SUB, :])
        a_near = rescaled(ref_near)
        a = jnp.zeros((L, L), F32)
        for s in range(SUB):
            ref = jnp.concatenate(
                [jnp.broadcast_to(b[i * SUB + s:i * SUB + s + 1, :], (SUB, B_DK)) for i in range(L // SUB)],
                axis=0)
            e = jnp.exp2(jnp.minimum(b - ref, 0.0))
            rs = _dot_nt((qf * e).astype(BF16), kc)
            a = jnp.where(sd == s, rs, a)
        a = jnp.where(m_far, a_far, jnp.where(m_near, a_near, jnp.where(m_diag, a, 0.0)))

        qe = (qf * jnp.exp2(b)).astype(BF16)
        o = _dot(a.astype(BF16), vc) + _dot_nt(qe, state_t.astype(BF16))
        ke = (kf * jnp.exp2(b_last - b)).astype(BF16)
        new_state = jnp.exp2(b_last) * state_t + _dot_tn(vc, ke)

        gate = gb_ref[pl.ds(r0, L), :].astype(F32)
        y = _rms(o, gn) * (gate / (1.0 + jnp.exp(-gate)))
        o_ref[pl.ds(r0, L), :] = y.astype(BF16)
        return new_state

    nc = seq // L
    unroll = GLA_UNROLL if nc % GLA_UNROLL == 0 else 1

    def chunks(c, state_t):
        for k in range(unroll):
            state_t = chunk(c * unroll + k, state_t)
        return state_t

    lax.fori_loop(0, nc // unroll, chunks, jnp.zeros((B_DV, B_DK), F32))


def _gla(u, z, w_gate, b_gate, g_gla, batch, seq):
    m = u.shape[0]
    return pl.pallas_call(
        functools.partial(_gla_kernel, seq=seq),
        grid=(batch, B_HEADS),
        in_specs=[pl.BlockSpec((seq, LANES), lambda b, h: (b, 0)),
                  pl.BlockSpec((LANES, B_DK), lambda b, h: (0, h)),
                  pl.BlockSpec((1, B_DK), lambda b, h: (0, h)),
                  pl.BlockSpec((1, B_DV), lambda b, h: (0, h)),
                  pl.BlockSpec((seq, B_DK), lambda b, h: (b, QB0 // B_DK + h)),
                  pl.BlockSpec((seq, B_DK), lambda b, h: (b, KB0 // B_DK + h)),
                  pl.BlockSpec((seq, B_DV), lambda b, h: (b, VB0 // B_DV + h)),
                  pl.BlockSpec((seq, B_DV), lambda b, h: (b, GB0 // B_DV + h))],
        out_specs=pl.BlockSpec((seq, B_DV), lambda b, h: (b, h)),
        out_shape=jax.ShapeDtypeStruct((m, B_WIDTH), BF16),
        compiler_params=_cparams("parallel", "parallel"), name="gla",
    )(z, w_gate, b_gate.reshape(1, -1), g_gla.reshape(1, -1), u, u, u, u)


def _pool_kernel(u_ref, w_ref, sc_ref, o_ref, carry_ref, *, tr):
    ti = pl.program_id(1)

    @pl.when(ti == 0)
    def _():
        carry_ref[...] = jnp.zeros_like(carry_ref)

    x = u_ref[...].astype(F32)
    ext = jnp.concatenate([carry_ref[...], x], axis=0)
    carry_ref[...] = x[tr - POOL_HALO:, :]
    pos = ti * tr + lax.broadcasted_iota(jnp.int32, (tr, 1), 0)
    for g, w in enumerate(POOL_WINDOWS):
        lo, hi = g * POOL_CH, (g + 1) * POOL_CH
        s = ext[:, lo:hi]
        sh = 1
        while sh < w:
            s = s + pltpu.roll(s, sh, axis=0)
            sh *= 2
        cnt = jnp.minimum(pos + 1, w).astype(F32)
        d = s[POOL_HALO:, :] / cnt - x[:, lo:hi]
        y = _dot(d.astype(BF16), w_ref[g]) * sc_ref[:, lo:hi]
        o_ref[:, lo:hi] = y.astype(BF16)


def _pool(uc, w_pool, pool_scale, batch, seq):
    m = uc.shape[0]
    tr = _tile(seq, 512)
    nt = seq // tr
    assert tr >= POOL_HALO and max(POOL_WINDOWS) <= POOL_HALO
    return pl.pallas_call(
        functools.partial(_pool_kernel, tr=tr),
        grid=(batch, nt),
        in_specs=[pl.BlockSpec((tr, C_WIDTH), lambda b, i: (b * nt + i, 0)),
                  pl.BlockSpec((N_POOL, POOL_CH, POOL_CH), lambda b, i: (0, 0, 0)),
                  pl.BlockSpec((1, C_WIDTH), lambda b, i: (0, 0))],
        out_specs=pl.BlockSpec((tr, C_WIDTH), lambda b, i: (b * nt + i, 0)),
        out_shape=jax.ShapeDtypeStruct((m, C_WIDTH), BF16),
        scratch_shapes=[pltpu.VMEM((POOL_HALO, C_WIDTH), F32)],
        compiler_params=_cparams("parallel", "arbitrary"), name="pool",
    )(uc, w_pool, pool_scale.reshape(1, C_WIDTH))


def _mm_out_kernel(a_ref, b_ref, c_ref, wa_ref, wb_ref, wc_ref, o_ref):
    o_ref[...] = (_dot(a_ref[...], wa_ref[...].astype(BF16))
                  + _dot(b_ref[...], wb_ref[...].astype(BF16))
                  + _dot(c_ref[...], wc_ref[...].astype(BF16))).astype(o_ref.dtype)


def _mm_out(oa, ob, oc, w_o, layer):
    m = oa.shape[0]
    n = w_o.shape[2]
    tm = _tile(m, 1024)
    tn = _tile(n, 512)
    assert A_WIDTH % B_WIDTH == 0 and B_WIDTH == C_WIDTH
    return pl.pallas_call(
        _mm_out_kernel, grid=(m // tm, n // tn),
        in_specs=[pl.BlockSpec((tm, A_WIDTH), lambda i, j: (i, 0)),
                  pl.BlockSpec((tm, B_WIDTH), lambda i, j: (i, 0)),
                  pl.BlockSpec((tm, C_WIDTH), lambda i, j: (i, 0)),
                  pl.BlockSpec((None, A_WIDTH, tn), lambda i, j: (layer, 0, j)),
                  pl.BlockSpec((None, B_WIDTH, tn), lambda i, j: (layer, A_WIDTH // B_WIDTH, j)),
                  pl.BlockSpec((None, C_WIDTH, tn), lambda i, j: (layer, A_WIDTH // C_WIDTH + 1, j))],
        out_specs=pl.BlockSpec((tm, tn), lambda i, j: (i, j)),
        out_shape=jax.ShapeDtypeStruct((m, n), BRANCH_DTYPE),
        compiler_params=_cparams("parallel", "parallel"), name="mm_out",
    )(oa, ob, oc, w_o, w_o, w_o)


GELU_C = math.sqrt(2.0 / math.pi)


def _mm_up_kernel(x_ref, wg_ref, wv_ref, cg_ref, cv_ref, bg_ref, bv_ref, o_ref,
                  tail_ref, yg_ref, yv_ref, *, tm, nj, tiles_per_seq):
    i = pl.program_id(0)
    j = pl.program_id(1)
    seq_start = (i % tiles_per_seq) == 0
    pad = SUBLANES

    @pl.when(seq_start)
    def _():
        tail_ref[j] = jnp.zeros((pad, tail_ref.shape[2]), F32)
        tail_ref[nj + j] = jnp.zeros((pad, tail_ref.shape[2]), F32)

    rs0 = min(tm, UP_ROWS)
    pieces = [(r0, rs0) for r0 in range(0, tm, rs0)]
    npieces = len(pieces)
    wg = wg_ref[...].astype(BF16)
    wv = wv_ref[...].astype(BF16)
    yg_ref[0:pad, :] = tail_ref[j]
    yv_ref[0:pad, :] = tail_ref[nj + j]
    cg = cg_ref[...]
    cv = cv_ref[...] * 0.5
    bg = bg_ref[...]
    bv = bv_ref[...] * 0.5

    def matmuls(r, zero):
        r0, rs = pieces[r]
        x = x_ref[r0:r0 + rs, :]
        if zero is not None:
            xb = pltpu.bitcast(x, jnp.uint32)
            x = pltpu.bitcast(xb | jnp.tile(zero, (xb.shape[0] // SUBLANES, xb.shape[1] // LANES)), BF16)
        yg_ref[pad + r0:pad + r0 + rs, :] = _dot(x, wg)
        yv_ref[pad + r0:pad + r0 + rs, :] = _dot(x, wv)

    def conv(y_ref, cw, b, r0, rs):
        return (b + cw[2:3, :] * y_ref[pad + r0:pad + r0 + rs, :]
                + cw[1:2, :] * y_ref[pl.ds(pad + r0 - 1, rs), :]
                + cw[0:1, :] * y_ref[pl.ds(pad + r0 - 2, rs), :])

    def epilogue(r):
        r0, rs = pieces[r]
        zg = conv(yg_ref, cg, bg, r0, rs)
        zv = conv(yv_ref, cv, bv, r0, rs)
        t = jnp.tanh(zg * (GELU_C + (GELU_C * 0.044715) * (zg * zg)))
        out = ((zg + zg * t) * zv).astype(BF16)
        o_ref[r0:r0 + rs, :] = out
        bits = pltpu.bitcast(out, jnp.uint32)
        acc = bits[0:SUBLANES, :]
        for k in range(1, bits.shape[0] // SUBLANES):
            acc = acc | bits[SUBLANES * k:SUBLANES * (k + 1), :]
        red = acc[:, 0:LANES]
        for k in range(1, acc.shape[1] // LANES):
            red = red | acc[:, k * LANES:(k + 1) * LANES]
        return (red >> 16) >> 16

    lag = UP_LAG
    for r in range(min(lag, npieces)):
        matmuls(r, None)
    for r in range(npieces):
        zero = epilogue(r)
        if r + lag < npieces:
            matmuls(r + lag, zero)
    tail_ref[j] = yg_ref[tm:tm + pad, :]
    tail_ref[nj + j] = yv_ref[tm:tm + pad, :]


def _mm_up(hn, w_up, w_conv, b_conv, layer, seq):
    m, d = hn.shape
    f = w_up.shape[2] // 2
    tm = _tile(seq, UP_TM)
    tn = MXU_COLS if f % MXU_COLS == 0 else f
    nj = f // tn
    b3 = b_conv.reshape(b_conv.shape[0], 1, 2 * f)
    return pl.pallas_call(
        functools.partial(_mm_up_kernel, tm=tm, nj=nj, tiles_per_seq=seq // tm),
        grid=(m // tm, nj),
        in_specs=[pl.BlockSpec((tm, d), lambda i, j: (i, 0), pipeline_mode=pl.Buffered(1)),
                  pl.BlockSpec((None, d, tn), lambda i, j: (layer, 0, j)),
                  pl.BlockSpec((None, d, tn), lambda i, j: (layer, 0, nj + j)),
                  pl.BlockSpec((None, CONV_K, tn), lambda i, j: (layer, 0, j)),
                  pl.BlockSpec((None, CONV_K, tn), lambda i, j: (layer, 0, nj + j)),
                  pl.BlockSpec((None, 1, tn), lambda i, j: (layer, 0, j)),
                  pl.BlockSpec((None, 1, tn), lambda i, j: (layer, 0, nj + j))],
        out_specs=pl.BlockSpec((tm, tn), lambda i, j: (i, j)),
        out_shape=jax.ShapeDtypeStruct((m, f), BF16),
        scratch_shapes=[pltpu.VMEM((2 * nj, SUBLANES, tn), F32),
                        pltpu.VMEM((tm + SUBLANES, tn), F32), pltpu.VMEM((tm + SUBLANES, tn), F32)],
        compiler_params=_cparams("arbitrary", "arbitrary"), name="mm_up",
    )(hn, w_up, w_up, w_conv, w_conv, b3, b3)


def _mm_kernel(x_ref, w_ref, o_ref):
    o_ref[...] = _dot(x_ref[...], w_ref[...].astype(BF16)).astype(o_ref.dtype)


def _mm_down(x, w, layer):
    m, k = x.shape
    n = w.shape[2]
    tm = _tile(m, 1024)
    tn = _tile(n, MXU_COLS)
    return pl.pallas_call(
        _mm_kernel, grid=(m // tm, n // tn),
        in_specs=[pl.BlockSpec((tm, k), lambda i, j: (i, 0), pipeline_mode=pl.Buffered(1)),
                  pl.BlockSpec((None, k, tn), lambda i, j: (layer, 0, j))],
        out_specs=pl.BlockSpec((tm, tn), lambda i, j: (i, j)),
        out_shape=jax.ShapeDtypeStruct((m, n), BRANCH_DTYPE),
        compiler_params=_cparams("parallel", "parallel"), name="mm_down",
    )(x, w)


def kernel(x, w_in, w_gate_lr2, b_gate, lam_q1, lam_k1, lam_q2, lam_k2, g_subln, g_gla, w_pool, pool_scale, w_o, w_up, w_conv, b_conv, w_down, g_pre_mix, g_post_mix, g_pre_ffn, g_post_ffn):
    batch, seq, d = x.shape
    depth = w_in.shape[0]
    m = batch * seq
    assert seq % CHUNK == 0 and w_in.shape[2] == U_COLS + GATE_RANK
    assert UC0 == GB0 + B_WIDTH

    w_in_t = jnp.swapaxes(w_in, 1, 2)
    w_pool_b = w_pool.astype(BF16)

    h = x.reshape(m, d)
    branch = None
    for l in range(depth):
        w_gate = jnp.pad(w_gate_lr2[l], ((0, LANES - GATE_RANK), (0, 0))).astype(BF16)

        if l == 0:
            hn = _norm_first(h, g_pre_mix[0])
        else:
            h, hn = _resnorm(h, branch, g_post_ffn[l - 1], g_pre_mix[l])
        u = _mm_in(hn, w_in_t, l)
        z, uc = _mm_tail(hn, w_in_t, l)
        lam_init = 0.8 - 0.6 * math.exp(-0.3 * l)
        oa = _att(u, lam_q1[l], lam_k1[l], lam_q2[l], lam_k2[l], g_subln[l], lam_init, batch, seq)
        ob = _gla(u, z, w_gate, b_gate[l], g_gla[l], batch, seq)
        oc = _pool(uc, w_pool_b[l], pool_scale[l], batch, seq)
        mix = _mm_out(oa, ob, oc, w_o, l)

        h, hn = _resnorm(h, mix, g_post_mix[l], g_pre_ffn[l])
        f_in = _mm_up(hn, w_up, w_conv, b_conv, l, seq)
        branch = _mm_down(f_in, w_down, l)
    h = _resnorm_last(h, branch, g_post_ffn[depth - 1])
    return h.reshape(batch, seq, d)
```

```python
import functools
import math

import jax
import jax.numpy as jnp
from jax import lax
from jax.experimental import pallas as pl
from jax.experimental.pallas import tpu as pltpu

F32 = jnp.float32
BF16 = jnp.bfloat16
BRANCH_DTYPE = jnp.bfloat16

CHUNK = 64
EPS = 1e-6
A_HEADS = 8
A_DH = 128
A_WIDTH = A_HEADS * 2 * A_DH
B_HEADS = 4
B_DK = 128
B_DV = 256
B_WIDTH = B_HEADS * B_DV
GATE_RANK = 16
GATE_TAU = 16.0
POOL_WINDOWS = (2, 4, 8, 16)
N_POOL = len(POOL_WINDOWS)
POOL_CH = 256
C_WIDTH = N_POOL * POOL_CH
CONV_K = 3

QA0 = 0
KA0 = A_WIDTH
VA0 = 2 * A_WIDTH
QB0 = 3 * A_WIDTH
KB0 = QB0 + B_HEADS * B_DK
VB0 = KB0 + B_HEADS * B_DK
GB0 = VB0 + B_WIDTH
UC0 = GB0 + B_WIDTH
U_COLS = UC0 + C_WIDTH

LANES = 128
SUBLANES = 8
MXU_COLS = 256
VMEM_LIMIT = 56 * 1024 * 1024
POOL_HALO = 16
NEG_BIG = -1e30
LOG2E = math.log2(math.e)
NORM_ROWS = 256
ATT_TQ = 512
ATT_GROUP = 2048
UP_TM = 2048
UP_ROWS = 256
UP_LAG = 2
GLA_UNROLL = 8
GLA_SUB = 16
TAIL_ROWS = 1152


def _cparams(*sem):
    return pltpu.CompilerParams(dimension_semantics=sem, vmem_limit_bytes=VMEM_LIMIT)


def _dot(a, b):
    return jnp.dot(a, b, preferred_element_type=F32)


def _dot_nt(a, b):
    return lax.dot_general(a, b, (((1,), (1,)), ((), ())), preferred_element_type=F32)


def _dot_tn(a, b):
    return lax.dot_general(a, b, (((0,), (0,)), ((), ())), preferred_element_type=F32)


def _rms(x, g):
    ms = jnp.mean(x * x, axis=-1, keepdims=True)
    return x * lax.rsqrt(ms + EPS) * g


def _tile(n, want):
    t = min(n, want)
    assert n % t == 0, (n, want)
    return t


def _norm_first_kernel(h_ref, gpre_ref, hn_ref):
    hn_ref[...] = _rms(h_ref[...], gpre_ref[...]).astype(BF16)


def _resnorm_kernel(h_ref, br_ref, gpost_ref, gpre_ref, hout_ref, hn_ref):
    h = h_ref[...] + _rms(br_ref[...].astype(F32), gpost_ref[...])
    hout_ref[...] = h
    hn_ref[...] = _rms(h, gpre_ref[...]).astype(BF16)


def _resnorm_last_kernel(h_ref, br_ref, gpost_ref, hout_ref):
    hout_ref[...] = h_ref[...] + _rms(br_ref[...].astype(F32), gpost_ref[...])


def _norm_first(h, gpre):
    m, d = h.shape
    tr = _tile(m, NORM_ROWS)
    row = pl.BlockSpec((tr, d), lambda i: (i, 0))
    vec = pl.BlockSpec((1, d), lambda i: (0, 0))
    return pl.pallas_call(
        _norm_first_kernel, grid=(m // tr,), in_specs=[row, vec], out_specs=row,
        out_shape=jax.ShapeDtypeStruct((m, d), BF16),
        compiler_params=_cparams("parallel"), name="norm_first",
    )(h, gpre.reshape(1, d))


def _resnorm(h, br, gpost, gpre):
    m, d = h.shape
    tr = _tile(m, NORM_ROWS)
    row = pl.BlockSpec((tr, d), lambda i: (i, 0))
    vec = pl.BlockSpec((1, d), lambda i: (0, 0))
    return pl.pallas_call(
        _resnorm_kernel, grid=(m // tr,), in_specs=[row, row, vec, vec], out_specs=[row, row],
        out_shape=[jax.ShapeDtypeStruct((m, d), F32), jax.ShapeDtypeStruct((m, d), BF16)],
        compiler_params=_cparams("parallel"), name="resnorm",
    )(h, br, gpost.reshape(1, d), gpre.reshape(1, d))


def _resnorm_last(h, br, gpost):
    m, d = h.shape
    tr = _tile(m, NORM_ROWS)
    row = pl.BlockSpec((tr, d), lambda i: (i, 0))
    vec = pl.BlockSpec((1, d), lambda i: (0, 0))
    return pl.pallas_call(
        _resnorm_last_kernel, grid=(m // tr,), in_specs=[row, row, vec], out_specs=row,
        out_shape=jax.ShapeDtypeStruct((m, d), F32),
        compiler_params=_cparams("parallel"), name="resnorm_last",
    )(h, br, gpost.reshape(1, d))


def _mm_in_kernel(x_ref, w_ref, u_ref):
    u_ref[...] = _dot_nt(x_ref[...], w_ref[...].astype(BF16)).astype(BF16)


def _mm_in(hn, w_in_t, layer):
    m, d = hn.shape
    tm = _tile(m, 1024)
    tn = 512
    assert UC0 % tn == 0
    return pl.pallas_call(
        _mm_in_kernel, grid=(m // tm, UC0 // tn),
        in_specs=[pl.BlockSpec((tm, d), lambda i, j: (i, 0)),
                  pl.BlockSpec((None, tn, d), lambda i, j: (layer, j, 0))],
        out_specs=pl.BlockSpec((tm, tn), lambda i, j: (i, j)),
        out_shape=jax.ShapeDtypeStruct((m, UC0), BF16),
        compiler_params=_cparams("parallel", "arbitrary"), name="mm_in",
    )(hn, w_in_t)


def _mm_tail_kernel(x_ref, w_ref, z_ref, uc_ref):
    x = x_ref[...]
    z_ref[...] = _dot_nt(x, w_ref[0:LANES, :].astype(BF16))
    uc_ref[...] = _dot_nt(x, w_ref[GATE_RANK:GATE_RANK + C_WIDTH, :].astype(BF16)).astype(BF16)


def _mm_tail(hn, w_in_t, layer):
    m, d = hn.shape
    tm = _tile(m, 1024)
    rows = TAIL_ROWS
    assert UC0 % rows == 0 and rows >= GATE_RANK + C_WIDTH and GATE_RANK % SUBLANES == 0
    return pl.pallas_call(
        _mm_tail_kernel, grid=(m // tm,),
        in_specs=[pl.BlockSpec((tm, d), lambda i: (i, 0)),
                  pl.BlockSpec((None, rows, d), lambda i: (layer, UC0 // rows, 0),
                               pipeline_mode=pl.Buffered(1))],
        out_specs=[pl.BlockSpec((tm, LANES), lambda i: (i, 0)),
                   pl.BlockSpec((tm, C_WIDTH), lambda i: (i, 0))],
        out_shape=[jax.ShapeDtypeStruct((m, LANES), F32), jax.ShapeDtypeStruct((m, C_WIDTH), BF16)],
        compiler_params=_cparams("arbitrary"), name="mm_tail",
    )(hn, w_in_t)


def _att_kernel(lq1_ref, lk1_ref, lq2_ref, lk2_ref, g_ref, q_ref, k_ref, v_ref, o_ref,
                s1_ref, s2_ref, acc1_ref, acc2_ref, bias_ref, stat_ref, *, lam_init, t, G):
    head = pl.program_id(1)
    qi = pl.program_id(2)
    qscale = LOG2E / math.sqrt(A_DH)
    slope = lax.bitcast_convert_type(
        jnp.full((1, 1), (126 - head) << 23, jnp.int32), F32) * LOG2E

    q = q_ref[...].astype(F32) * qscale
    qt1 = q[:, :A_DH].T.astype(BF16)
    qt2 = q[:, A_DH:].T.astype(BF16)

    def fold8(x, op):
        return op(x.reshape(x.shape[0] // SUBLANES, SUBLANES, t), axis=0)

    nt = G // t

    @pl.when(qi == 0)
    def _():
        cshift = CHUNK.bit_length() - 1
        kd = lax.broadcasted_iota(jnp.int32, (t, t), 0)
        rd = lax.broadcasted_iota(jnp.int32, (t, t), 1)
        corr = slope * (rd - jnp.abs(rd - kd) - kd).astype(F32)
        diag = jnp.where((kd >> cshift) <= (rd >> cshift), corr, NEG_BIG)
        for i in range(nt):
            base = slope * (kd + i * t).astype(F32)
            bias_ref[0, i * t:(i + 1) * t, :] = base
            for j in range(i, nt):
                bias_ref[1 + j, i * t:(i + 1) * t, :] = base + diag if i == j else base

    qs = qi * t
    gl = qs // G
    jd = (qs - gl * G) // t

    def offset(g):
        return slope * (g * G - qs).astype(F32)

    l0 = pl.multiple_of(gl * G, G)

    def scores(k0, rows, bias, off, carry):
        m1, m2 = carry
        kt = k_ref[pl.ds(k0, rows), :]
        s1 = _dot(kt[:, :A_DH], qt1) + bias
        s2 = _dot(kt[:, A_DH:], qt2) + bias
        s1_ref[pl.ds(k0, rows), :] = s1
        s2_ref[pl.ds(k0, rows), :] = s2
        return (jnp.maximum(m1, fold8(s1, jnp.max) + off), jnp.maximum(m2, fold8(s2, jnp.max) + off))

    def score_body(g, carry):
        return scores(pl.multiple_of(g * G, G), G, bias_ref[0], offset(g), carry)

    neg8 = jnp.full((SUBLANES, t), NEG_BIG, F32)
    m_groups = lax.fori_loop(0, gl, score_body, (neg8, neg8))
    for j in range(nt):
        @pl.when(jd == j)
        def _():
            rows = (j + 1) * t
            m1, m2 = scores(l0, rows, bias_ref[1 + j, 0:rows, :], offset(gl), m_groups)
            stat_ref[0] = m1
            stat_ref[1] = m2
    m1 = jnp.max(stat_ref[0], axis=0, keepdims=True)
    m2 = jnp.max(stat_ref[1], axis=0, keepdims=True)

    acc1_ref[...] = jnp.zeros_like(acc1_ref)
    acc2_ref[...] = jnp.zeros_like(acc2_ref)

    def probs(k0, rows, off, carry):
        l1, l2 = carry
        vt = v_ref[pl.ds(k0, rows), :]
        p1 = jnp.exp2(s1_ref[pl.ds(k0, rows), :] - (m1 - off))
        p2 = jnp.exp2(s2_ref[pl.ds(k0, rows), :] - (m2 - off))
        acc1_ref[...] += _dot_tn(vt, p1.astype(BF16))
        acc2_ref[...] += _dot_tn(vt, p2.astype(BF16))
        return l1 + fold8(p1, jnp.sum), l2 + fold8(p2, jnp.sum)

    def pv_body(g, carry):
        return probs(pl.multiple_of(g * G, G), G, offset(g), carry)

    zero8 = jnp.zeros((SUBLANES, t), F32)
    l_groups = lax.fori_loop(0, gl, pv_body, (zero8, zero8))
    for j in range(nt):
        @pl.when(jd == j)
        def _():
            l1, l2 = probs(l0, (j + 1) * t, offset(gl), l_groups)
            stat_ref[0] = l1
            stat_ref[1] = l2
    l1 = jnp.sum(stat_ref[0], axis=0, keepdims=True)
    l2 = jnp.sum(stat_ref[1], axis=0, keepdims=True)

    lam = (jnp.exp(jnp.sum(lq1_ref[...] * lk1_ref[...], axis=-1, keepdims=True))
           - jnp.exp(jnp.sum(lq2_ref[...] * lk2_ref[...], axis=-1, keepdims=True))
           + lam_init)
    ot = acc1_ref[...] * (1.0 / l1) - acc2_ref[...] * (lam / l2)
    ms = jnp.mean(ot * ot, axis=0, keepdims=True)
    o = (ot * lax.rsqrt(ms + EPS)).T
    o_ref[...] = (o * g_ref[...] * (1.0 - lam_init)).astype(BF16)


def _att(u, lq1, lk1, lq2, lk2, g_subln, lam_init, batch, seq):
    m = u.shape[0]
    t = _tile(seq, ATT_TQ)
    G = _tile(seq, ATT_GROUP)
    nq = seq // t
    e = 2 * A_DH
    assert G % t == 0
    vec = pl.BlockSpec((1, A_DH), lambda b, h, i: (0, 0))
    return pl.pallas_call(
        functools.partial(_att_kernel, lam_init=lam_init, t=t, G=G),
        grid=(batch, A_HEADS, nq),
        in_specs=[vec, vec, vec, vec,
                  pl.BlockSpec((1, e), lambda b, h, i: (0, 0)),
                  pl.BlockSpec((t, e), lambda b, h, i: (b * nq + i, QA0 // e + h)),
                  pl.BlockSpec((seq, e), lambda b, h, i: (b, KA0 // e + h)),
                  pl.BlockSpec((seq, e), lambda b, h, i: (b, VA0 // e + h))],
        out_specs=pl.BlockSpec((t, e), lambda b, h, i: (b * nq + i, h)),
        out_shape=jax.ShapeDtypeStruct((m, A_WIDTH), BF16),
        scratch_shapes=[pltpu.VMEM((seq, t), F32), pltpu.VMEM((seq, t), F32),
                        pltpu.VMEM((e, t), F32), pltpu.VMEM((e, t), F32),
                        pltpu.VMEM((1 + G // t, G, t), F32), pltpu.VMEM((2, SUBLANES, t), F32)],
        compiler_params=_cparams("arbitrary", "arbitrary", "arbitrary"), name="diff_attn",
    )(lq1.reshape(1, A_DH), lk1.reshape(1, A_DH), lq2.reshape(1, A_DH), lk2.reshape(1, A_DH),
      g_subln.reshape(1, e), u, u, u)


def _gla_kernel(z_ref, wg_ref, bg_ref, gn_ref, q_ref, k_ref, v_ref, gb_ref, o_ref, *, seq):
    L = CHUNK
    wg = wg_ref[...]
    bg = bg_ref[...]
    gn = gn_ref[...]
    rows = lax.broadcasted_iota(jnp.int32, (L, B_DK), 0)
    ti = lax.broadcasted_iota(jnp.int32, (L, L), 0)
    si = lax.broadcasted_iota(jnp.int32, (L, L), 1)
    SUB = GLA_SUB
    assert L == 4 * SUB
    sshift = SUB.bit_length() - 1
    bi = ti >> sshift
    bj = si >> sshift
    sd = si - (ti - (ti & (SUB - 1)))
    m_diag = (bi == bj) & (si <= ti)
    m_near = (bi == bj + 1) & ((bi & 1) == 1)
    m_far = (bi >= 2) & (bj <= 1)
    qscale = B_DK ** -0.5

    def chunk(c, state_t):
        r0 = pl.multiple_of(c * L, L)
        x = _dot(z_ref[pl.ds(r0, L), :].astype(BF16), wg) + bg
        la = (jnp.minimum(x, 0.0) - jnp.log(1.0 + jnp.exp(-jnp.abs(x)))) * (1.0 / GATE_TAU)
        b = la
        sh = 1
        while sh < L:
            b = b + jnp.where(rows >= sh, pltpu.roll(b, sh, axis=0), 0.0)
            sh *= 2
        b = b * LOG2E
        b_last = b[L - 1:L, :]
        qf = q_ref[pl.ds(r0, L), :].astype(F32) * qscale
        kc = k_ref[pl.ds(r0, L), :]
        kf = kc.astype(F32)
        vc = v_ref[pl.ds(r0, L), :]

        def rescaled(ref):
            qr = (qf * jnp.exp2(jnp.minimum(b - ref, 0.0))).astype(BF16)
            kr = (kf * jnp.exp2(jnp.minimum(ref - b, 0.0))).astype(BF16)
            return _dot_nt(qr, kr)

        half = L // 2
        a_far = rescaled(b[half - 1:half, :])
        ref_near = jnp.where(rows < half, b[SUB - 1:SUB, :], b[half + SUB - 1:half + SUB, :])
        a_near = rescaled(ref_near)
        a = jnp.zeros((L, L), F32)
        for s in range(SUB):
            ref = jnp.concatenate(
                [jnp.broadcast_to(b[i * SUB + s:i * SUB + s + 1, :], (SUB, B_DK)) for i in range(L // SUB)],
                axis=0)
            e = jnp.exp2(jnp.minimum(b - ref, 0.0))
            rs = _dot_nt((qf * e).astype(BF16), kc)
            a = jnp.where(sd == s, rs, a)
        a = jnp.where(m_far, a_far, jnp.where(m_near, a_near, jnp.where(m_diag, a, 0.0)))

        qe = (qf * jnp.exp2(b)).astype(BF16)
        o = _dot(a.astype(BF16), vc) + _dot_nt(qe, state_t.astype(BF16))
        ke = (kf * jnp.exp2(b_last - b)).astype(BF16)
        new_state = jnp.exp2(b_last) * state_t + _dot_tn(vc, ke)

        gate = gb_ref[pl.ds(r0, L), :].astype(F32)
        y = _rms(o, gn) * (gate / (1.0 + jnp.exp(-gate)))
        o_ref[pl.ds(r0, L), :] = y.astype(BF16)
        return new_state

    nc = seq // L
    unroll = GLA_UNROLL if nc % GLA_UNROLL == 0 else 1

    def chunks(c, state_t):
        for k in range(unroll):
            state_t = chunk(c * unroll + k, state_t)
        return state_t

    lax.fori_loop(0, nc // unroll, chunks, jnp.zeros((B_DV, B_DK), F32))


def _gla(u, z, w_gate, b_gate, g_gla, batch, seq):
    m = u.shape[0]
    return pl.pallas_call(
        functools.partial(_gla_kernel, seq=seq),
        grid=(batch, B_HEADS),
        in_specs=[pl.BlockSpec((seq, LANES), lambda b, h: (b, 0)),
                  pl.BlockSpec((LANES, B_DK), lambda b, h: (0, h)),
                  pl.BlockSpec((1, B_DK), lambda b, h: (0, h)),
                  pl.BlockSpec((1, B_DV), lambda b, h: (0, h)),
                  pl.BlockSpec((seq, B_DK), lambda b, h: (b, QB0 // B_DK + h)),
                  pl.BlockSpec((seq, B_DK), lambda b, h: (b, KB0 // B_DK + h)),
                  pl.BlockSpec((seq, B_DV), lambda b, h: (b, VB0 // B_DV + h)),
                  pl.BlockSpec((seq, B_DV), lambda b, h: (b, GB0 // B_DV + h))],
        out_specs=pl.BlockSpec((seq, B_DV), lambda b, h: (b, h)),
        out_shape=jax.ShapeDtypeStruct((m, B_WIDTH), BF16),
        compiler_params=_cparams("parallel", "parallel"), name="gla",
    )(z, w_gate, b_gate.reshape(1, -1), g_gla.reshape(1, -1), u, u, u, u)


def _pool_kernel(u_ref, w_ref, sc_ref, o_ref, carry_ref, *, tr):
    ti = pl.program_id(1)

    @pl.when(ti == 0)
    def _():
        carry_ref[...] = jnp.zeros_like(carry_ref)

    x = u_ref[...].astype(F32)
    ext = jnp.concatenate([carry_ref[...], x], axis=0)
    carry_ref[...] = x[tr - POOL_HALO:, :]
    pos = ti * tr + lax.broadcasted_iota(jnp.int32, (tr, 1), 0)
    for g, w in enumerate(POOL_WINDOWS):
        lo, hi = g * POOL_CH, (g + 1) * POOL_CH
        s = ext[:, lo:hi]
        sh = 1
        while sh < w:
            s = s + pltpu.roll(s, sh, axis=0)
            sh *= 2
        cnt = jnp.minimum(pos + 1, w).astype(F32)
        d = s[POOL_HALO:, :] / cnt - x[:, lo:hi]
        y = _dot(d.astype(BF16), w_ref[g]) * sc_ref[:, lo:hi]
        o_ref[:, lo:hi] = y.astype(BF16)


def _pool(uc, w_pool, pool_scale, batch, seq):
    m = uc.shape[0]
    tr = _tile(seq, 512)
    nt = seq // tr
    assert tr >= POOL_HALO and max(POOL_WINDOWS) <= POOL_HALO
    return pl.pallas_call(
        functools.partial(_pool_kernel, tr=tr),
        grid=(batch, nt),
        in_specs=[pl.BlockSpec((tr, C_WIDTH), lambda b, i: (b * nt + i, 0)),
                  pl.BlockSpec((N_POOL, POOL_CH, POOL_CH), lambda b, i: (0, 0, 0)),
                  pl.BlockSpec((1, C_WIDTH), lambda b, i: (0, 0))],
        out_specs=pl.BlockSpec((tr, C_WIDTH), lambda b, i: (b * nt + i, 0)),
        out_shape=jax.ShapeDtypeStruct((m, C_WIDTH), BF16),
        scratch_shapes=[pltpu.VMEM((POOL_HALO, C_WIDTH), F32)],
        compiler_params=_cparams("parallel", "arbitrary"), name="pool",
    )(uc, w_pool, pool_scale.reshape(1, C_WIDTH))


def _mm_out_kernel(a_ref, b_ref, c_ref, wa_ref, wb_ref, wc_ref, o_ref):
    o_ref[...] = (_dot(a_ref[...], wa_ref[...].astype(BF16))
                  + _dot(b_ref[...], wb_ref[...].astype(BF16))
                  + _dot(c_ref[...], wc_ref[...].astype(BF16))).astype(o_ref.dtype)


def _mm_out(oa, ob, oc, w_o, layer):
    m = oa.shape[0]
    n = w_o.shape[2]
    tm = _tile(m, 1024)
    tn = _tile(n, 512)
    assert A_WIDTH % B_WIDTH == 0 and B_WIDTH == C_WIDTH
    return pl.pallas_call(
        _mm_out_kernel, grid=(m // tm, n // tn),
        in_specs=[pl.BlockSpec((tm, A_WIDTH), lambda i, j: (i, 0)),
                  pl.BlockSpec((tm, B_WIDTH), lambda i, j: (i, 0)),
                  pl.BlockSpec((tm, C_WIDTH), lambda i, j: (i, 0)),
                  pl.BlockSpec((None, A_WIDTH, tn), lambda i, j: (layer, 0, j)),
                  pl.BlockSpec((None, B_WIDTH, tn), lambda i, j: (layer, A_WIDTH // B_WIDTH, j)),
                  pl.BlockSpec((None, C_WIDTH, tn), lambda i, j: (layer, A_WIDTH // C_WIDTH + 1, j))],
        out_specs=pl.BlockSpec((tm, tn), lambda i, j: (i, j)),
        out_shape=jax.ShapeDtypeStruct((m, n), BRANCH_DTYPE),
        compiler_params=_cparams("parallel", "parallel"), name="mm_out",
    )(oa, ob, oc, w_o, w_o, w_o)


GELU_C = math.sqrt(2.0 / math.pi)


def _mm_up_kernel(x_ref, wg_ref, wv_ref, cg_ref, cv_ref, bg_ref, bv_ref, o_ref,
                  tail_ref, yg_ref, yv_ref, *, tm, nj, tiles_per_seq):
    i = pl.program_id(0)
    j = pl.program_id(1)
    seq_start = (i % tiles_per_seq) == 0
    pad = SUBLANES

    @pl.when(seq_start)
    def _():
        tail_ref[j] = jnp.zeros((pad, tail_ref.shape[2]), F32)
        tail_ref[nj + j] = jnp.zeros((pad, tail_ref.shape[2]), F32)

    rs0 = min(tm, UP_ROWS)
    pieces = [(r0, rs0) for r0 in range(0, tm, rs0)]
    npieces = len(pieces)
    wg = wg_ref[...].astype(BF16)
    wv = wv_ref[...].astype(BF16)
    yg_ref[0:pad, :] = tail_ref[j]
    yv_ref[0:pad, :] = tail_ref[nj + j]
    cg = cg_ref[...]
    cv = cv_ref[...] * 0.5
    bg = bg_ref[...]
    bv = bv_ref[...] * 0.5

    def matmuls(r, zero):
        r0, rs = pieces[r]
        x = x_ref[r0:r0 + rs, :]
        if zero is not None:
            xb = pltpu.bitcast(x, jnp.uint32)
            x = pltpu.bitcast(xb | jnp.tile(zero, (xb.shape[0] // SUBLANES, xb.shape[1] // LANES)), BF16)
        yg_ref[pad + r0:pad + r0 + rs, :] = _dot(x, wg)
        yv_ref[pad + r0:pad + r0 + rs, :] = _dot(x, wv)

    def conv(y_ref, cw, b, r0, rs):
        return (b + cw[2:3, :] * y_ref[pad + r0:pad + r0 + rs, :]
                + cw[1:2, :] * y_ref[pl.ds(pad + r0 - 1, rs), :]
                + cw[0:1, :] * y_ref[pl.ds(pad + r0 - 2, rs), :])

    def epilogue(r):
        r0, rs = pieces[r]
        zg = conv(yg_ref, cg, bg, r0, rs)
        zv = conv(yv_ref, cv, bv, r0, rs)
        t = jnp.tanh(zg * (GELU_C + (GELU_C * 0.044715) * (zg * zg)))
        out = ((zg + zg * t) * zv).astype(BF16)
        o_ref[r0:r0 + rs, :] = out
        bits = pltpu.bitcast(out, jnp.uint32)
        acc = bits[0:SUBLANES, :]
        for k in range(1, bits.shape[0] // SUBLANES):
            acc = acc | bits[SUBLANES * k:SUBLANES * (k + 1), :]
        red = acc[:, 0:LANES]
        for k in range(1, acc.shape[1] // LANES):
            red = red | acc[:, k * LANES:(k + 1) * LANES]
        return (red >> 16) >> 16

    lag = UP_LAG
    for r in range(min(lag, npieces)):
        matmuls(r, None)
    for r in range(npieces):
        zero = epilogue(r)
        if r + lag < npieces:
            matmuls(r + lag, zero)
    tail_ref[j] = yg_ref[tm:tm + pad, :]
    tail_ref[nj + j] = yv_ref[tm:tm + pad, :]


def _mm_up(hn, w_up, w_conv, b_conv, layer, seq):
    m, d = hn.shape
    f = w_up.shape[2] // 2
    tm = _tile(seq, UP_TM)
    tn = MXU_COLS if f % MXU_COLS == 0 else f
    nj = f // tn
    b3 = b_conv.reshape(b_conv.shape[0], 1, 2 * f)
    return pl.pallas_call(
        functools.partial(_mm_up_kernel, tm=tm, nj=nj, tiles_per_seq=seq // tm),
        grid=(m // tm, nj),
        in_specs=[pl.BlockSpec((tm, d), lambda i, j: (i, 0), pipeline_mode=pl.Buffered(1)),
                  pl.BlockSpec((None, d, tn), lambda i, j: (layer, 0, j)),
                  pl.BlockSpec((None, d, tn), lambda i, j: (layer, 0, nj + j)),
                  pl.BlockSpec((None, CONV_K, tn), lambda i, j: (layer, 0, j)),
                  pl.BlockSpec((None, CONV_K, tn), lambda i, j: (layer, 0, nj + j)),
                  pl.BlockSpec((None, 1, tn), lambda i, j: (layer, 0, j)),
                  pl.BlockSpec((None, 1, tn), lambda i, j: (layer, 0, nj + j))],
        out_specs=pl.BlockSpec((tm, tn), lambda i, j: (i, j)),
        out_shape=jax.ShapeDtypeStruct((m, f), BF16),
        scratch_shapes=[pltpu.VMEM((2 * nj, SUBLANES, tn), F32),
                        pltpu.VMEM((tm + SUBLANES, tn), F32), pltpu.VMEM((tm + SUBLANES, tn), F32)],
        compiler_params=_cparams("arbitrary", "arbitrary"), name="mm_up",
    )(hn, w_up, w_up, w_conv, w_conv, b3, b3)


def _mm_kernel(x_ref, w_ref, o_ref):
    o_ref[...] = _dot(x_ref[...], w_ref[...].astype(BF16)).astype(o_ref.dtype)


def _mm_down(x, w, layer):
    m, k = x.shape
    n = w.shape[2]
    tm = _tile(m, 1024)
    tn = _tile(n, MXU_COLS)
    return pl.pallas_call(
        _mm_kernel, grid=(m // tm, n // tn),
        in_specs=[pl.BlockSpec((tm, k), lambda i, j: (i, 0), pipeline_mode=pl.Buffered(1)),
                  pl.BlockSpec((None, k, tn), lambda i, j: (layer, 0, j))],
        out_specs=pl.BlockSpec((tm, tn), lambda i, j: (i, j)),
        out_shape=jax.ShapeDtypeStruct((m, n), BRANCH_DTYPE),
        compiler_params=_cparams("parallel", "parallel"), name="mm_down",
    )(x, w)


def kernel(x, w_in, w_gate_lr2, b_gate, lam_q1, lam_k1, lam_q2, lam_k2, g_subln, g_gla, w_pool, pool_scale, w_o, w_up, w_conv, b_conv, w_down, g_pre_mix, g_post_mix, g_pre_ffn, g_post_ffn):
    batch, seq, d = x.shape
    depth = w_in.shape[0]
    m = batch * seq
    assert seq % CHUNK == 0 and w_in.shape[2] == U_COLS + GATE_RANK
    assert UC0 == GB0 + B_WIDTH

    w_in_t = jnp.swapaxes(w_in, 1, 2)
    w_pool_b = w_pool.astype(BF16)

    h = x.reshape(m, d)
    branch = None
    for l in range(depth):
        w_gate = jnp.pad(w_gate_lr2[l], ((0, LANES - GATE_RANK), (0, 0))).astype(BF16)

        if l == 0:
            hn = _norm_first(h, g_pre_mix[0])
        else:
            h, hn = _resnorm(h, branch, g_post_ffn[l - 1], g_pre_mix[l])
        u = _mm_in(hn, w_in_t, l)
        z, uc = _mm_tail(hn, w_in_t, l)
        lam_init = 0.8 - 0.6 * math.exp(-0.3 * l)
        oa = _att(u, lam_q1[l], lam_k1[l], lam_q2[l], lam_k2[l], g_subln[l], lam_init, batch, seq)
        ob = _gla(u, z, w_gate, b_gate[l], g_gla[l], batch, seq)
        oc = _pool(uc, w_pool_b[l], pool_scale[l], batch, seq)
        mix = _mm_out(oa, ob, oc, w_o, l)

        h, hn = _resnorm(h, mix, g_post_mix[l], g_pre_ffn[l])
        f_in = _mm_up(hn, w_up, w_conv, b_conv, l, seq)
        branch = _mm_down(f_in, w_down, l)
    h = _resnorm_last(h, branch, g_post_ffn[depth - 1])
    return h.reshape(batch, seq, d)
```

```python
import functools
import math

import jax
import jax.numpy as jnp
from jax import lax
from jax.experimental import pallas as pl
from jax.experimental.pallas import tpu as pltpu

F32 = jnp.float32
BF16 = jnp.bfloat16
BRANCH_DTYPE = jnp.bfloat16

CHUNK = 64
EPS = 1e-6
A_HEADS = 8
A_DH = 128
A_WIDTH = A_HEADS * 2 * A_DH
B_HEADS = 4
B_DK = 128
B_DV = 256
B_WIDTH = B_HEADS * B_DV
GATE_RANK = 16
GATE_TAU = 16.0
POOL_WINDOWS = (2, 4, 8, 16)
N_POOL = len(POOL_WINDOWS)
POOL_CH = 256
C_WIDTH = N_POOL * POOL_CH
CONV_K = 3

QA0 = 0
KA0 = A_WIDTH
VA0 = 2 * A_WIDTH
QB0 = 3 * A_WIDTH
KB0 = QB0 + B_HEADS * B_DK
VB0 = KB0 + B_HEADS * B_DK
GB0 = VB0 + B_WIDTH
UC0 = GB0 + B_WIDTH
U_COLS = UC0 + C_WIDTH

LANES = 128
SUBLANES = 8
MXU_COLS = 256
VMEM_LIMIT = 56 * 1024 * 1024
POOL_HALO = 16
NEG_BIG = -1e30
LOG2E = math.log2(math.e)
NORM_ROWS = 256
ATT_TQ = 512
ATT_GROUP = 2048
UP_TM = 2048
UP_ROWS = 256
UP_LAG = 2
GLA_UNROLL = 8
GLA_SUB = 16
TAIL_ROWS = 1152


def _cparams(*sem):
    return pltpu.CompilerParams(dimension_semantics=sem, vmem_limit_bytes=VMEM_LIMIT)


def _dot(a, b):
    return jnp.dot(a, b, preferred_element_type=F32)


def _dot_nt(a, b):
    return lax.dot_general(a, b, (((1,), (1,)), ((), ())), preferred_element_type=F32)


def _dot_tn(a, b):
    return lax.dot_general(a, b, (((0,), (0,)), ((), ())), preferred_element_type=F32)


def _rms(x, g):
    ms = jnp.mean(x * x, axis=-1, keepdims=True)
    return x * lax.rsqrt(ms + EPS) * g


def _tile(n, want):
    t = min(n, want)
    assert n % t == 0, (n, want)
    return t


def _norm_first_kernel(h_ref, gpre_ref, hn_ref):
    hn_ref[...] = _rms(h_ref[...], gpre_ref[...]).astype(BF16)


def _resnorm_kernel(h_ref, br_ref, gpost_ref, gpre_ref, hout_ref, hn_ref):
    h = h_ref[...] + _rms(br_ref[...].astype(F32), gpost_ref[...])
    hout_ref[...] = h
    hn_ref[...] = _rms(h, gpre_ref[...]).astype(BF16)


def _resnorm_last_kernel(h_ref, br_ref, gpost_ref, hout_ref):
    hout_ref[...] = h_ref[...] + _rms(br_ref[...].astype(F32), gpost_ref[...])


def _norm_first(h, gpre):
    m, d = h.shape
    tr = _tile(m, NORM_ROWS)
    row = pl.BlockSpec((tr, d), lambda i: (i, 0))
    vec = pl.BlockSpec((1, d), lambda i: (0, 0))
    return pl.pallas_call(
        _norm_first_kernel, grid=(m // tr,), in_specs=[row, vec], out_specs=row,
        out_shape=jax.ShapeDtypeStruct((m, d), BF16),
        compiler_params=_cparams("parallel"), name="norm_first",
    )(h, gpre.reshape(1, d))


def _resnorm(h, br, gpost, gpre):
    m, d = h.shape
    tr = _tile(m, NORM_ROWS)
    row = pl.BlockSpec((tr, d), lambda i: (i, 0))
    vec = pl.BlockSpec((1, d), lambda i: (0, 0))
    return pl.pallas_call(
        _resnorm_kernel, grid=(m // tr,), in_specs=[row, row, vec, vec], out_specs=[row, row],
        out_shape=[jax.ShapeDtypeStruct((m, d), F32), jax.ShapeDtypeStruct((m, d), BF16)],
        compiler_params=_cparams("parallel"), name="resnorm",
    )(h, br, gpost.reshape(1, d), gpre.reshape(1, d))


def _resnorm_last(h, br, gpost):
    m, d = h.shape
    tr = _tile(m, NORM_ROWS)
    row = pl.BlockSpec((tr, d), lambda i: (i, 0))
    vec = pl.BlockSpec((1, d), lambda i: (0, 0))
    return pl.pallas_call(
        _resnorm_last_kernel, grid=(m // tr,), in_specs=[row, row, vec], out_specs=row,
        out_shape=jax.ShapeDtypeStruct((m, d), F32),
        compiler_params=_cparams("parallel"), name="resnorm_last",
    )(h, br, gpost.reshape(1, d))


def _mm_in_kernel(x_ref, w_ref, u_ref):
    u_ref[...] = _dot_nt(x_ref[...], w_ref[...].astype(BF16)).astype(BF16)


def _mm_in(hn, w_in_t, layer):
    m, d = hn.shape
    tm = _tile(m, 1024)
    tn = 512
    assert UC0 % tn == 0
    return pl.pallas_call(
        _mm_in_kernel, grid=(m // tm, UC0 // tn),
        in_specs=[pl.BlockSpec((tm, d), lambda i, j: (i, 0)),
                  pl.BlockSpec((None, tn, d), lambda i, j: (layer, j, 0))],
        out_specs=pl.BlockSpec((tm, tn), lambda i, j: (i, j)),
        out_shape=jax.ShapeDtypeStruct((m, UC0), BF16),
        compiler_params=_cparams("parallel", "arbitrary"), name="mm_in",
    )(hn, w_in_t)


def _mm_tail_kernel(x_ref, w_ref, z_ref, uc_ref):
    x = x_ref[...]
    z_ref[...] = _dot_nt(x, w_ref[0:LANES, :].astype(BF16))
    uc_ref[...] = _dot_nt(x, w_ref[GATE_RANK:GATE_RANK + C_WIDTH, :].astype(BF16)).astype(BF16)


def _mm_tail(hn, w_in_t, layer):
    m, d = hn.shape
    tm = _tile(m, 1024)
    rows = TAIL_ROWS
    assert UC0 % rows == 0 and rows >= GATE_RANK + C_WIDTH and GATE_RANK % SUBLANES == 0
    return pl.pallas_call(
        _mm_tail_kernel, grid=(m // tm,),
        in_specs=[pl.BlockSpec((tm, d), lambda i: (i, 0)),
                  pl.BlockSpec((None, rows, d), lambda i: (layer, UC0 // rows, 0),
                               pipeline_mode=pl.Buffered(1))],
        out_specs=[pl.BlockSpec((tm, LANES), lambda i: (i, 0)),
                   pl.BlockSpec((tm, C_WIDTH), lambda i: (i, 0))],
        out_shape=[jax.ShapeDtypeStruct((m, LANES), F32), jax.ShapeDtypeStruct((m, C_WIDTH), BF16)],
        compiler_params=_cparams("arbitrary"), name="mm_tail",
    )(hn, w_in_t)


def _att_kernel(lq1_ref, lk1_ref, lq2_ref, lk2_ref, g_ref, q_ref, k_ref, v_ref, o_ref,
                s1_ref, s2_ref, acc1_ref, acc2_ref, bias_ref, stat_ref, *, lam_init, t, G):
    head = pl.program_id(1)
    qi = pl.program_id(2)
    qscale = LOG2E / math.sqrt(A_DH)
    slope = lax.bitcast_convert_type(
        jnp.full((1, 1), (126 - head) << 23, jnp.int32), F32) * LOG2E

    q = q_ref[...].astype(F32) * qscale
    qt1 = q[:, :A_DH].T.astype(BF16)
    qt2 = q[:, A_DH:].T.astype(BF16)

    def fold8(x, op):
        return op(x.reshape(x.shape[0] // SUBLANES, SUBLANES, t), axis=0)

    nt = G // t

    @pl.when(qi == 0)
    def _():
        cshift = CHUNK.bit_length() - 1
        kd = lax.broadcasted_iota(jnp.int32, (t, t), 0)
        rd = lax.broadcasted_iota(jnp.int32, (t, t), 1)
        corr = slope * (rd - jnp.abs(rd - kd) - kd).astype(F32)
        diag = jnp.where((kd >> cshift) <= (rd >> cshift), corr, NEG_BIG)
        for i in range(nt):
            base = slope * (kd + i * t).astype(F32)
            bias_ref[0, i * t:(i + 1) * t, :] = base
            for j in range(i, nt):
                bias_ref[1 + j, i * t:(i + 1) * t, :] = base + diag if i == j else base

    qs = qi * t
    gl = qs // G
    jd = (qs - gl * G) // t

    def offset(g):
        return slope * (g * G - qs).astype(F32)

    l0 = pl.multiple_of(gl * G, G)

    def scores(k0, rows, bias, off, carry):
        m1, m2 = carry
        kt = k_ref[pl.ds(k0, rows), :]
        s1 = _dot(kt[:, :A_DH], qt1) + bias
        s2 = _dot(kt[:, A_DH:], qt2) + bias
        s1_ref[pl.ds(k0, rows), :] = s1
        s2_ref[pl.ds(k0, rows), :] = s2
        return (jnp.maximum(m1, fold8(s1, jnp.max) + off), jnp.maximum(m2, fold8(s2, jnp.max) + off))

    def score_body(g, carry):
        return scores(pl.multiple_of(g * G, G), G, bias_ref[0], offset(g), carry)

    neg8 = jnp.full((SUBLANES, t), NEG_BIG, F32)
    m_groups = lax.fori_loop(0, gl, score_body, (neg8, neg8))
    for j in range(nt):
        @pl.when(jd == j)
        def _():
            rows = (j + 1) * t
            m1, m2 = scores(l0, rows, bias_ref[1 + j, 0:rows, :], offset(gl), m_groups)
            stat_ref[0] = m1
            stat_ref[1] = m2
    m1 = jnp.max(stat_ref[0], axis=0, keepdims=True)
    m2 = jnp.max(stat_ref[1], axis=0, keepdims=True)

    acc1_ref[...] = jnp.zeros_like(acc1_ref)
    acc2_ref[...] = jnp.zeros_like(acc2_ref)

    def probs(k0, rows, off, carry):
        l1, l2 = carry
        vt = v_ref[pl.ds(k0, rows), :]
        p1 = jnp.exp2(s1_ref[pl.ds(k0, rows), :] - (m1 - off))
        p2 = jnp.exp2(s2_ref[pl.ds(k0, rows), :] - (m2 - off))
        acc1_ref[...] += _dot_tn(vt, p1.astype(BF16))
        acc2_ref[...] += _dot_tn(vt, p2.astype(BF16))
        return l1 + fold8(p1, jnp.sum), l2 + fold8(p2, jnp.sum)

    def pv_body(g, carry):
        return probs(pl.multiple_of(g * G, G), G, offset(g), carry)

    zero8 = jnp.zeros((SUBLANES, t), F32)
    l_groups = lax.fori_loop(0, gl, pv_body, (zero8, zero8))
    for j in range(nt):
        @pl.when(jd == j)
        def _():
            l1, l2 = probs(l0, (j + 1) * t, offset(gl), l_groups)
            stat_ref[0] = l1
            stat_ref[1] = l2
    l1 = jnp.sum(stat_ref[0], axis=0, keepdims=True)
    l2 = jnp.sum(stat_ref[1], axis=0, keepdims=True)

    lam = (jnp.exp(jnp.sum(lq1_ref[...] * lk1_ref[...], axis=-1, keepdims=True))
           - jnp.exp(jnp.sum(lq2_ref[...] * lk2_ref[...], axis=-1, keepdims=True))
           + lam_init)
    ot = acc1_ref[...] * (1.0 / l1) - acc2_ref[...] * (lam / l2)
    ms = jnp.mean(ot * ot, axis=0, keepdims=True)
    o_ref[...] = (ot * lax.rsqrt(ms + EPS) * (g_ref[...] * (1.0 - lam_init))).astype(BF16)


def _att(u, lq1, lk1, lq2, lk2, g_subln, lam_init, batch, seq):
    m = u.shape[0]
    t = _tile(seq, ATT_TQ)
    G = _tile(seq, ATT_GROUP)
    nq = seq // t
    e = 2 * A_DH
    assert G % t == 0
    vec = pl.BlockSpec((1, A_DH), lambda b, h, i: (0, 0))
    return pl.pallas_call(
        functools.partial(_att_kernel, lam_init=lam_init, t=t, G=G),
        grid=(batch, A_HEADS, nq),
        in_specs=[vec, vec, vec, vec,
                  pl.BlockSpec((e, 1), lambda b, h, i: (0, 0)),
                  pl.BlockSpec((t, e), lambda b, h, i: (b * nq + i, QA0 // e + h)),
                  pl.BlockSpec((seq, e), lambda b, h, i: (b, KA0 // e + h)),
                  pl.BlockSpec((seq, e), lambda b, h, i: (b, VA0 // e + h))],
        out_specs=pl.BlockSpec((e, t), lambda b, h, i: (h, b * nq + i)),
        out_shape=jax.ShapeDtypeStruct((A_WIDTH, m), BF16),
        scratch_shapes=[pltpu.VMEM((seq, t), F32), pltpu.VMEM((seq, t), F32),
                        pltpu.VMEM((e, t), F32), pltpu.VMEM((e, t), F32),
                        pltpu.VMEM((1 + G // t, G, t), F32), pltpu.VMEM((2, SUBLANES, t), F32)],
        compiler_params=_cparams("arbitrary", "arbitrary", "arbitrary"), name="diff_attn",
    )(lq1.reshape(1, A_DH), lk1.reshape(1, A_DH), lq2.reshape(1, A_DH), lk2.reshape(1, A_DH),
      g_subln.reshape(e, 1), u, u, u)


def _gla_kernel(z_ref, wg_ref, bg_ref, gn_ref, q_ref, k_ref, v_ref, gb_ref, o_ref, *, seq):
    L = CHUNK
    wg = wg_ref[...]
    bg = bg_ref[...]
    gn = gn_ref[...]
    rows = lax.broadcasted_iota(jnp.int32, (L, B_DK), 0)
    ti = lax.broadcasted_iota(jnp.int32, (L, L), 0)
    si = lax.broadcasted_iota(jnp.int32, (L, L), 1)
    SUB = GLA_SUB
    assert L == 4 * SUB
    sshift = SUB.bit_length() - 1
    bi = ti >> sshift
    bj = si >> sshift
    sd = si - (ti - (ti & (SUB - 1)))
    m_diag = (bi == bj) & (si <= ti)
    m_near = (bi == bj + 1) & ((bi & 1) == 1)
    m_far = (bi >= 2) & (bj <= 1)
    qscale = B_DK ** -0.5

    def chunk(c, state_t):
        r0 = pl.multiple_of(c * L, L)
        x = _dot(z_ref[pl.ds(r0, L), :].astype(BF16), wg) + bg
        la = (jnp.minimum(x, 0.0) - jnp.log(1.0 + jnp.exp(-jnp.abs(x)))) * (1.0 / GATE_TAU)
        b = la
        sh = 1
        while sh < L:
            b = b + jnp.where(rows >= sh, pltpu.roll(b, sh, axis=0), 0.0)
            sh *= 2
        b = b * LOG2E
        b_last = b[L - 1:L, :]
        qf = q_ref[pl.ds(r0, L), :].astype(F32) * qscale
        kc = k_ref[pl.ds(r0, L), :]
        kf = kc.astype(F32)
        vc = v_ref[pl.ds(r0, L), :]

        def rescaled(ref):
            qr = (qf * jnp.exp2(jnp.minimum(b - ref, 0.0))).astype(BF16)
            kr = (kf * jnp.exp2(jnp.minimum(ref - b, 0.0))).astype(BF16)
            return _dot_nt(qr, kr)

        half = L // 2
        a_far = rescaled(b[half - 1:half, :])
        ref_near = jnp.where(rows < half, b[SUB - 1:SUB, :], b[half + SUB - 1:half + SUB, :])
        a_near = rescaled(ref_near)
        a = jnp.zeros((L, L), F32)
        for s in range(SUB):
            ref = jnp.concatenate(
                [jnp.broadcast_to(b[i * SUB + s:i * SUB + s + 1, :], (SUB, B_DK)) for i in range(L // SUB)],
                axis=0)
            e = jnp.exp2(jnp.minimum(b - ref, 0.0))
            rs = _dot_nt((qf * e).astype(BF16), kc)
            a = jnp.where(sd == s, rs, a)
        a = jnp.where(m_far, a_far, jnp.where(m_near, a_near, jnp.where(m_diag, a, 0.0)))

        qe = (qf * jnp.exp2(b)).astype(BF16)
        o = _dot(a.astype(BF16), vc) + _dot_nt(qe, state_t.astype(BF16))
        ke = (kf * jnp.exp2(b_last - b)).astype(BF16)
        new_state = jnp.exp2(b_last) * state_t + _dot_tn(vc, ke)

        gate = gb_ref[pl.ds(r0, L), :].astype(F32)
        y = _rms(o, gn) * (gate / (1.0 + jnp.exp(-gate)))
        o_ref[pl.ds(r0, L), :] = y.astype(BF16)
        return new_state

    nc = seq // L
    unroll = GLA_UNROLL if nc % GLA_UNROLL == 0 else 1

    def chunks(c, state_t):
        for k in range(unroll):
            state_t = chunk(c * unroll + k, state_t)
        return state_t

    lax.fori_loop(0, nc // unroll, chunks, jnp.zeros((B_DV, B_DK), F32))


def _gla(u, z, w_gate, b_gate, g_gla, batch, seq):
    m = u.shape[0]
    return pl.pallas_call(
        functools.partial(_gla_kernel, seq=seq),
        grid=(batch, B_HEADS),
        in_specs=[pl.BlockSpec((seq, LANES), lambda b, h: (b, 0)),
                  pl.BlockSpec((LANES, B_DK), lambda b, h: (0, h)),
                  pl.BlockSpec((1, B_DK), lambda b, h: (0, h)),
                  pl.BlockSpec((1, B_DV), lambda b, h: (0, h)),
                  pl.BlockSpec((seq, B_DK), lambda b, h: (b, QB0 // B_DK + h)),
                  pl.BlockSpec((seq, B_DK), lambda b, h: (b, KB0 // B_DK + h)),
                  pl.BlockSpec((seq, B_DV), lambda b, h: (b, VB0 // B_DV + h)),
                  pl.BlockSpec((seq, B_DV), lambda b, h: (b, GB0 // B_DV + h))],
        out_specs=pl.BlockSpec((seq, B_DV), lambda b, h: (b, h)),
        out_shape=jax.ShapeDtypeStruct((m, B_WIDTH), BF16),
        compiler_params=_cparams("parallel", "parallel"), name="gla",
    )(z, w_gate, b_gate.reshape(1, -1), g_gla.reshape(1, -1), u, u, u, u)


def _pool_kernel(u_ref, w_ref, sc_ref, o_ref, carry_ref, *, tr):
    ti = pl.program_id(1)

    @pl.when(ti == 0)
    def _():
        carry_ref[...] = jnp.zeros_like(carry_ref)

    x = u_ref[...].astype(F32)
    ext = jnp.concatenate([carry_ref[...], x], axis=0)
    carry_ref[...] = x[tr - POOL_HALO:, :]
    pos = ti * tr + lax.broadcasted_iota(jnp.int32, (tr, 1), 0)
    for g, w in enumerate(POOL_WINDOWS):
        lo, hi = g * POOL_CH, (g + 1) * POOL_CH
        s = ext[:, lo:hi]
        sh = 1
        while sh < w:
            s = s + pltpu.roll(s, sh, axis=0)
            sh *= 2
        cnt = jnp.minimum(pos + 1, w).astype(F32)
        d = s[POOL_HALO:, :] / cnt - x[:, lo:hi]
        y = _dot(d.astype(BF16), w_ref[g]) * sc_ref[:, lo:hi]
        o_ref[:, lo:hi] = y.astype(BF16)


def _pool(uc, w_pool, pool_scale, batch, seq):
    m = uc.shape[0]
    tr = _tile(seq, 512)
    nt = seq // tr
    assert tr >= POOL_HALO and max(POOL_WINDOWS) <= POOL_HALO
    return pl.pallas_call(
        functools.partial(_pool_kernel, tr=tr),
        grid=(batch, nt),
        in_specs=[pl.BlockSpec((tr, C_WIDTH), lambda b, i: (b * nt + i, 0)),
                  pl.BlockSpec((N_POOL, POOL_CH, POOL_CH), lambda b, i: (0, 0, 0)),
                  pl.BlockSpec((1, C_WIDTH), lambda b, i: (0, 0))],
        out_specs=pl.BlockSpec((tr, C_WIDTH), lambda b, i: (b * nt + i, 0)),
        out_shape=jax.ShapeDtypeStruct((m, C_WIDTH), BF16),
        scratch_shapes=[pltpu.VMEM((POOL_HALO, C_WIDTH), F32)],
        compiler_params=_cparams("parallel", "arbitrary"), name="pool",
    )(uc, w_pool, pool_scale.reshape(1, C_WIDTH))


def _mm_out_kernel(a_ref, b_ref, c_ref, wa_ref, wb_ref, wc_ref, o_ref):
    o_ref[...] = (_dot_tn(a_ref[...], wa_ref[...].astype(BF16))
                  + _dot(b_ref[...], wb_ref[...].astype(BF16))
                  + _dot(c_ref[...], wc_ref[...].astype(BF16))).astype(o_ref.dtype)


def _mm_out(oa, ob, oc, w_o, layer):
    m = ob.shape[0]
    n = w_o.shape[2]
    tm = _tile(m, 1024)
    tn = _tile(n, 512)
    assert A_WIDTH % B_WIDTH == 0 and B_WIDTH == C_WIDTH
    return pl.pallas_call(
        _mm_out_kernel, grid=(m // tm, n // tn),
        in_specs=[pl.BlockSpec((A_WIDTH, tm), lambda i, j: (0, i)),
                  pl.BlockSpec((tm, B_WIDTH), lambda i, j: (i, 0)),
                  pl.BlockSpec((tm, C_WIDTH), lambda i, j: (i, 0)),
                  pl.BlockSpec((None, A_WIDTH, tn), lambda i, j: (layer, 0, j)),
                  pl.BlockSpec((None, B_WIDTH, tn), lambda i, j: (layer, A_WIDTH // B_WIDTH, j)),
                  pl.BlockSpec((None, C_WIDTH, tn), lambda i, j: (layer, A_WIDTH // C_WIDTH + 1, j))],
        out_specs=pl.BlockSpec((tm, tn), lambda i, j: (i, j)),
        out_shape=jax.ShapeDtypeStruct((m, n), BRANCH_DTYPE),
        compiler_params=_cparams("parallel", "parallel"), name="mm_out",
    )(oa, ob, oc, w_o, w_o, w_o)


GELU_C = math.sqrt(2.0 / math.pi)


def _mm_up_kernel(x_ref, wg_ref, wv_ref, cg_ref, cv_ref, bg_ref, bv_ref, o_ref,
                  tail_ref, yg_ref, yv_ref, *, tm, nj, tiles_per_seq):
    i = pl.program_id(0)
    j = pl.program_id(1)
    seq_start = (i % tiles_per_seq) == 0
    pad = SUBLANES

    @pl.when(seq_start)
    def _():
        tail_ref[j] = jnp.zeros((pad, tail_ref.shape[2]), F32)
        tail_ref[nj + j] = jnp.zeros((pad, tail_ref.shape[2]), F32)

    rs0 = min(tm, UP_ROWS)
    pieces = [(r0, rs0) for r0 in range(0, tm, rs0)]
    npieces = len(pieces)
    wg = wg_ref[...].astype(BF16)
    wv = wv_ref[...].astype(BF16)
    yg_ref[0:pad, :] = tail_ref[j]
    yv_ref[0:pad, :] = tail_ref[nj + j]
    cg = cg_ref[...]
    cv = cv_ref[...] * 0.5
    bg = bg_ref[...]
    bv = bv_ref[...] * 0.5

    def matmuls(r, zero):
        r0, rs = pieces[r]
        x = x_ref[r0:r0 + rs, :]
        if zero is not None:
            xb = pltpu.bitcast(x, jnp.uint32)
            x = pltpu.bitcast(xb | jnp.tile(zero, (xb.shape[0] // SUBLANES, xb.shape[1] // LANES)), BF16)
        yg_ref[pad + r0:pad + r0 + rs, :] = _dot(x, wg)
        yv_ref[pad + r0:pad + r0 + rs, :] = _dot(x, wv)

    def conv(y_ref, cw, b, r0, rs):
        return (b + cw[2:3, :] * y_ref[pad + r0:pad + r0 + rs, :]
                + cw[1:2, :] * y_ref[pl.ds(pad + r0 - 1, rs), :]
                + cw[0:1, :] * y_ref[pl.ds(pad + r0 - 2, rs), :])

    def epilogue(r):
        r0, rs = pieces[r]
        zg = conv(yg_ref, cg, bg, r0, rs)
        zv = conv(yv_ref, cv, bv, r0, rs)
        t = jnp.tanh(zg * (GELU_C + (GELU_C * 0.044715) * (zg * zg)))
        out = ((zg + zg * t) * zv).astype(BF16)
        o_ref[r0:r0 + rs, :] = out
        bits = pltpu.bitcast(out, jnp.uint32)
        acc = bits[0:SUBLANES, :]
        for k in range(1, bits.shape[0] // SUBLANES):
            acc = acc | bits[SUBLANES * k:SUBLANES * (k + 1), :]
        red = acc[:, 0:LANES]
        for k in range(1, acc.shape[1] // LANES):
            red = red | acc[:, k * LANES:(k + 1) * LANES]
        return (red >> 16) >> 16

    lag = UP_LAG
    for r in range(min(lag, npieces)):
        matmuls(r, None)
    for r in range(npieces):
        zero = epilogue(r)
        if r + lag < npieces:
            matmuls(r + lag, zero)
    tail_ref[j] = yg_ref[tm:tm + pad, :]
    tail_ref[nj + j] = yv_ref[tm:tm + pad, :]


def _mm_up(hn, w_up, w_conv, b_conv, layer, seq):
    m, d = hn.shape
    f = w_up.shape[2] // 2
    tm = _tile(seq, UP_TM)
    tn = MXU_COLS if f % MXU_COLS == 0 else f
    nj = f // tn
    b3 = b_conv.reshape(b_conv.shape[0], 1, 2 * f)
    return pl.pallas_call(
        functools.partial(_mm_up_kernel, tm=tm, nj=nj, tiles_per_seq=seq // tm),
        grid=(m // tm, nj),
        in_specs=[pl.BlockSpec((tm, d), lambda i, j: (i, 0), pipeline_mode=pl.Buffered(1)),
                  pl.BlockSpec((None, d, tn), lambda i, j: (layer, 0, j)),
                  pl.BlockSpec((None, d, tn), lambda i, j: (layer, 0, nj + j)),
                  pl.BlockSpec((None, CONV_K, tn), lambda i, j: (layer, 0, j)),
                  pl.BlockSpec((None, CONV_K, tn), lambda i, j: (layer, 0, nj + j)),
                  pl.BlockSpec((None, 1, tn), lambda i, j: (layer, 0, j)),
                  pl.BlockSpec((None, 1, tn), lambda i, j: (layer, 0, nj + j))],
        out_specs=pl.BlockSpec((tm, tn), lambda i, j: (i, j)),
        out_shape=jax.ShapeDtypeStruct((m, f), BF16),
        scratch_shapes=[pltpu.VMEM((2 * nj, SUBLANES, tn), F32),
                        pltpu.VMEM((tm + SUBLANES, tn), F32), pltpu.VMEM((tm + SUBLANES, tn), F32)],
        compiler_params=_cparams("arbitrary", "arbitrary"), name="mm_up",
    )(hn, w_up, w_up, w_conv, w_conv, b3, b3)


def _mm_kernel(x_ref, w_ref, o_ref):
    o_ref[...] = _dot(x_ref[...], w_ref[...].astype(BF16)).astype(o_ref.dtype)


def _mm_down(x, w, layer):
    m, k = x.shape
    n = w.shape[2]
    tm = _tile(m, 1024)
    tn = _tile(n, MXU_COLS)
    return pl.pallas_call(
        _mm_kernel, grid=(m // tm, n // tn),
        in_specs=[pl.BlockSpec((tm, k), lambda i, j: (i, 0), pipeline_mode=pl.Buffered(1)),
                  pl.BlockSpec((None, k, tn), lambda i, j: (layer, 0, j))],
        out_specs=pl.BlockSpec((tm, tn), lambda i, j: (i, j)),
        out_shape=jax.ShapeDtypeStruct((m, n), BRANCH_DTYPE),
        compiler_params=_cparams("parallel", "parallel"), name="mm_down",
    )(x, w)


def kernel(x, w_in, w_gate_lr2, b_gate, lam_q1, lam_k1, lam_q2, lam_k2, g_subln, g_gla, w_pool, pool_scale, w_o, w_up, w_conv, b_conv, w_down, g_pre_mix, g_post_mix, g_pre_ffn, g_post_ffn):
    batch, seq, d = x.shape
    depth = w_in.shape[0]
    m = batch * seq
    assert seq % CHUNK == 0 and w_in.shape[2] == U_COLS + GATE_RANK
    assert UC0 == GB0 + B_WIDTH

    w_in_t = jnp.swapaxes(w_in, 1, 2)
    w_pool_b = w_pool.astype(BF16)

    h = x.reshape(m, d)
    branch = None
    for l in range(depth):
        w_gate = jnp.pad(w_gate_lr2[l], ((0, LANES - GATE_RANK), (0, 0))).astype(BF16)

        if l == 0:
            hn = _norm_first(h, g_pre_mix[0])
        else:
            h, hn = _resnorm(h, branch, g_post_ffn[l - 1], g_pre_mix[l])
        u = _mm_in(hn, w_in_t, l)
        z, uc = _mm_tail(hn, w_in_t, l)
        lam_init = 0.8 - 0.6 * math.exp(-0.3 * l)
        oa = _att(u, lam_q1[l], lam_k1[l], lam_q2[l], lam_k2[l], g_subln[l], lam_init, batch, seq)
        ob = _gla(u, z, w_gate, b_gate[l], g_gla[l], batch, seq)
        oc = _pool(uc, w_pool_b[l], pool_scale[l], batch, seq)
        mix = _mm_out(oa, ob, oc, w_o, l)

        h, hn = _resnorm(h, mix, g_post_mix[l], g_pre_ffn[l])
        f_in = _mm_up(hn, w_up, w_conv, b_conv, l, seq)
        branch = _mm_down(f_in, w_down, l)
    h = _resnorm_last(h, branch, g_post_ffn[depth - 1])
    return h.reshape(batch, seq, d)
```

```python
import functools
import math

import jax
import jax.numpy as jnp
from jax import lax
from jax.experimental import pallas as pl
from jax.experimental.pallas import tpu as pltpu

F32 = jnp.float32
BF16 = jnp.bfloat16
BRANCH_DTYPE = jnp.bfloat16

CHUNK = 64
EPS = 1e-6
A_HEADS = 8
A_DH = 128
A_WIDTH = A_HEADS * 2 * A_DH
B_HEADS = 4
B_DK = 128
B_DV = 256
B_WIDTH = B_HEADS * B_DV
GATE_RANK = 16
GATE_TAU = 16.0
POOL_WINDOWS = (2, 4, 8, 16)
N_POOL = len(POOL_WINDOWS)
POOL_CH = 256
C_WIDTH = N_POOL * POOL_CH
CONV_K = 3

QA0 = 0
KA0 = A_WIDTH
VA0 = 2 * A_WIDTH
QB0 = 3 * A_WIDTH
KB0 = QB0 + B_HEADS * B_DK
VB0 = KB0 + B_HEADS * B_DK
GB0 = VB0 + B_WIDTH
UC0 = GB0 + B_WIDTH
U_COLS = UC0 + C_WIDTH

LANES = 128
SUBLANES = 8
MXU_COLS = 256
VMEM_LIMIT = 56 * 1024 * 1024
POOL_HALO = 16
NEG_BIG = -1e30
LOG2E = math.log2(math.e)
NORM_ROWS = 256
ATT_TQ = 512
ATT_GROUP = 2048
UP_TM = 2048
UP_ROWS = 256
UP_LAG = 2
GLA_UNROLL = 8
GLA_SUB = 16
TAIL_ROWS = 1152


def _cparams(*sem):
    return pltpu.CompilerParams(dimension_semantics=sem, vmem_limit_bytes=VMEM_LIMIT)


def _dot(a, b):
    return jnp.dot(a, b, preferred_element_type=F32)


def _dot_nt(a, b):
    return lax.dot_general(a, b, (((1,), (1,)), ((), ())), preferred_element_type=F32)


def _dot_tn(a, b):
    return lax.dot_general(a, b, (((0,), (0,)), ((), ())), preferred_element_type=F32)


def _rms(x, g):
    ms = jnp.mean(x * x, axis=-1, keepdims=True)
    return x * lax.rsqrt(ms + EPS) * g


def _tile(n, want):
    t = min(n, want)
    assert n % t == 0, (n, want)
    return t


def _norm_first_kernel(h_ref, gpre_ref, hn_ref):
    hn_ref[...] = _rms(h_ref[...], gpre_ref[...]).astype(BF16)


def _resnorm_kernel(h_ref, br_ref, gpost_ref, gpre_ref, hout_ref, hn_ref):
    h = h_ref[...] + _rms(br_ref[...].astype(F32), gpost_ref[...])
    hout_ref[...] = h
    hn_ref[...] = _rms(h, gpre_ref[...]).astype(BF16)


def _resnorm_last_kernel(h_ref, br_ref, gpost_ref, hout_ref):
    hout_ref[...] = h_ref[...] + _rms(br_ref[...].astype(F32), gpost_ref[...])


def _norm_first(h, gpre):
    m, d = h.shape
    tr = _tile(m, NORM_ROWS)
    row = pl.BlockSpec((tr, d), lambda i: (i, 0))
    vec = pl.BlockSpec((1, d), lambda i: (0, 0))
    return pl.pallas_call(
        _norm_first_kernel, grid=(m // tr,), in_specs=[row, vec], out_specs=row,
        out_shape=jax.ShapeDtypeStruct((m, d), BF16),
        compiler_params=_cparams("parallel"), name="norm_first",
    )(h, gpre.reshape(1, d))


def _resnorm(h, br, gpost, gpre):
    m, d = h.shape
    tr = _tile(m, NORM_ROWS)
    row = pl.BlockSpec((tr, d), lambda i: (i, 0))
    vec = pl.BlockSpec((1, d), lambda i: (0, 0))
    return pl.pallas_call(
        _resnorm_kernel, grid=(m // tr,), in_specs=[row, row, vec, vec], out_specs=[row, row],
        out_shape=[jax.ShapeDtypeStruct((m, d), F32), jax.ShapeDtypeStruct((m, d), BF16)],
        compiler_params=_cparams("parallel"), name="resnorm",
    )(h, br, gpost.reshape(1, d), gpre.reshape(1, d))


def _resnorm_last(h, br, gpost):
    m, d = h.shape
    tr = _tile(m, NORM_ROWS)
    row = pl.BlockSpec((tr, d), lambda i: (i, 0))
    vec = pl.BlockSpec((1, d), lambda i: (0, 0))
    return pl.pallas_call(
        _resnorm_last_kernel, grid=(m // tr,), in_specs=[row, row, vec], out_specs=row,
        out_shape=jax.ShapeDtypeStruct((m, d), F32),
        compiler_params=_cparams("parallel"), name="resnorm_last",
    )(h, br, gpost.reshape(1, d))


def _mm_in_kernel(x_ref, w_ref, u_ref):
    u_ref[...] = _dot_nt(x_ref[...], w_ref[...].astype(BF16)).astype(BF16)


def _mm_in(hn, w_in_t, layer):
    m, d = hn.shape
    tm = _tile(m, 1024)
    tn = 512
    assert UC0 % tn == 0
    return pl.pallas_call(
        _mm_in_kernel, grid=(m // tm, UC0 // tn),
        in_specs=[pl.BlockSpec((tm, d), lambda i, j: (i, 0)),
                  pl.BlockSpec((None, tn, d), lambda i, j: (layer, j, 0))],
        out_specs=pl.BlockSpec((tm, tn), lambda i, j: (i, j)),
        out_shape=jax.ShapeDtypeStruct((m, UC0), BF16),
        compiler_params=_cparams("parallel", "arbitrary"), name="mm_in",
    )(hn, w_in_t)


def _mm_tail_kernel(x_ref, w_ref, wp_ref, sc_ref, z_ref, oc_ref, carry_ref, *, tr, tiles_per_seq):
    ti = pl.program_id(0) % tiles_per_seq

    @pl.when(ti == 0)
    def _():
        carry_ref[...] = jnp.zeros_like(carry_ref)

    x = x_ref[...]
    z_ref[...] = _dot_nt(x, w_ref[0:LANES, :].astype(BF16))
    uc = _dot_nt(x, w_ref[GATE_RANK:GATE_RANK + C_WIDTH, :].astype(BF16))
    _pool_rows(uc, ti, tr, wp_ref, sc_ref, oc_ref, carry_ref)


def _mm_tail(hn, w_in_t, layer, w_pool, pool_scale, seq):
    m, d = hn.shape
    tm = _tile(seq, 512)
    rows = TAIL_ROWS
    assert UC0 % rows == 0 and rows >= GATE_RANK + C_WIDTH and GATE_RANK % SUBLANES == 0
    assert tm >= POOL_HALO and max(POOL_WINDOWS) <= POOL_HALO
    return pl.pallas_call(
        functools.partial(_mm_tail_kernel, tr=tm, tiles_per_seq=seq // tm), grid=(m // tm,),
        in_specs=[pl.BlockSpec((tm, d), lambda i: (i, 0)),
                  pl.BlockSpec((None, rows, d), lambda i: (layer, UC0 // rows, 0),
                               pipeline_mode=pl.Buffered(1)),
                  pl.BlockSpec((N_POOL, POOL_CH, POOL_CH), lambda i: (0, 0, 0)),
                  pl.BlockSpec((1, C_WIDTH), lambda i: (0, 0))],
        out_specs=[pl.BlockSpec((tm, LANES), lambda i: (i, 0)),
                   pl.BlockSpec((tm, C_WIDTH), lambda i: (i, 0))],
        out_shape=[jax.ShapeDtypeStruct((m, LANES), F32), jax.ShapeDtypeStruct((m, C_WIDTH), BF16)],
        scratch_shapes=[pltpu.VMEM((POOL_HALO, C_WIDTH), F32)],
        compiler_params=_cparams("arbitrary"), name="mm_tail",
    )(hn, w_in_t, w_pool, pool_scale.reshape(1, C_WIDTH))


def _att_kernel(lq1_ref, lk1_ref, lq2_ref, lk2_ref, g_ref, q_ref, k_ref, v_ref, o_ref,
                s1_ref, s2_ref, acc1_ref, acc2_ref, bias_ref, stat_ref, *, lam_init, t, G):
    head = pl.program_id(1)
    qi = pl.program_id(2)
    qscale = LOG2E / math.sqrt(A_DH)
    slope = lax.bitcast_convert_type(
        jnp.full((1, 1), (126 - head) << 23, jnp.int32), F32) * LOG2E

    q = q_ref[...].astype(F32) * qscale
    qt1 = q[:, :A_DH].T.astype(BF16)
    qt2 = q[:, A_DH:].T.astype(BF16)

    def fold8(x, op):
        return op(x.reshape(x.shape[0] // SUBLANES, SUBLANES, t), axis=0)

    nt = G // t

    @pl.when(qi == 0)
    def _():
        cshift = CHUNK.bit_length() - 1
        kd = lax.broadcasted_iota(jnp.int32, (t, t), 0)
        rd = lax.broadcasted_iota(jnp.int32, (t, t), 1)
        corr = slope * (rd - jnp.abs(rd - kd) - kd).astype(F32)
        diag = jnp.where((kd >> cshift) <= (rd >> cshift), corr, NEG_BIG)
        for i in range(nt):
            base = slope * (kd + i * t).astype(F32)
            bias_ref[0, i * t:(i + 1) * t, :] = base
            for j in range(i, nt):
                bias_ref[1 + j, i * t:(i + 1) * t, :] = base + diag if i == j else base

    qs = qi * t
    gl = qs // G
    jd = (qs - gl * G) // t

    def offset(g):
        return slope * (g * G - qs).astype(F32)

    l0 = pl.multiple_of(gl * G, G)

    def scores(k0, rows, bias, off, carry):
        m1, m2 = carry
        kt = k_ref[pl.ds(k0, rows), :]
        s1 = _dot(kt[:, :A_DH], qt1) + bias
        s2 = _dot(kt[:, A_DH:], qt2) + bias
        s1_ref[pl.ds(k0, rows), :] = s1
        s2_ref[pl.ds(k0, rows), :] = s2
        return (jnp.maximum(m1, fold8(s1, jnp.max) + off), jnp.maximum(m2, fold8(s2, jnp.max) + off))

    def score_body(g, carry):
        return scores(pl.multiple_of(g * G, G), G, bias_ref[0], offset(g), carry)

    neg8 = jnp.full((SUBLANES, t), NEG_BIG, F32)
    m_groups = lax.fori_loop(0, gl, score_body, (neg8, neg8))
    for j in range(nt):
        @pl.when(jd == j)
        def _():
            rows = (j + 1) * t
            m1, m2 = scores(l0, rows, bias_ref[1 + j, 0:rows, :], offset(gl), m_groups)
            stat_ref[0] = m1
            stat_ref[1] = m2
    m1 = jnp.max(stat_ref[0], axis=0, keepdims=True)
    m2 = jnp.max(stat_ref[1], axis=0, keepdims=True)

    acc1_ref[...] = jnp.zeros_like(acc1_ref)
    acc2_ref[...] = jnp.zeros_like(acc2_ref)

    def probs(k0, rows, off, carry):
        l1, l2 = carry
        vt = v_ref[pl.ds(k0, rows), :]
        p1 = jnp.exp2(s1_ref[pl.ds(k0, rows), :] - (m1 - off))
        p2 = jnp.exp2(s2_ref[pl.ds(k0, rows), :] - (m2 - off))
        acc1_ref[...] += _dot_tn(vt, p1.astype(BF16))
        acc2_ref[...] += _dot_tn(vt, p2.astype(BF16))
        return l1 + fold8(p1, jnp.sum), l2 + fold8(p2, jnp.sum)

    def pv_body(g, carry):
        return probs(pl.multiple_of(g * G, G), G, offset(g), carry)

    zero8 = jnp.zeros((SUBLANES, t), F32)
    l_groups = lax.fori_loop(0, gl, pv_body, (zero8, zero8))
    for j in range(nt):
        @pl.when(jd == j)
        def _():
            l1, l2 = probs(l0, (j + 1) * t, offset(gl), l_groups)
            stat_ref[0] = l1
            stat_ref[1] = l2
    l1 = jnp.sum(stat_ref[0], axis=0, keepdims=True)
    l2 = jnp.sum(stat_ref[1], axis=0, keepdims=True)

    lam = (jnp.exp(jnp.sum(lq1_ref[...] * lk1_ref[...], axis=-1, keepdims=True))
           - jnp.exp(jnp.sum(lq2_ref[...] * lk2_ref[...], axis=-1, keepdims=True))
           + lam_init)
    ot = acc1_ref[...] * (1.0 / l1) - acc2_ref[...] * (lam / l2)
    ms = jnp.mean(ot * ot, axis=0, keepdims=True)
    o_ref[...] = (ot * lax.rsqrt(ms + EPS) * (g_ref[...] * (1.0 - lam_init))).astype(BF16)


def _att(u, lq1, lk1, lq2, lk2, g_subln, lam_init, batch, seq):
    m = u.shape[0]
    t = _tile(seq, ATT_TQ)
    G = _tile(seq, ATT_GROUP)
    nq = seq // t
    e = 2 * A_DH
    assert G % t == 0
    vec = pl.BlockSpec((1, A_DH), lambda b, h, i: (0, 0))
    return pl.pallas_call(
        functools.partial(_att_kernel, lam_init=lam_init, t=t, G=G),
        grid=(batch, A_HEADS, nq),
        in_specs=[vec, vec, vec, vec,
                  pl.BlockSpec((e, 1), lambda b, h, i: (0, 0)),
                  pl.BlockSpec((t, e), lambda b, h, i: (b * nq + i, QA0 // e + h)),
                  pl.BlockSpec((seq, e), lambda b, h, i: (b, KA0 // e + h)),
                  pl.BlockSpec((seq, e), lambda b, h, i: (b, VA0 // e + h))],
        out_specs=pl.BlockSpec((e, t), lambda b, h, i: (h, b * nq + i)),
        out_shape=jax.ShapeDtypeStruct((A_WIDTH, m), BF16),
        scratch_shapes=[pltpu.VMEM((seq, t), F32), pltpu.VMEM((seq, t), F32),
                        pltpu.VMEM((e, t), F32), pltpu.VMEM((e, t), F32),
                        pltpu.VMEM((1 + G // t, G, t), F32), pltpu.VMEM((2, SUBLANES, t), F32)],
        compiler_params=_cparams("arbitrary", "arbitrary", "arbitrary"), name="diff_attn",
    )(lq1.reshape(1, A_DH), lk1.reshape(1, A_DH), lq2.reshape(1, A_DH), lk2.reshape(1, A_DH),
      g_subln.reshape(e, 1), u, u, u)


def _gla_kernel(z_ref, wg_ref, bg_ref, gn_ref, q_ref, k_ref, v_ref, gb_ref, o_ref, *, seq):
    L = CHUNK
    wg = wg_ref[...]
    bg = bg_ref[...]
    gn = gn_ref[...]
    rows = lax.broadcasted_iota(jnp.int32, (L, B_DK), 0)
    ti = lax.broadcasted_iota(jnp.int32, (L, L), 0)
    si = lax.broadcasted_iota(jnp.int32, (L, L), 1)
    SUB = GLA_SUB
    assert L == 4 * SUB
    sshift = SUB.bit_length() - 1
    bi = ti >> sshift
    bj = si >> sshift
    sd = si - (ti - (ti & (SUB - 1)))
    m_diag = (bi == bj) & (si <= ti)
    m_near = (bi == bj + 1) & ((bi & 1) == 1)
    m_far = (bi >= 2) & (bj <= 1)
    qscale = B_DK ** -0.5

    def chunk(c, state_t):
        r0 = pl.multiple_of(c * L, L)
        x = _dot(z_ref[pl.ds(r0, L), :].astype(BF16), wg) + bg
        la = (jnp.minimum(x, 0.0) - jnp.log(1.0 + jnp.exp(-jnp.abs(x)))) * (1.0 / GATE_TAU)
        b = la
        sh = 1
        while sh < L:
            b = b + jnp.where(rows >= sh, pltpu.roll(b, sh, axis=0), 0.0)
            sh *= 2
        b = b * LOG2E
        b_last = b[L - 1:L, :]
        qf = q_ref[pl.ds(r0, L), :].astype(F32) * qscale
        kc = k_ref[pl.ds(r0, L), :]
        kf = kc.astype(F32)
        vc = v_ref[pl.ds(r0, L), :]

        def rescaled(ref):
            qr = (qf * jnp.exp2(jnp.minimum(b - ref, 0.0))).astype(BF16)
            kr = (kf * jnp.exp2(jnp.minimum(ref - b, 0.0))).astype(BF16)
            return _dot_nt(qr, kr)

        half = L // 2
        a_far = rescaled(b[half - 1:half, :])
        ref_near = jnp.where(rows < half, b[SUB - 1:SUB, :], b[half + SUB - 1:half + SUB, :])
        a_near = rescaled(ref_near)
        a = jnp.zeros((L, L), F32)
        for s in range(SUB):
            ref = jnp.concatenate(
                [jnp.broadcast_to(b[i * SUB + s:i * SUB + s + 1, :], (SUB, B_DK)) for i in range(L // SUB)],
                axis=0)
            e = jnp.exp2(jnp.minimum(b - ref, 0.0))
            rs = _dot_nt((qf * e).astype(BF16), kc)
            a = jnp.where(sd == s, rs, a)
        a = jnp.where(m_far, a_far, jnp.where(m_near, a_near, jnp.where(m_diag, a, 0.0)))

        qe = (qf * jnp.exp2(b)).astype(BF16)
        o = _dot(a.astype(BF16), vc) + _dot_nt(qe, state_t.astype(BF16))
        ke = (kf * jnp.exp2(b_last - b)).astype(BF16)
        new_state = jnp.exp2(b_last) * state_t + _dot_tn(vc, ke)

        gate = gb_ref[pl.ds(r0, L), :].astype(F32)
        y = _rms(o, gn) * (gate / (1.0 + jnp.exp(-gate)))
        o_ref[pl.ds(r0, L), :] = y.astype(BF16)
        return new_state

    nc = seq // L
    unroll = GLA_UNROLL if nc % GLA_UNROLL == 0 else 1

    def chunks(c, state_t):
        for k in range(unroll):
            state_t = chunk(c * unroll + k, state_t)
        return state_t

    lax.fori_loop(0, nc // unroll, chunks, jnp.zeros((B_DV, B_DK), F32))


def _gla(u, z, w_gate, b_gate, g_gla, batch, seq):
    m = u.shape[0]
    return pl.pallas_call(
        functools.partial(_gla_kernel, seq=seq),
        grid=(batch, B_HEADS),
        in_specs=[pl.BlockSpec((seq, LANES), lambda b, h: (b, 0)),
                  pl.BlockSpec((LANES, B_DK), lambda b, h: (0, h)),
                  pl.BlockSpec((1, B_DK), lambda b, h: (0, h)),
                  pl.BlockSpec((1, B_DV), lambda b, h: (0, h)),
                  pl.BlockSpec((seq, B_DK), lambda b, h: (b, QB0 // B_DK + h)),
                  pl.BlockSpec((seq, B_DK), lambda b, h: (b, KB0 // B_DK + h)),
                  pl.BlockSpec((seq, B_DV), lambda b, h: (b, VB0 // B_DV + h)),
                  pl.BlockSpec((seq, B_DV), lambda b, h: (b, GB0 // B_DV + h))],
        out_specs=pl.BlockSpec((seq, B_DV), lambda b, h: (b, h)),
        out_shape=jax.ShapeDtypeStruct((m, B_WIDTH), BF16),
        compiler_params=_cparams("parallel", "parallel"), name="gla",
    )(z, w_gate, b_gate.reshape(1, -1), g_gla.reshape(1, -1), u, u, u, u)


def _pool_rows(x, ti, tr, w_ref, sc_ref, o_ref, carry_ref):
    ext = jnp.concatenate([carry_ref[...], x], axis=0)
    carry_ref[...] = x[tr - POOL_HALO:, :]
    pos = ti * tr + lax.broadcasted_iota(jnp.int32, (tr, 1), 0)
    for g, w in enumerate(POOL_WINDOWS):
        lo, hi = g * POOL_CH, (g + 1) * POOL_CH
        s = ext[:, lo:hi]
        sh = 1
        while sh < w:
            s = s + pltpu.roll(s, sh, axis=0)
            sh *= 2
        cnt = jnp.minimum(pos + 1, w).astype(F32)
        d = s[POOL_HALO:, :] / cnt - x[:, lo:hi]
        y = _dot(d.astype(BF16), w_ref[g]) * sc_ref[:, lo:hi]
        o_ref[:, lo:hi] = y.astype(BF16)


def _mm_out_kernel(a_ref, b_ref, c_ref, wa_ref, wb_ref, wc_ref, o_ref):
    o_ref[...] = (_dot_tn(a_ref[...], wa_ref[...].astype(BF16))
                  + _dot(b_ref[...], wb_ref[...].astype(BF16))
                  + _dot(c_ref[...], wc_ref[...].astype(BF16))).astype(o_ref.dtype)


def _mm_out(oa, ob, oc, w_o, layer):
    m = ob.shape[0]
    n = w_o.shape[2]
    tm = _tile(m, 1024)
    tn = _tile(n, 512)
    assert A_WIDTH % B_WIDTH == 0 and B_WIDTH == C_WIDTH
    return pl.pallas_call(
        _mm_out_kernel, grid=(m // tm, n // tn),
        in_specs=[pl.BlockSpec((A_WIDTH, tm), lambda i, j: (0, i)),
                  pl.BlockSpec((tm, B_WIDTH), lambda i, j: (i, 0)),
                  pl.BlockSpec((tm, C_WIDTH), lambda i, j: (i, 0)),
                  pl.BlockSpec((None, A_WIDTH, tn), lambda i, j: (layer, 0, j)),
                  pl.BlockSpec((None, B_WIDTH, tn), lambda i, j: (layer, A_WIDTH // B_WIDTH, j)),
                  pl.BlockSpec((None, C_WIDTH, tn), lambda i, j: (layer, A_WIDTH // C_WIDTH + 1, j))],
        out_specs=pl.BlockSpec((tm, tn), lambda i, j: (i, j)),
        out_shape=jax.ShapeDtypeStruct((m, n), BRANCH_DTYPE),
        compiler_params=_cparams("parallel", "parallel"), name="mm_out",
    )(oa, ob, oc, w_o, w_o, w_o)


GELU_C = math.sqrt(2.0 / math.pi)


def _mm_up_kernel(x_ref, wg_ref, wv_ref, cg_ref, cv_ref, bg_ref, bv_ref, o_ref,
                  tail_ref, yg_ref, yv_ref, *, tm, nj, tiles_per_seq):
    i = pl.program_id(0)
    j = pl.program_id(1)
    seq_start = (i % tiles_per_seq) == 0
    pad = SUBLANES

    @pl.when(seq_start)
    def _():
        tail_ref[j] = jnp.zeros((pad, tail_ref.shape[2]), F32)
        tail_ref[nj + j] = jnp.zeros((pad, tail_ref.shape[2]), F32)

    rs0 = min(tm, UP_ROWS)
    pieces = [(r0, rs0) for r0 in range(0, tm, rs0)]
    npieces = len(pieces)
    wg = wg_ref[...].astype(BF16)
    wv = wv_ref[...].astype(BF16)
    yg_ref[0:pad, :] = tail_ref[j]
    yv_ref[0:pad, :] = tail_ref[nj + j]
    cg = cg_ref[...]
    cv = cv_ref[...] * 0.5
    bg = bg_ref[...]
    bv = bv_ref[...] * 0.5

    def matmuls(r, zero):
        r0, rs = pieces[r]
        x = x_ref[r0:r0 + rs, :]
        if zero is not None:
            xb = pltpu.bitcast(x, jnp.uint32)
            x = pltpu.bitcast(xb | jnp.tile(zero, (xb.shape[0] // SUBLANES, xb.shape[1] // LANES)), BF16)
        yg_ref[pad + r0:pad + r0 + rs, :] = _dot(x, wg)
        yv_ref[pad + r0:pad + r0 + rs, :] = _dot(x, wv)

    def conv(y_ref, cw, b, r0, rs):
        return (b + cw[2:3, :] * y_ref[pad + r0:pad + r0 + rs, :]
                + cw[1:2, :] * y_ref[pl.ds(pad + r0 - 1, rs), :]
                + cw[0:1, :] * y_ref[pl.ds(pad + r0 - 2, rs), :])

    def epilogue(r):
        r0, rs = pieces[r]
        zg = conv(yg_ref, cg, bg, r0, rs)
        zv = conv(yv_ref, cv, bv, r0, rs)
        t = jnp.tanh(zg * (GELU_C + (GELU_C * 0.044715) * (zg * zg)))
        out = ((zg + zg * t) * zv).astype(BF16)
        o_ref[r0:r0 + rs, :] = out
        bits = pltpu.bitcast(out, jnp.uint32)
        acc = bits[0:SUBLANES, :]
        for k in range(1, bits.shape[0] // SUBLANES):
            acc = acc | bits[SUBLANES * k:SUBLANES * (k + 1), :]
        red = acc[:, 0:LANES]
        for k in range(1, acc.shape[1] // LANES):
            red = red | acc[:, k * LANES:(k + 1) * LANES]
        return (red >> 16) >> 16

    lag = UP_LAG
    for r in range(min(lag, npieces)):
        matmuls(r, None)
    for r in range(npieces):
        zero = epilogue(r)
        if r + lag < npieces:
            matmuls(r + lag, zero)
    tail_ref[j] = yg_ref[tm:tm + pad, :]
    tail_ref[nj + j] = yv_ref[tm:tm + pad, :]


def _mm_up(hn, w_up, w_conv, b_conv, layer, seq):
    m, d = hn.shape
    f = w_up.shape[2] // 2
    tm = _tile(seq, UP_TM)
    tn = MXU_COLS if f % MXU_COLS == 0 else f
    nj = f // tn
    b3 = b_conv.reshape(b_conv.shape[0], 1, 2 * f)
    return pl.pallas_call(
        functools.partial(_mm_up_kernel, tm=tm, nj=nj, tiles_per_seq=seq // tm),
        grid=(m // tm, nj),
        in_specs=[pl.BlockSpec((tm, d), lambda i, j: (i, 0), pipeline_mode=pl.Buffered(1)),
                  pl.BlockSpec((None, d, tn), lambda i, j: (layer, 0, j)),
                  pl.BlockSpec((None, d, tn), lambda i, j: (layer, 0, nj + j)),
                  pl.BlockSpec((None, CONV_K, tn), lambda i, j: (layer, 0, j)),
                  pl.BlockSpec((None, CONV_K, tn), lambda i, j: (layer, 0, nj + j)),
                  pl.BlockSpec((None, 1, tn), lambda i, j: (layer, 0, j)),
                  pl.BlockSpec((None, 1, tn), lambda i, j: (layer, 0, nj + j))],
        out_specs=pl.BlockSpec((tm, tn), lambda i, j: (i, j)),
        out_shape=jax.ShapeDtypeStruct((m, f), BF16),
        scratch_shapes=[pltpu.VMEM((2 * nj, SUBLANES, tn), F32),
                        pltpu.VMEM((tm + SUBLANES, tn), F32), pltpu.VMEM((tm + SUBLANES, tn), F32)],
        compiler_params=_cparams("arbitrary", "arbitrary"), name="mm_up",
    )(hn, w_up, w_up, w_conv, w_conv, b3, b3)


def _mm_kernel(x_ref, w_ref, o_ref):
    o_ref[...] = _dot(x_ref[...], w_ref[...].astype(BF16)).astype(o_ref.dtype)


def _mm_down(x, w, layer):
    m, k = x.shape
    n = w.shape[2]
    tm = _tile(m, 1024)
    tn = _tile(n, MXU_COLS)
    return pl.pallas_call(
        _mm_kernel, grid=(m // tm, n // tn),
        in_specs=[pl.BlockSpec((tm, k), lambda i, j: (i, 0), pipeline_mode=pl.Buffered(1)),
                  pl.BlockSpec((None, k, tn), lambda i, j: (layer, 0, j))],
        out_specs=pl.BlockSpec((tm, tn), lambda i, j: (i, j)),
        out_shape=jax.ShapeDtypeStruct((m, n), BRANCH_DTYPE),
        compiler_params=_cparams("parallel", "parallel"), name="mm_down",
    )(x, w)


def kernel(x, w_in, w_gate_lr2, b_gate, lam_q1, lam_k1, lam_q2, lam_k2, g_subln, g_gla, w_pool, pool_scale, w_o, w_up, w_conv, b_conv, w_down, g_pre_mix, g_post_mix, g_pre_ffn, g_post_ffn):
    batch, seq, d = x.shape
    depth = w_in.shape[0]
    m = batch * seq
    assert seq % CHUNK == 0 and w_in.shape[2] == U_COLS + GATE_RANK
    assert UC0 == GB0 + B_WIDTH

    w_in_t = jnp.swapaxes(w_in, 1, 2)
    w_pool_b = w_pool.astype(BF16)

    h = x.reshape(m, d)
    branch = None
    for l in range(depth):
        w_gate = jnp.pad(w_gate_lr2[l], ((0, LANES - GATE_RANK), (0, 0))).astype(BF16)

        if l == 0:
            hn = _norm_first(h, g_pre_mix[0])
        else:
            h, hn = _resnorm(h, branch, g_post_ffn[l - 1], g_pre_mix[l])
        u = _mm_in(hn, w_in_t, l)
        z, oc = _mm_tail(hn, w_in_t, l, w_pool_b[l], pool_scale[l], seq)
        lam_init = 0.8 - 0.6 * math.exp(-0.3 * l)
        oa = _att(u, lam_q1[l], lam_k1[l], lam_q2[l], lam_k2[l], g_subln[l], lam_init, batch, seq)
        ob = _gla(u, z, w_gate, b_gate[l], g_gla[l], batch, seq)
        mix = _mm_out(oa, ob, oc, w_o, l)

        h, hn = _resnorm(h, mix, g_post_mix[l], g_pre_ffn[l])
        f_in = _mm_up(hn, w_up, w_conv, b_conv, l, seq)
        branch = _mm_down(f_in, w_down, l)
    h = _resnorm_last(h, branch, g_post_ffn[depth - 1])
    return h.reshape(batch, seq, d)
```
